```python
import jax, jax.numpy as jnp
from jax import lax
import numpy as np

D_MODEL = 2048
BATCH = 2
SEQ = 4096
DEPTH = 2
DEC_BATCH = 8
DEC_SEQ = 8
PAST_LEN = 16384
PAGE_SIZE = 128

N_MIXERS = 2
HEAD_A = 64
H_A = D_MODEL // HEAD_A
D_DECAY_LORA = max(32, int(round(1.8 * D_MODEL ** 0.5 / 32)) * 32)
D_AAA_LORA = max(32, int(round(1.8 * D_MODEL ** 0.5 / 32)) * 32)
D_GATE_LORA = max(32, int(round(0.6 * D_MODEL ** 0.8 / 32)) * 32)
GN_EPS = 64e-5
GROUPS = ((128, 1), (512, 4), (2048, 16))
N_GROUPS = len(GROUPS)
H_B = 16
HD_B = 64
Q_BLOCK = 128
D_FF = 4 * D_MODEL
RMS_EPS = 1e-6

kernel_name = 'rwkv7_dilated_swa_hybrid_step'


def rmsnorm(x, g):
    xf = x.astype(jnp.float32)
    y = xf * lax.rsqrt(jnp.mean(xf * xf, axis=-1, keepdims=True) + RMS_EPS)
    return (y * g.astype(jnp.float32)).astype(x.dtype)


def sqrelu_mlp(x, w_up, w_down):
    return jnp.square(jax.nn.relu(x @ w_up)) @ w_down


def wkv7_scan(r, decay, k, v, a, b, s0):
    def step(S, inp):
        r_t, w_t, k_t, v_t, a_t, b_t = inp
        sa = jnp.einsum('bhij,bhj->bhi', S, a_t)
        S = S * w_t[:, :, None, :] + sa[..., None] * b_t[:, :, None, :] + v_t[..., None] * k_t[:, :, None, :]
        return S, jnp.einsum('bhij,bhj->bhi', S, r_t)
    xs = tuple(jnp.moveaxis(z, 1, 0) for z in (r, decay, k, v, a, b))
    S, ys = lax.scan(step, s0, xs)
    return jnp.moveaxis(ys, 0, 1), S


def rwkv7_time_mix(h, shift0, wkv0, mu, w0, w1, w2, a0, a1, a2, g1, g2, k_k, k_a, r_k,
                   lnx_w, lnx_b, w_r, w_k, w_v, w_o):
    B, T, D = h.shape
    f32 = jnp.float32
    prev = jnp.concatenate([shift0[:, None, :].astype(h.dtype), h[:, :-1]], axis=1)
    xx = prev - h
    xr, xw, xk, xv, xa, xg = (h + xx * mu[i] for i in range(6))
    r = xr @ w_r
    k = xk @ w_k
    v = xv @ w_v
    w_log = -jax.nn.softplus(-(w0 + jnp.tanh(xw @ w1) @ w2)) - 0.5
    a = jax.nn.sigmoid(a0 + (xa @ a1) @ a2)
    g = jax.nn.sigmoid(xg @ g1) @ g2
    heads = lambda z: z.astype(f32).reshape(B, T, H_A, HEAD_A)
    kk = heads(k * k_k)
    kk = kk / jnp.maximum(jnp.sqrt(jnp.sum(kk * kk, axis=-1, keepdims=True)), 1e-12)
    k = k * (1 + (a - 1) * k_a)
    rh, kh, vh, ah = heads(r), heads(k), heads(v), heads(a)
    decay = jnp.exp(-jnp.exp(heads(w_log)))
    y, wkv = wkv7_scan(rh, decay, kh, vh, -kk, kk * ah, wkv0.astype(f32))
    mean = jnp.mean(y, axis=-1, keepdims=True)
    var = jnp.mean(jnp.square(y - mean), axis=-1, keepdims=True)
    y = (y - mean) * lax.rsqrt(var + GN_EPS)
    y = y * lnx_w.astype(f32).reshape(H_A, HEAD_A) + lnx_b.astype(f32).reshape(H_A, HEAD_A)
    y = y + jnp.sum(rh * kh * r_k.astype(f32), axis=-1, keepdims=True) * vh
    out = (y.reshape(B, T, D).astype(h.dtype) * g) @ w_o
    return out, wkv.astype(wkv0.dtype), h[:, -1]


def alibi_slopes():
    n = N_GROUPS * H_B
    return (2.0 ** (-8.0 * jnp.arange(1, n + 1, dtype=jnp.float32) / n)).reshape(N_GROUPS, H_B)


def merge_groups(stats):
    m = jnp.stack([s[0] for s in stats])
    l = jnp.stack([s[1] for s in stats])
    acc = jnp.stack([s[2] for s in stats])
    wgt = jnp.exp(m - jnp.max(m, axis=0))
    return jnp.sum(wgt[..., None] * acc, axis=0) / jnp.sum(wgt * l, axis=0)[..., None]


def dilated_attention_prompt(h, w_qkv, w_o):
    B, T, D = h.shape
    f32 = jnp.float32
    scale = HD_B ** -0.5
    slopes = alibi_slopes()
    qkv = (h @ w_qkv).reshape(B, T, N_GROUPS, 3, H_B, HD_B)
    n_blocks = T // Q_BLOCK
    q_blocks, kv_pads, new_bufs = [], [], []
    for g, (window, dil) in enumerate(GROUPS):
        kv = qkv[:, :, g, 1:]
        new_bufs.append(kv[:, T - min(window, T):])
        kv_pads.append(jnp.pad(kv, ((0, 0), (window, 0), (0, 0), (0, 0), (0, 0))))
        q_blocks.append(jnp.moveaxis(qkv[:, :, g, 0].reshape(B, n_blocks, Q_BLOCK, H_B, HD_B), 1, 0))

    def block(args):
        blk, qbs = args
        start = blk * Q_BLOCK
        stats = []
        for g, (window, dil) in enumerate(GROUPS):
            n = window // dil
            A = Q_BLOCK // dil
            S = n + A
            q = qbs[g].reshape(B, A, dil, H_B, HD_B).astype(f32)
            region = lax.dynamic_slice_in_dim(kv_pads[g], start, window + Q_BLOCK, axis=1)
            region = region.reshape(B, S, dil, 2, H_B, HD_B).astype(f32)
            steps = n + jnp.arange(A)[:, None] - jnp.arange(S)[None, :]
            key_pos = start - window + jnp.arange(S)[None, :] * dil + jnp.arange(dil)[:, None]
            valid = ((steps >= 0) & (steps <= n))[None] & (key_pos >= 0)[:, None, :]
            sc = jnp.einsum('barhd,bsrhd->brhas', q, region[:, :, :, 0]) * scale
            sc = sc - slopes[g][None, None, :, None, None] * (steps * dil).astype(f32)
            sc = jnp.where(valid[None, :, None], sc, -jnp.inf)
            m = jnp.max(sc, axis=-1)
            p = jnp.exp(sc - m[..., None])
            l = jnp.sum(p, axis=-1)
            acc = jnp.einsum('brhas,bsrhd->barhd', p, region[:, :, :, 1])
            m = jnp.transpose(m, (0, 3, 1, 2)).reshape(B, Q_BLOCK, H_B)
            l = jnp.transpose(l, (0, 3, 1, 2)).reshape(B, Q_BLOCK, H_B)
            stats.append((m, l, acc.reshape(B, Q_BLOCK, H_B, HD_B)))
        return merge_groups(stats)

    out = lax.map(block, (jnp.arange(n_blocks), tuple(q_blocks)))
    out = jnp.moveaxis(out, 0, 1).reshape(B, T, H_B * HD_B).astype(h.dtype)
    return out @ w_o, new_bufs


def dilated_attention_sample(h, bufs, w_qkv, w_o):
    B, S, D = h.shape
    f32 = jnp.float32
    scale = HD_B ** -0.5
    slopes = alibi_slopes()
    qkv = (h @ w_qkv).reshape(B, S, N_GROUPS, 3, H_B, HD_B)
    j = jnp.arange(S)
    stats, new_bufs = [], []
    for g, (window, dil) in enumerate(GROUPS):
        n = window // dil
        q = qkv[:, :, g, 0].astype(f32)
        buf = bufs[g].astype(h.dtype)
        Lb = buf.shape[1]
        kv_all = jnp.concatenate([buf, qkv[:, :, g, 1:]], axis=1)
        new_bufs.append(kv_all[:, S:])
        kv_ext = jnp.pad(kv_all, ((0, 0), (window - Lb, 0), (0, 0), (0, 0), (0, 0)))
        steps = jnp.arange(n + 1)
        idx = window + j[:, None] - steps[None, :] * dil
        kg = kv_ext[:, idx].astype(f32)
        valid = (PAST_LEN - window + idx) >= 0
        sc = jnp.einsum('bshd,bskhd->bhsk', q, kg[:, :, :, 0]) * scale
        sc = sc - slopes[g][None, :, None, None] * (steps * dil).astype(f32)[None, None, None, :]
        sc = jnp.where(valid[None, None], sc, -jnp.inf)
        m = jnp.max(sc, axis=-1)
        p = jnp.exp(sc - m[..., None])
        l = jnp.sum(p, axis=-1)
        acc = jnp.einsum('bhsk,bskhd->bshd', p, kg[:, :, :, 1])
        stats.append((jnp.transpose(m, (0, 2, 1)), jnp.transpose(l, (0, 2, 1)), acc))
    out = merge_groups(stats).reshape(B, S, H_B * HD_B).astype(h.dtype)
    return out @ w_o, new_bufs


def setup_inputs(seed: int = 0) -> dict:
    key = jax.random.key(seed)
    ks = jax.random.split(key, 40)
    f32 = jnp.float32
    nrm = lambda k, shape, s: jax.random.normal(k, shape, f32) * s
    D = D_MODEL
    W_ATT = N_GROUPS * 3 * H_B * HD_B
    return {
        'x_prompt': nrm(ks[0], (BATCH, SEQ, D), 1.0),
        'x_sample': nrm(ks[1], (DEC_BATCH, DEC_SEQ, D), 1.0),
        'state_wkv': nrm(ks[2], (DEC_BATCH, H_A, HEAD_A, HEAD_A), 0.3),
        'state_shift': nrm(ks[3], (DEC_BATCH, D), 1.0),
        'cache_kv_g1': nrm(ks[4], (DEC_BATCH, min(GROUPS[0][0], PAST_LEN), 2, H_B, HD_B), 1.0),
        'cache_kv_g2': nrm(ks[5], (DEC_BATCH, min(GROUPS[1][0], PAST_LEN), 2, H_B, HD_B), 1.0),
        'cache_kv_g3': nrm(ks[6], (DEC_BATCH, min(GROUPS[2][0], PAST_LEN), 2, H_B, HD_B), 1.0),
        'norm_mix': 1.0 + nrm(ks[7], (DEPTH, D), 0.05),
        'norm_ffn': 1.0 + nrm(ks[8], (DEPTH, D), 0.05),
        'norm_final': 1.0 + nrm(ks[9], (D,), 0.05),
        'rwkv_mu': jax.random.uniform(ks[10], (6, D), f32),
        'rwkv_w0': jax.random.uniform(ks[11], (D,), f32, -6.0, -0.5),
        'rwkv_w1': nrm(ks[12], (D, D_DECAY_LORA), D ** -0.5),
        'rwkv_w2': nrm(ks[13], (D_DECAY_LORA, D), 0.5 * D_DECAY_LORA ** -0.5),
        'rwkv_a0': nrm(ks[14], (D,), 0.3),
        'rwkv_a1': nrm(ks[15], (D, D_AAA_LORA), D ** -0.5),
        'rwkv_a2': nrm(ks[16], (D_AAA_LORA, D), D_AAA_LORA ** -0.5),
        'rwkv_g1': nrm(ks[17], (D, D_GATE_LORA), D ** -0.5),
        'rwkv_g2': nrm(ks[18], (D_GATE_LORA, D), D_GATE_LORA ** -0.5),
        'rwkv_k_k': 0.85 + nrm(ks[19], (D,), 0.05),
        'rwkv_k_a': 1.0 + nrm(ks[20], (D,), 0.05),
        'rwkv_r_k': nrm(ks[21], (H_A, HEAD_A), 0.1),
        'rwkv_lnx_w': 1.0 + nrm(ks[22], (D,), 0.05),
        'rwkv_lnx_b': nrm(ks[23], (D,), 0.01),
        'rwkv_w_r': nrm(ks[24], (D, D), D ** -0.5),
        'rwkv_w_k': nrm(ks[25], (D, D), D ** -0.5),
        'rwkv_w_v': nrm(ks[26], (D, D), D ** -0.5),
        'rwkv_w_o': nrm(ks[27], (D, D), D ** -0.5),
        'attn_w_qkv': nrm(ks[28], (D, W_ATT), D ** -0.5),
        'attn_w_o': nrm(ks[29], (H_B * HD_B, D), (H_B * HD_B) ** -0.5),
        'ffn_w_up': nrm(ks[30], (DEPTH, D, D_FF), D ** -0.5),
        'ffn_w_down': nrm(ks[31], (DEPTH, D_FF, D), D_FF ** -0.5),
    }


def reference(x_prompt, x_sample, state_wkv, state_shift, cache_kv_g1, cache_kv_g2, cache_kv_g3,
              norm_mix, norm_ffn, norm_final,
              rwkv_mu, rwkv_w0, rwkv_w1, rwkv_w2, rwkv_a0, rwkv_a1, rwkv_a2, rwkv_g1, rwkv_g2,
              rwkv_k_k, rwkv_k_a, rwkv_r_k, rwkv_lnx_w, rwkv_lnx_b, rwkv_w_r, rwkv_w_k, rwkv_w_v, rwkv_w_o,
              attn_w_qkv, attn_w_o, ffn_w_up, ffn_w_down):
    rwkv_params = (rwkv_mu, rwkv_w0, rwkv_w1, rwkv_w2, rwkv_a0, rwkv_a1, rwkv_a2, rwkv_g1, rwkv_g2,
                   rwkv_k_k, rwkv_k_a, rwkv_r_k, rwkv_lnx_w, rwkv_lnx_b, rwkv_w_r, rwkv_w_k, rwkv_w_v, rwkv_w_o)
    xp, xs = x_prompt, x_sample
    Bp = x_prompt.shape[0]
    for i in range(DEPTH):
        hp = rmsnorm(xp, norm_mix[i])
        hs = rmsnorm(xs, norm_mix[i])
        if i % N_MIXERS == 0:
            shift0 = jnp.zeros((Bp, D_MODEL), hp.dtype)
            wkv0 = jnp.zeros((Bp, H_A, HEAD_A, HEAD_A), hp.dtype)
            op, wkv_p, shift_p = rwkv7_time_mix(hp, shift0, wkv0, *rwkv_params)
            os_, wkv_s, shift_s = rwkv7_time_mix(hs, state_shift, state_wkv, *rwkv_params)
        else:
            op, (kv1_p, kv2_p, kv3_p) = dilated_attention_prompt(hp, attn_w_qkv, attn_w_o)
            os_, (kv1_s, kv2_s, kv3_s) = dilated_attention_sample(
                hs, (cache_kv_g1, cache_kv_g2, cache_kv_g3), attn_w_qkv, attn_w_o)
        xp = xp + op
        xs = xs + os_
        xp = xp + sqrelu_mlp(rmsnorm(xp, norm_ffn[i]), ffn_w_up[i], ffn_w_down[i])
        xs = xs + sqrelu_mlp(rmsnorm(xs, norm_ffn[i]), ffn_w_up[i], ffn_w_down[i])
    y_prompt = rmsnorm(xp, norm_final)
    y_sample = rmsnorm(xs, norm_final)
    return (y_prompt, y_sample, wkv_p, shift_p, kv1_p, kv2_p, kv3_p, wkv_s, shift_s, kv1_s, kv2_s, kv3_s)
```

```python
import functools

import jax
import jax.numpy as jnp
from jax import lax
from jax.experimental import pallas as pl
from jax.experimental.pallas import tpu as pltpu

F32 = jnp.float32
BF16 = jnp.bfloat16

D_MODEL = 2048
HEAD_A = 64
H_A = D_MODEL // HEAD_A
GN_EPS = 64e-5
GROUPS = ((128, 1), (512, 4), (2048, 16))
N_GROUPS = len(GROUPS)
H_B = 16
HD_B = 64
ATT_STEPS = 128
RMS_EPS = 1e-6

LANES = 128
VMEM_LIMIT = 56 * 1024 * 1024


def _bdot(a, b):
    return jnp.dot(a.astype(BF16), b.astype(BF16), preferred_element_type=F32)


def _bdot_nt(a, b):
    return lax.dot_general(a.astype(BF16), b.astype(BF16), (((1,), (1,)), ((), ())),
                           preferred_element_type=F32)


def _bdot_tn(a, b):
    return lax.dot_general(a.astype(BF16), b.astype(BF16), (((0,), (0,)), ((), ())),
                           preferred_element_type=F32)


def _exact_dot(m, x):
    hi = x.astype(BF16)
    r1 = x - hi.astype(F32)
    mid = r1.astype(BF16)
    lo = (r1 - mid.astype(F32)).astype(BF16)
    mb = m.astype(BF16)
    acc = jnp.dot(mb, hi, preferred_element_type=F32)
    acc = acc + jnp.dot(mb, mid, preferred_element_type=F32)
    return acc + jnp.dot(mb, lo, preferred_element_type=F32)


def _head_sum(x, lane_lo):
    s0 = jnp.sum(jnp.where(lane_lo, x, 0.0), axis=-1, keepdims=True)
    s1 = jnp.sum(jnp.where(lane_lo, 0.0, x), axis=-1, keepdims=True)
    return jnp.where(lane_lo, s0, s1)


def _wkv_kernel(r_ref, k_ref, v_ref, wl_ref, a_ref, g_ref, kk_ref, ka_ref, rk_ref, lw_ref, lb_ref,
                s0_ref, y_ref, sout_ref, s_scr, *, L):
    c = pl.program_id(2)

    @pl.when(c == 0)
    def _():
        s_scr[...] = s0_ref[0, 0]

    lane = lax.broadcasted_iota(jnp.int32, (L, LANES), 1)
    lane_lo = lane < HEAD_A

    r = r_ref[0]
    kraw = k_ref[0]
    v = v_ref[0]
    a = a_ref[0]
    logd = -jnp.exp(wl_ref[0])

    kkraw = kraw * kk_ref[...]
    kk = kkraw / jnp.maximum(jnp.sqrt(_head_sum(kkraw * kkraw, lane_lo)), 1e-12)
    k2 = kraw * (1.0 + (a - 1.0) * ka_ref[...])

    row = lax.broadcasted_iota(jnp.int32, (L, L), 0)
    col = lax.broadcasted_iota(jnp.int32, (L, L), 1)
    cum = _exact_dot((col <= row).astype(F32), logd)
    w_incl = jnp.exp(cum)
    w_excl = jnp.exp(cum - logd)
    w_inv = jnp.exp(-cum)
    w_last = w_incl[L - 1:L, :]

    at = -kk * w_excl
    bt = (kk * a) * w_inv
    kt = k2 * w_inv
    rt = r * w_incl

    s2 = s_scr[...]
    ar = jnp.concatenate([at, rt], axis=0)
    bk = jnp.concatenate([bt, kt], axis=0)
    kb = jnp.concatenate([kt, bt], axis=0)
    lane2 = lax.broadcasted_iota(jnp.int32, (2 * L, LANES), 1)
    lo2 = lane2 < HEAD_A
    m0 = _bdot_nt(jnp.where(lo2, ar, 0.0), bk)
    m1 = _bdot_nt(jnp.where(lo2, 0.0, ar), kb)
    aprp = _bdot_nt(ar, s2)

    tt = lax.broadcasted_iota(jnp.int32, (L, 2 * L), 0)
    ss = lax.broadcasted_iota(jnp.int32, (L, 2 * L), 1) % L
    strict = ss < tt
    incl = ss <= tt
    half_lo = lax.broadcasted_iota(jnp.int32, (L, 2 * L), 1) < L

    a0 = m0[:L]
    a1 = m1[:L]
    n_top = jnp.where(strict & half_lo, a0, 0.0)
    n_bot = jnp.where(strict & (~half_lo), a1, 0.0)
    n2 = jnp.concatenate([n_top, n_bot], axis=0)

    r2 = lax.broadcasted_iota(jnp.int32, (2 * L, 2 * L), 0)
    c2 = lax.broadcasted_iota(jnp.int32, (2 * L, 2 * L), 1)
    t2 = (r2 == c2).astype(F32) + jnp.where((r2 // 2 == c2 // 2) & (c2 < r2), n2, 0.0)
    size = 2
    while size < L:
        blk = (r2 // (2 * size) == c2 // (2 * size)) & ((r2 // size) % 2 == 1) & ((c2 // size) % 2 == 0)
        cmat = jnp.where(blk, n2, 0.0)
        t2 = t2 + _bdot(_bdot(t2, cmat), t2)
        size *= 2
    tcat = t2[:L] + t2[L:]

    v_lo = jnp.where(lane_lo, v, 0.0)
    v_hi = jnp.where(lane_lo, 0.0, v)
    ak_cat = jnp.where(strict, jnp.where(half_lo, a1, a0), 0.0)
    rhs = aprp[:L] + _bdot(ak_cat, jnp.concatenate([v_hi, v_lo], axis=0))
    u = _bdot(tcat, jnp.concatenate([jnp.where(lane_lo, rhs, 0.0), jnp.where(lane_lo, 0.0, rhs)], axis=0))
    u_lo = jnp.where(lane_lo, u, 0.0)
    u_hi = jnp.where(lane_lo, 0.0, u)

    rb0 = jnp.where(incl, m0[L:], 0.0)
    rb1 = jnp.where(incl, m1[L:], 0.0)
    y = (aprp[L:] + _bdot(rb0, jnp.concatenate([u_lo, v_lo], axis=0))
         + _bdot(rb1, jnp.concatenate([v_hi, u_hi], axis=0)))

    delta = _bdot_tn(jnp.concatenate([u, v], axis=0), bk)
    bd = (r2 < HEAD_A) == (c2 < HEAD_A)
    s_new = (s2 + jnp.where(bd, delta, 0.0)) * w_last
    s_scr[...] = s_new

    mean = _head_sum(y, lane_lo) * (1.0 / HEAD_A)
    yc = y - mean
    var = _head_sum(yc * yc, lane_lo) * (1.0 / HEAD_A)
    yn = yc * lax.rsqrt(var + GN_EPS) * lw_ref[...] + lb_ref[...]
    bonus = _head_sum(r * k2 * rk_ref[...], lane_lo)
    y_ref[0] = (yn + bonus * v) * g_ref[0]

    @pl.when(c == pl.num_programs(2) - 1)
    def _():
        sout_ref[0, 0] = s_new


def _wkv_call(r, k, v, wlog, a, g, k_k, k_a, r_k, lnx_w, lnx_b, s2_0, *, L):
    B, T, D = r.shape
    assert T % L == 0 and L == HEAD_A and 2 * L == LANES
    n_pairs = D // LANES
    seq = pl.BlockSpec((1, L, LANES), lambda b, p, c: (b, c, p))
    par = pl.BlockSpec((1, LANES), lambda b, p, c: (0, p))
    st = pl.BlockSpec((1, 1, LANES, LANES), lambda b, p, c: (b, p, 0, 0))
    row = lambda x: x.reshape(1, D)
    return pl.pallas_call(
        functools.partial(_wkv_kernel, L=L),
        grid=(B, n_pairs, T // L),
        in_specs=[seq] * 6 + [par] * 5 + [st],
        out_specs=[seq, st],
        out_shape=[jax.ShapeDtypeStruct((B, T, D), F32),
                   jax.ShapeDtypeStruct((B, n_pairs, LANES, LANES), F32)],
        scratch_shapes=[pltpu.VMEM((LANES, LANES), F32)],
        compiler_params=pltpu.CompilerParams(
            dimension_semantics=("parallel", "parallel", "arbitrary")),
        name="wkv7_chunk_scan",
    )(r, k, v, wlog, a, g, row(k_k), row(k_a), row(r_k), row(lnx_w), row(lnx_b), s2_0)


def _rmsnorm_kernel(x_ref, g_ref, o_ref):
    x = x_ref[...]
    y = x * lax.rsqrt(jnp.mean(x * x, axis=-1, keepdims=True) + RMS_EPS)
    o_ref[...] = (y * g_ref[...]).astype(o_ref.dtype)


def _rmsnorm_call(x, g, out_dtype=F32):
    M, D = x.shape
    bm = min(M, 512)
    return pl.pallas_call(
        _rmsnorm_kernel,
        grid=(M // bm,),
        in_specs=[pl.BlockSpec((bm, D), lambda m: (m, 0)), pl.BlockSpec((1, D), lambda m: (0, 0))],
        out_specs=pl.BlockSpec((bm, D), lambda m: (m, 0)),
        out_shape=jax.ShapeDtypeStruct((M, D), out_dtype),
        compiler_params=pltpu.CompilerParams(dimension_semantics=("parallel",)),
        name="rmsnorm",
    )(x, g.reshape(1, D))


def _mm_kernel(*refs, n_x, n_e, pre, post, nk):
    x_refs = refs[:n_x]
    w_ref = refs[n_x]
    e_refs = refs[n_x + 1:n_x + 1 + n_e]
    o_ref = refs[n_x + 1 + n_e]
    x = pre(*[r[...] for r in x_refs])
    part = jnp.dot(x.astype(BF16), w_ref[...].astype(BF16), preferred_element_type=F32)
    if nk == 1:
        o_ref[...] = post(part, *[e[...] for e in e_refs]).astype(o_ref.dtype)
        return
    acc_ref = refs[-1]
    k = pl.program_id(2)

    @pl.when(k == 0)
    def _():
        acc_ref[...] = part

    @pl.when(k > 0)
    def _():
        acc_ref[...] += part

    @pl.when(k == nk - 1)
    def _():
        o_ref[...] = post(acc_ref[...], *[e[...] for e in e_refs]).astype(o_ref.dtype)


def _identity(x):
    return x


def _matmul_call(xs, w, extras=(), *, pre=_identity, post=_identity, out_dtype=F32,
                 bm=512, bn=512, bk=None, name="matmul"):
    K, N = w.shape
    M = max(x.shape[0] for x in xs)
    bm = min(bm, M)
    bn = min(bn, N)
    bk = K if bk is None else min(bk, K)
    assert M % bm == 0 and N % bn == 0 and K % bk == 0
    nk = K // bk
    x_specs = [pl.BlockSpec((bm, bk), lambda n, m, k: (m, k)) if x.shape[0] == M
               else pl.BlockSpec((x.shape[0], bk), lambda n, m, k: (0, k)) for x in xs]
    e_specs = [pl.BlockSpec((bm, bn), lambda n, m, k: (m, n)) if e.shape[0] == M
               else pl.BlockSpec((1, bn), lambda n, m, k: (0, n)) for e in extras]
    return pl.pallas_call(
        functools.partial(_mm_kernel, n_x=len(xs), n_e=len(extras), pre=pre, post=post, nk=nk),
        grid=(N // bn, M // bm, nk),
        in_specs=x_specs + [pl.BlockSpec((bk, bn), lambda n, m, k: (k, n))] + e_specs,
        out_specs=pl.BlockSpec((bm, bn), lambda n, m, k: (m, n)),
        out_shape=jax.ShapeDtypeStruct((M, N), out_dtype),
        scratch_shapes=[pltpu.VMEM((bm, bn), F32)] if nk > 1 else [],
        compiler_params=pltpu.CompilerParams(
            dimension_semantics=("parallel", "parallel", "arbitrary"),
            vmem_limit_bytes=VMEM_LIMIT),
        name=name,
    )(*xs, w, *extras)


def _lora_kernel(*refs, n_x, n_e, pre, mid, post):
    x_refs = refs[:n_x]
    w1_ref, w2_ref = refs[n_x], refs[n_x + 1]
    e_refs = refs[n_x + 2:n_x + 2 + n_e]
    o_ref = refs[n_x + 2 + n_e]
    x = pre(*[r[...] for r in x_refs])
    z = mid(_bdot(x, w1_ref[...]))
    o_ref[...] = post(_bdot(z, w2_ref[...]), *[e[...] for e in e_refs]).astype(o_ref.dtype)


def _lora_call(xs, w1, w2, extras=(), *, pre=_identity, mid=_identity, post=_identity, bm=512, name="lora"):
    K, R = w1.shape
    N = w2.shape[1]
    M = max(x.shape[0] for x in xs)
    bm = min(bm, M)
    x_specs = [pl.BlockSpec((bm, K), lambda m: (m, 0)) if x.shape[0] == M
               else pl.BlockSpec((x.shape[0], K), lambda m: (0, 0)) for x in xs]
    e_specs = [pl.BlockSpec((bm, N), lambda m: (m, 0)) if e.shape[0] == M
               else pl.BlockSpec((1, N), lambda m: (0, 0)) for e in extras]
    return pl.pallas_call(
        functools.partial(_lora_kernel, n_x=len(xs), n_e=len(extras), pre=pre, mid=mid, post=post),
        grid=(M // bm,),
        in_specs=x_specs + [pl.BlockSpec((K, R), lambda m: (0, 0)), pl.BlockSpec((R, N), lambda m: (0, 0))]
        + e_specs,
        out_specs=pl.BlockSpec((bm, N), lambda m: (m, 0)),
        out_shape=jax.ShapeDtypeStruct((M, N), F32),
        compiler_params=pltpu.CompilerParams(dimension_semantics=("parallel",),
                                             vmem_limit_bytes=VMEM_LIMIT),
        name=name,
    )(*xs, w1, w2, *extras)


def _swa_kernel(slopes_ref, q_ref, kp_ref, vp_ref, kc_ref, vc_ref, o_ref, lse_ref,
                *, tq, n_cur, dil, group, first_has_prev):
    hp = pl.program_id(2)
    i = pl.program_id(3)
    q = q_ref[0]
    kp, vp, kc, vc = kp_ref[0], vp_ref[0], kc_ref[0], vc_ref[0]
    tp = kp.shape[0]
    lane_q = lax.broadcasted_iota(jnp.int32, (tq, LANES), 1) < HD_B
    iq_p = lax.broadcasted_iota(jnp.int32, (tq, tp), 0)
    jk_p = lax.broadcasted_iota(jnp.int32, (tq, tp), 1)
    steps_p = iq_p + tp - jk_p
    valid_p = steps_p <= ATT_STEPS
    if not first_has_prev:
        valid_p = valid_p & (i > 0)
    iq_c = lax.broadcasted_iota(jnp.int32, (tq, tq), 0)
    jk_c = lax.broadcasted_iota(jnp.int32, (tq, tq), 1)
    steps_c = iq_c - jk_c
    valid_c = (steps_c >= 0) & (jk_c < n_cur)
    scale = HD_B ** -0.5
    outs, lses = [], []
    for h in range(2):
        slope = slopes_ref[group * H_B + 2 * hp + h]
        qh = jnp.where(lane_q, q, 0.0) if h == 0 else jnp.where(lane_q, 0.0, q)
        sp = _bdot_nt(qh, kp) * scale - slope * (steps_p * dil).astype(F32)
        sc = _bdot_nt(qh, kc) * scale - slope * (steps_c * dil).astype(F32)
        sp = jnp.where(valid_p, sp, -jnp.inf)
        sc = jnp.where(valid_c, sc, -jnp.inf)
        m = jnp.maximum(jnp.max(sp, axis=-1, keepdims=True), jnp.max(sc, axis=-1, keepdims=True))
        pp = jnp.exp(sp - m)
        pc = jnp.exp(sc - m)
        l = jnp.sum(pp, axis=-1, keepdims=True) + jnp.sum(pc, axis=-1, keepdims=True)
        acc = _bdot(pp, vp) + _bdot(pc, vc)
        outs.append(acc / l)
        lses.append(jnp.broadcast_to(m + jnp.log(l), (tq, LANES)))
    o_ref[0] = jnp.where(lane_q, outs[0], outs[1])
    lse_ref[0] = jnp.where(lane_q, lses[0], lses[1])


def _swa_call(slopes, q_arr, kv_prev_arr, kv_cur_arr, *, n_streams, n_qblocks, tq, n_cur, dil, group,
              q_cols, prev_cols, cur_cols, prev_is_cache, name):
    B = q_arr.shape[0]
    n_hp = H_B // 2
    if prev_is_cache:
        prev_row = lambda i: 0
    else:
        prev_row = lambda i: jnp.maximum(i - 1, 0)
    in_specs = [
        pl.BlockSpec(memory_space=pltpu.SMEM),
        pl.BlockSpec((1, tq, LANES), lambda b, c, hp, i: (b, i, q_cols(c, hp))),
        pl.BlockSpec((1, ATT_STEPS, LANES), lambda b, c, hp, i: (b, prev_row(i), prev_cols(c, hp)[0])),
        pl.BlockSpec((1, ATT_STEPS, LANES), lambda b, c, hp, i: (b, prev_row(i), prev_cols(c, hp)[1])),
        pl.BlockSpec((1, tq, LANES), lambda b, c, hp, i: (b, i, cur_cols(c, hp)[0])),
        pl.BlockSpec((1, tq, LANES), lambda b, c, hp, i: (b, i, cur_cols(c, hp)[1])),
    ]
    out_spec = pl.BlockSpec((1, tq, LANES), lambda b, c, hp, i: (b, i, c * n_hp + hp))
    out_sds = jax.ShapeDtypeStruct((B, n_qblocks * tq, n_streams * H_B * HD_B), F32)
    return pl.pallas_call(
        functools.partial(_swa_kernel, tq=tq, n_cur=n_cur, dil=dil, group=group,
                          first_has_prev=prev_is_cache),
        grid=(B, n_streams, n_hp, n_qblocks),
        in_specs=in_specs,
        out_specs=[out_spec, out_spec],
        out_shape=[out_sds, out_sds],
        compiler_params=pltpu.CompilerParams(
            dimension_semantics=("parallel", "parallel", "parallel", "arbitrary")),
        name=name,
    )(slopes, q_arr, kv_prev_arr, kv_prev_arr, kv_cur_arr, kv_cur_arr)


def _merge_groups(o0, l0, o1, l1, o2, l2):
    m = jnp.maximum(jnp.maximum(l0, l1), l2)
    w0, w1, w2 = jnp.exp(l0 - m), jnp.exp(l1 - m), jnp.exp(l2 - m)
    return (w0 * o0 + w1 * o1 + w2 * o2) / (w0 + w1 + w2)


def _alibi_slopes():
    n = N_GROUPS * H_B
    return 2.0 ** (-8.0 * jnp.arange(1, n + 1, dtype=F32) / n)


def _pair_states(s):
    B, H = s.shape[:2]
    s = s.reshape(B, H // 2, 2, HEAD_A, HEAD_A)
    z = jnp.zeros_like(s[:, :, 0])
    top = jnp.concatenate([s[:, :, 0], z], axis=-1)
    bot = jnp.concatenate([z, s[:, :, 1]], axis=-1)
    return jnp.concatenate([top, bot], axis=-2)


def _unpair_states(s2):
    B, P = s2.shape[:2]
    return jnp.stack([s2[:, :, :HEAD_A, :HEAD_A], s2[:, :, HEAD_A:, HEAD_A:]],
                     axis=2).reshape(B, 2 * P, HEAD_A, HEAD_A)


def _mix(i):
    def pre(h, prev, mu):
        return h + (prev - h) * mu[i:i + 1, :]
    return pre


def _rwkv_block(x, shift0, wkv0, norm_g, p):
    (mu, w0, w1, w2, a0, a1, a2, g1, g2, k_k, k_a, r_k, lnx_w, lnx_b, w_r, w_k, w_v, w_o) = p
    B, T, D = x.shape
    M = B * T
    x2 = x.reshape(M, D)
    h2 = _rmsnorm_call(x2, norm_g)
    h = h2.reshape(B, T, D)
    prev2 = jnp.concatenate([shift0[:, None, :], h[:, :-1]], axis=1).reshape(M, D)
    hx = [h2, prev2, mu]
    r = _matmul_call(hx, w_r, pre=_mix(0), name="rwkv_r")
    k = _matmul_call(hx, w_k, pre=_mix(2), name="rwkv_k")
    v = _matmul_call(hx, w_v, pre=_mix(3), name="rwkv_v")
    wlog = _lora_call(hx, w1, w2, [w0.reshape(1, D)], pre=_mix(1), mid=jnp.tanh,
                      post=lambda z, b: -jax.nn.softplus(-(b + z)) - 0.5, name="rwkv_w")
    a = _lora_call(hx, a1, a2, [a0.reshape(1, D)], pre=_mix(4),
                   post=lambda z, b: jax.nn.sigmoid(b + z), name="rwkv_a")
    g = _lora_call(hx, g1, g2, pre=_mix(5), mid=jax.nn.sigmoid, name="rwkv_g")

    L = HEAD_A
    Tp = -(-T // L) * L
    def seq(z, fill=0.0):
        z = z.reshape(B, T, D)
        if Tp != T:
            z = jnp.pad(z, ((0, 0), (0, Tp - T), (0, 0)), constant_values=fill)
        return z
    y, s2 = _wkv_call(seq(r), seq(k), seq(v), seq(wlog, -jnp.inf), seq(a), seq(g),
                      k_k, k_a, r_k.reshape(D), lnx_w, lnx_b, _pair_states(wkv0), L=L)
    y2 = y[:, :T].reshape(M, D)
    out = _matmul_call([y2], w_o, [x2], post=lambda acc, res: res + acc, name="rwkv_o")
    return out.reshape(B, T, D), _unpair_states(s2), h[:, -1]


def _ffn_block(x2, norm_g, w_up, w_down):
    hn = _rmsnorm_call(x2, norm_g, BF16)
    mid = _matmul_call([hn], w_up, post=lambda acc: jnp.square(jnp.maximum(acc, 0.0)),
                       out_dtype=BF16, bm=1024, bn=1024, name="ffn_up")
    return _matmul_call([mid], w_down, [x2], post=lambda acc, res: res + acc,
                        bm=1024, bn=512, bk=2048, name="ffn_down")


def _attn_prompt(qkv, slopes):
    B, T, W = qkv.shape
    gw = 3 * H_B * HD_B
    res = []
    for g, (window, dil) in enumerate(GROUPS):
        view = qkv.reshape(B, T // dil, dil * W)
        base = lambda c, hp, which, g=g: (c * W + g * gw + which * H_B * HD_B) // LANES + hp
        kvc = lambda c, hp, base=base: (base(c, hp, 1), base(c, hp, 2))
        o, lse = _swa_call(
            slopes, view, view, view, n_streams=dil, n_qblocks=T // dil // ATT_STEPS, tq=ATT_STEPS,
            n_cur=ATT_STEPS, dil=dil, group=g, q_cols=lambda c, hp, base=base: base(c, hp, 0),
            prev_cols=kvc, cur_cols=kvc, prev_is_cache=False, name=f"attn_prompt_g{g}")
        res += [o.reshape(B * T, H_B * HD_B), lse.reshape(B * T, H_B * HD_B)]
    return res


def _attn_sample(qkv, caches, slopes):
    B, S, W = qkv.shape
    gw = 3 * H_B * HD_B
    hw = H_B * HD_B
    TQ = 8
    res = []
    for g, (window, dil) in enumerate(GROUPS):
        dd = min(dil, S)
        nq = S // dd
        new = qkv[:, :, g * gw:(g + 1) * gw].reshape(B, nq, dd * gw)
        new = jnp.pad(new, ((0, 0), (0, TQ - nq), (0, 0)))
        cache = caches[g].reshape(B, window // dil, dil * 2 * hw)
        o, lse = _swa_call(
            slopes, new, cache, new, n_streams=dd, n_qblocks=1, tq=TQ, n_cur=nq, dil=dil, group=g,
            q_cols=lambda c, hp: (c * gw) // LANES + hp,
            prev_cols=lambda c, hp: ((c * 2 * hw) // LANES + hp, (c * 2 * hw + hw) // LANES + hp),
            cur_cols=lambda c, hp: ((c * gw + hw) // LANES + hp, (c * gw + 2 * hw) // LANES + hp),
            prev_is_cache=True, name=f"attn_sample_g{g}")
        res += [o[:, :nq].reshape(B * S, hw), lse[:, :nq].reshape(B * S, hw)]
    return res


def _attn_block(x, norm_g, w_qkv, w_o, slopes, caches=None):
    B, T, D = x.shape
    M = B * T
    x2 = x.reshape(M, D)
    hn = _rmsnorm_call(x2, norm_g, BF16)
    qkv = _matmul_call([hn], w_qkv, bm=1024, bn=512, name="attn_qkv").reshape(B, T, -1)
    q5 = qkv.reshape(B, T, N_GROUPS, 3, H_B, HD_B)
    if caches is None:
        parts = _attn_prompt(qkv, slopes)
        bufs = [q5[:, T - min(w, T):, g, 1:] for g, (w, _) in enumerate(GROUPS)]
    else:
        parts = _attn_sample(qkv, caches, slopes)
        bufs = [jnp.concatenate([caches[g], q5[:, :, g, 1:]], axis=1)[:, T:] for g in range(N_GROUPS)]
    out = _matmul_call(parts, w_o, [x2], pre=_merge_groups, post=lambda acc, res: res + acc,
                       bm=256, name="attn_o")
    return out.reshape(B, T, D), bufs


def kernel(x_prompt, x_sample, state_wkv, state_shift, cache_kv_g1, cache_kv_g2, cache_kv_g3,
           norm_mix, norm_ffn, norm_final,
           rwkv_mu, rwkv_w0, rwkv_w1, rwkv_w2, rwkv_a0, rwkv_a1, rwkv_a2, rwkv_g1, rwkv_g2,
           rwkv_k_k, rwkv_k_a, rwkv_r_k, rwkv_lnx_w, rwkv_lnx_b, rwkv_w_r, rwkv_w_k, rwkv_w_v, rwkv_w_o,
           attn_w_qkv, attn_w_o, ffn_w_up, ffn_w_down):
    rwkv_params = (rwkv_mu, rwkv_w0, rwkv_w1, rwkv_w2, rwkv_a0, rwkv_a1, rwkv_a2, rwkv_g1, rwkv_g2,
                   rwkv_k_k, rwkv_k_a, rwkv_r_k, rwkv_lnx_w, rwkv_lnx_b,
                   rwkv_w_r, rwkv_w_k, rwkv_w_v, rwkv_w_o)
    slopes = _alibi_slopes()
    Bp, Tp, D = x_prompt.shape
    Bs, Ts, _ = x_sample.shape

    def ffn(x, i):
        B, T, _ = x.shape
        return _ffn_block(x.reshape(B * T, D), norm_ffn[i], ffn_w_up[i], ffn_w_down[i]).reshape(B, T, D)

    xp, wkv_p, shift_p = _rwkv_block(x_prompt, jnp.zeros((Bp, D), F32),
                                     jnp.zeros((Bp, H_A, HEAD_A, HEAD_A), F32), norm_mix[0], rwkv_params)
    xs, wkv_s, shift_s = _rwkv_block(x_sample, state_shift, state_wkv, norm_mix[0], rwkv_params)
    xp, xs = ffn(xp, 0), ffn(xs, 0)
    xp, (kv1_p, kv2_p, kv3_p) = _attn_block(xp, norm_mix[1], attn_w_qkv, attn_w_o, slopes)
    xs, (kv1_s, kv2_s, kv3_s) = _attn_block(xs, norm_mix[1], attn_w_qkv, attn_w_o, slopes,
                                            (cache_kv_g1, cache_kv_g2, cache_kv_g3))
    xp, xs = ffn(xp, 1), ffn(xs, 1)
    y_prompt = _rmsnorm_call(xp.reshape(Bp * Tp, D), norm_final).reshape(Bp, Tp, D)
    y_sample = _rmsnorm_call(xs.reshape(Bs * Ts, D), norm_final).reshape(Bs, Ts, D)
    return (y_prompt, y_sample, wkv_p, shift_p, kv1_p, kv2_p, kv3_p,
            wkv_s, shift_s, kv1_s, kv2_s, kv3_s)
```

```python
import functools

import jax
import jax.numpy as jnp
from jax import lax
from jax.experimental import pallas as pl
from jax.experimental.pallas import tpu as pltpu

F32 = jnp.float32
BF16 = jnp.bfloat16

D_MODEL = 2048
HEAD_A = 64
H_A = D_MODEL // HEAD_A
GN_EPS = 64e-5
GROUPS = ((128, 1), (512, 4), (2048, 16))
N_GROUPS = len(GROUPS)
H_B = 16
HD_B = 64
ATT_STEPS = 128
RMS_EPS = 1e-6

LANES = 128
VMEM_LIMIT = 56 * 1024 * 1024


def _bdot(a, b):
    return jnp.dot(a.astype(BF16), b.astype(BF16), preferred_element_type=F32)


def _bdot_nt(a, b):
    return lax.dot_general(a.astype(BF16), b.astype(BF16), (((1,), (1,)), ((), ())),
                           preferred_element_type=F32)


def _bdot_tn(a, b):
    return lax.dot_general(a.astype(BF16), b.astype(BF16), (((0,), (0,)), ((), ())),
                           preferred_element_type=F32)


def _exact_dot(m, x):
    hi = x.astype(BF16)
    r1 = x - hi.astype(F32)
    mid = r1.astype(BF16)
    lo = (r1 - mid.astype(F32)).astype(BF16)
    mb = m.astype(BF16)
    acc = jnp.dot(mb, hi, preferred_element_type=F32)
    acc = acc + jnp.dot(mb, mid, preferred_element_type=F32)
    return acc + jnp.dot(mb, lo, preferred_element_type=F32)


def _head_sum(x, lane_lo):
    s0 = jnp.sum(jnp.where(lane_lo, x, 0.0), axis=-1, keepdims=True)
    s1 = jnp.sum(jnp.where(lane_lo, 0.0, x), axis=-1, keepdims=True)
    return jnp.where(lane_lo, s0, s1)


WKV_CHUNK = HEAD_A
(_MK_EYE, _MK_LVL0, _MK_BD, _MK_NMASK, _MK_CAUSAL, _MK_TRI, _MK_LEVELS) = range(7)
_N_LEVELS = 5
_N_MASKS = _MK_LEVELS + _N_LEVELS


def _wkv_fill_masks(mk_ref):
    L = WKV_CHUNK
    r2 = lax.broadcasted_iota(jnp.int32, (LANES, LANES), 0)
    c2 = lax.broadcasted_iota(jnp.int32, (LANES, LANES), 1)
    f = lambda m: m.astype(F32)
    same_block = (r2 < L) == (c2 < L)
    strict = (c2 % L) < (r2 % L)
    incl = (c2 % L) <= (r2 % L)
    mk_ref[_MK_EYE] = f(r2 == c2)
    mk_ref[_MK_LVL0] = f((r2 // 2 == c2 // 2) & (c2 < r2))
    mk_ref[_MK_BD] = f(same_block)
    mk_ref[_MK_NMASK] = f(strict & same_block)
    mk_ref[_MK_CAUSAL] = f(strict | (incl & (r2 >= L)))
    mk_ref[_MK_TRI] = f(incl & (r2 < L) & (c2 < L))
    size = 2
    for lvl in range(_N_LEVELS):
        mk_ref[_MK_LEVELS + lvl] = f((r2 // (2 * size) == c2 // (2 * size))
                                     & ((r2 // size) % 2 == 1) & ((c2 // size) % 2 == 0) & same_block)
        size *= 2


def _wkv_pair_chunk(r, kraw, v, wl, a, g, k_k, k_a, r_k, lnx_w, lnx_b, s2, mk_ref):
    L = WKV_CHUNK
    lane_lo = lax.broadcasted_iota(jnp.int32, (L, LANES), 1) < HEAD_A
    lo2 = lax.broadcasted_iota(jnp.int32, (2 * L, LANES), 1) < HEAD_A

    logd = -jnp.exp(wl)
    kkraw = kraw * k_k
    kk = kkraw / jnp.maximum(jnp.sqrt(_head_sum(kkraw * kkraw, lane_lo)), 1e-12)
    k2 = kraw * (1.0 + (a - 1.0) * k_a)

    cum = _exact_dot(mk_ref[_MK_TRI][:L, :L], logd)
    w_incl = jnp.exp(cum)
    w_excl = jnp.exp(cum - logd)
    w_inv = jnp.exp(-cum)
    w_last = w_incl[L - 1:L, :]

    at = -kk * w_excl
    bt = (kk * a) * w_inv
    kt = k2 * w_inv
    rt = r * w_incl

    ar = jnp.concatenate([at, rt], axis=0)
    bk = jnp.concatenate([bt, kt], axis=0)
    kb = jnp.concatenate([kt, bt], axis=0)
    m0 = _bdot_nt(jnp.where(lo2, ar, 0.0), bk)
    m1 = _bdot_nt(jnp.where(lo2, 0.0, ar), kb)
    aprp = _bdot_nt(ar, s2)
    yield

    a0 = m0[:L]
    a1 = m1[:L]
    n2 = jnp.concatenate([a0, a1], axis=0) * mk_ref[_MK_NMASK]

    t2 = mk_ref[_MK_EYE] + n2 * mk_ref[_MK_LVL0]
    for lvl in range(_N_LEVELS):
        tc = _bdot(t2, n2 * mk_ref[_MK_LEVELS + lvl])
        yield
        t2 = t2 + _bdot(tc, t2)
        yield
    tcat = t2[:L] + t2[L:]

    causal = mk_ref[_MK_CAUSAL]
    strict, incl = causal[:L], causal[L:]
    v_lo = jnp.where(lane_lo, v, 0.0)
    v_hi = jnp.where(lane_lo, 0.0, v)
    ak_cat = jnp.where(lane_lo, a1, a0) * strict
    rhs = aprp[:L] + _bdot(ak_cat, jnp.concatenate([v_hi, v_lo], axis=0))
    yield
    u = _bdot(tcat, jnp.concatenate([jnp.where(lane_lo, rhs, 0.0), jnp.where(lane_lo, 0.0, rhs)], axis=0))
    yield
    u_lo = jnp.where(lane_lo, u, 0.0)
    u_hi = jnp.where(lane_lo, 0.0, u)

    y = (aprp[L:] + _bdot(m0[L:] * incl, jnp.concatenate([u_lo, v_lo], axis=0))
         + _bdot(m1[L:] * incl, jnp.concatenate([v_hi, u_hi], axis=0)))

    delta = _bdot_tn(jnp.concatenate([u, v], axis=0), bk)
    yield
    s_new = (s2 + delta * mk_ref[_MK_BD]) * w_last

    mean = _head_sum(y, lane_lo) * (1.0 / HEAD_A)
    yc = y - mean
    var = _head_sum(yc * yc, lane_lo) * (1.0 / HEAD_A)
    yn = yc * lax.rsqrt(var + GN_EPS) * lnx_w + lnx_b
    bonus = _head_sum(r * k2 * r_k, lane_lo)
    return (yn + bonus * v) * g, s_new


def _wkv_kernel(r_ref, k_ref, v_ref, wl_ref, a_ref, g_ref, kk_ref, ka_ref, rk_ref, lw_ref, lb_ref,
                s0_ref, y_ref, sout_ref, s_scr, mk_scr, *, n_pairs):
    b, pb, c = pl.program_id(0), pl.program_id(1), pl.program_id(2)

    @pl.when((b == 0) & (pb == 0) & (c == 0))
    def _():
        _wkv_fill_masks(mk_scr)

    @pl.when(c == 0)
    def _():
        s_scr[...] = s0_ref[0]

    lanes = [slice(p * LANES, (p + 1) * LANES) for p in range(n_pairs)]
    chains = [_wkv_pair_chunk(
        r_ref[0, :, sl], k_ref[0, :, sl], v_ref[0, :, sl], wl_ref[0, :, sl], a_ref[0, :, sl],
        g_ref[0, :, sl], kk_ref[:, sl], ka_ref[:, sl], rk_ref[:, sl], lw_ref[:, sl], lb_ref[:, sl],
        s_scr[p], mk_scr) for p, sl in enumerate(lanes)]
    results = [None] * n_pairs
    while any(res is None for res in results):
        for p, chain in enumerate(chains):
            if results[p] is None:
                try:
                    next(chain)
                except StopIteration as done:
                    results[p] = done.value
    for p, sl in enumerate(lanes):
        y_ref[0, :, sl], s_scr[p] = results[p]

    @pl.when(c == pl.num_programs(2) - 1)
    def _():
        sout_ref[0] = s_scr[...]


def _wkv_call(r, k, v, wlog, a, g, k_k, k_a, r_k, lnx_w, lnx_b, s2_0, *, pairs_per_step):
    B, T, D = r.shape
    L, P = WKV_CHUNK, pairs_per_step
    assert T % L == 0 and 2 * L == LANES and (D // LANES) % P == 0
    seq = pl.BlockSpec((1, L, P * LANES), lambda b, p, c: (b, c, p))
    par = pl.BlockSpec((1, P * LANES), lambda b, p, c: (0, p))
    st = pl.BlockSpec((1, P, LANES, LANES), lambda b, p, c: (b, p, 0, 0))
    row = lambda x: x.reshape(1, D)
    return pl.pallas_call(
        functools.partial(_wkv_kernel, n_pairs=P),
        grid=(B, D // LANES // P, T // L),
        in_specs=[seq] * 6 + [par] * 5 + [st],
        out_specs=[seq, st],
        out_shape=[jax.ShapeDtypeStruct((B, T, D), F32),
                   jax.ShapeDtypeStruct((B, D // LANES, LANES, LANES), F32)],
        scratch_shapes=[pltpu.VMEM((P, LANES, LANES), F32), pltpu.VMEM((_N_MASKS, LANES, LANES), F32)],
        compiler_params=pltpu.CompilerParams(
            dimension_semantics=("arbitrary", "arbitrary", "arbitrary")),
        name="wkv7_chunk_scan",
    )(r, k, v, wlog, a, g, row(k_k), row(k_a), row(r_k), row(lnx_w), row(lnx_b), s2_0)


def _rmsnorm_kernel(x_ref, g_ref, o_ref):
    x = x_ref[...]
    y = x * lax.rsqrt(jnp.mean(x * x, axis=-1, keepdims=True) + RMS_EPS)
    o_ref[...] = (y * g_ref[...]).astype(o_ref.dtype)


def _rmsnorm_call(x, g, out_dtype=F32):
    M, D = x.shape
    bm = min(M, 512)
    return pl.pallas_call(
        _rmsnorm_kernel,
        grid=(M // bm,),
        in_specs=[pl.BlockSpec((bm, D), lambda m: (m, 0)), pl.BlockSpec((1, D), lambda m: (0, 0))],
        out_specs=pl.BlockSpec((bm, D), lambda m: (m, 0)),
        out_shape=jax.ShapeDtypeStruct((M, D), out_dtype),
        compiler_params=pltpu.CompilerParams(dimension_semantics=("parallel",)),
        name="rmsnorm",
    )(x, g.reshape(1, D))


def _mm_kernel(*refs, n_x, n_e, pre, post, nk):
    x_refs = refs[:n_x]
    w_ref = refs[n_x]
    e_refs = refs[n_x + 1:n_x + 1 + n_e]
    o_ref = refs[n_x + 1 + n_e]
    x = pre(*[r[...] for r in x_refs])
    part = jnp.dot(x.astype(BF16), w_ref[...].astype(BF16), preferred_element_type=F32)
    if nk == 1:
        o_ref[...] = post(part, *[e[...] for e in e_refs]).astype(o_ref.dtype)
        return
    acc_ref = refs[-1]
    k = pl.program_id(2)

    @pl.when(k == 0)
    def _():
        acc_ref[...] = part

    @pl.when(k > 0)
    def _():
        acc_ref[...] += part

    @pl.when(k == nk - 1)
    def _():
        o_ref[...] = post(acc_ref[...], *[e[...] for e in e_refs]).astype(o_ref.dtype)


def _identity(x):
    return x


def _matmul_call(xs, w, extras=(), *, pre=_identity, post=_identity, out_dtype=F32,
                 bm=512, bn=512, bk=None, name="matmul"):
    K, N = w.shape
    M = max(x.shape[0] for x in xs)
    bm = min(bm, M)
    bn = min(bn, N)
    bk = K if bk is None else min(bk, K)
    assert M % bm == 0 and N % bn == 0 and K % bk == 0
    nk = K // bk
    x_specs = [pl.BlockSpec((bm, bk), lambda n, m, k: (m, k)) if x.shape[0] == M
               else pl.BlockSpec((x.shape[0], bk), lambda n, m, k: (0, k)) for x in xs]
    e_specs = [pl.BlockSpec((bm, bn), lambda n, m, k: (m, n)) if e.shape[0] == M
               else pl.BlockSpec((1, bn), lambda n, m, k: (0, n)) for e in extras]
    return pl.pallas_call(
        functools.partial(_mm_kernel, n_x=len(xs), n_e=len(extras), pre=pre, post=post, nk=nk),
        grid=(N // bn, M // bm, nk),
        in_specs=x_specs + [pl.BlockSpec((bk, bn), lambda n, m, k: (k, n))] + e_specs,
        out_specs=pl.BlockSpec((bm, bn), lambda n, m, k: (m, n)),
        out_shape=jax.ShapeDtypeStruct((M, N), out_dtype),
        scratch_shapes=[pltpu.VMEM((bm, bn), F32)] if nk > 1 else [],
        compiler_params=pltpu.CompilerParams(
            dimension_semantics=("parallel", "parallel", "arbitrary"),
            vmem_limit_bytes=VMEM_LIMIT),
        name=name,
    )(*xs, w, *extras)


def _lora_kernel(*refs, n_x, n_e, pre, mid, post):
    x_refs = refs[:n_x]
    w1_ref, w2_ref = refs[n_x], refs[n_x + 1]
    e_refs = refs[n_x + 2:n_x + 2 + n_e]
    o_ref = refs[n_x + 2 + n_e]
    x = pre(*[r[...] for r in x_refs])
    z = mid(_bdot(x, w1_ref[...]))
    o_ref[...] = post(_bdot(z, w2_ref[...]), *[e[...] for e in e_refs]).astype(o_ref.dtype)


def _lora_call(xs, w1, w2, extras=(), *, pre=_identity, mid=_identity, post=_identity, bm=512, name="lora"):
    K, R = w1.shape
    N = w2.shape[1]
    M = max(x.shape[0] for x in xs)
    bm = min(bm, M)
    x_specs = [pl.BlockSpec((bm, K), lambda m: (m, 0)) if x.shape[0] == M
               else pl.BlockSpec((x.shape[0], K), lambda m: (0, 0)) for x in xs]
    e_specs = [pl.BlockSpec((bm, N), lambda m: (m, 0)) if e.shape[0] == M
               else pl.BlockSpec((1, N), lambda m: (0, 0)) for e in extras]
    return pl.pallas_call(
        functools.partial(_lora_kernel, n_x=len(xs), n_e=len(extras), pre=pre, mid=mid, post=post),
        grid=(M // bm,),
        in_specs=x_specs + [pl.BlockSpec((K, R), lambda m: (0, 0)), pl.BlockSpec((R, N), lambda m: (0, 0))]
        + e_specs,
        out_specs=pl.BlockSpec((bm, N), lambda m: (m, 0)),
        out_shape=jax.ShapeDtypeStruct((M, N), F32),
        compiler_params=pltpu.CompilerParams(dimension_semantics=("parallel",),
                                             vmem_limit_bytes=VMEM_LIMIT),
        name=name,
    )(*xs, w1, w2, *extras)


def _swa_kernel(slopes_ref, q_ref, kp_ref, vp_ref, kc_ref, vc_ref, o_ref, lse_ref,
                *, tq, n_cur, dil, group, first_has_prev):
    hp = pl.program_id(2)
    i = pl.program_id(3)
    q = q_ref[0]
    kp, vp, kc, vc = kp_ref[0], vp_ref[0], kc_ref[0], vc_ref[0]
    tp = kp.shape[0]
    lane_q = lax.broadcasted_iota(jnp.int32, (tq, LANES), 1) < HD_B
    iq_p = lax.broadcasted_iota(jnp.int32, (tq, tp), 0)
    jk_p = lax.broadcasted_iota(jnp.int32, (tq, tp), 1)
    steps_p = iq_p + tp - jk_p
    valid_p = steps_p <= ATT_STEPS
    if not first_has_prev:
        valid_p = valid_p & (i > 0)
    iq_c = lax.broadcasted_iota(jnp.int32, (tq, tq), 0)
    jk_c = lax.broadcasted_iota(jnp.int32, (tq, tq), 1)
    steps_c = iq_c - jk_c
    valid_c = (steps_c >= 0) & (jk_c < n_cur)
    scale = HD_B ** -0.5
    outs, lses = [], []
    for h in range(2):
        slope = slopes_ref[group * H_B + 2 * hp + h]
        qh = jnp.where(lane_q, q, 0.0) if h == 0 else jnp.where(lane_q, 0.0, q)
        sp = _bdot_nt(qh, kp) * scale - slope * (steps_p * dil).astype(F32)
        sc = _bdot_nt(qh, kc) * scale - slope * (steps_c * dil).astype(F32)
        sp = jnp.where(valid_p, sp, -jnp.inf)
        sc = jnp.where(valid_c, sc, -jnp.inf)
        m = jnp.maximum(jnp.max(sp, axis=-1, keepdims=True), jnp.max(sc, axis=-1, keepdims=True))
        pp = jnp.exp(sp - m)
        pc = jnp.exp(sc - m)
        l = jnp.sum(pp, axis=-1, keepdims=True) + jnp.sum(pc, axis=-1, keepdims=True)
        acc = _bdot(pp, vp) + _bdot(pc, vc)
        outs.append(acc / l)
        lses.append(jnp.broadcast_to(m + jnp.log(l), (tq, LANES)))
    o_ref[0] = jnp.where(lane_q, outs[0], outs[1])
    lse_ref[0] = jnp.where(lane_q, lses[0], lses[1])


def _swa_call(slopes, q_arr, kv_prev_arr, kv_cur_arr, *, n_streams, n_qblocks, tq, n_cur, dil, group,
              q_cols, prev_cols, cur_cols, prev_is_cache, name):
    B = q_arr.shape[0]
    n_hp = H_B // 2
    if prev_is_cache:
        prev_row = lambda i: 0
    else:
        prev_row = lambda i: jnp.maximum(i - 1, 0)
    in_specs = [
        pl.BlockSpec(memory_space=pltpu.SMEM),
        pl.BlockSpec((1, tq, LANES), lambda b, c, hp, i: (b, i, q_cols(c, hp))),
        pl.BlockSpec((1, ATT_STEPS, LANES), lambda b, c, hp, i: (b, prev_row(i), prev_cols(c, hp)[0])),
        pl.BlockSpec((1, ATT_STEPS, LANES), lambda b, c, hp, i: (b, prev_row(i), prev_cols(c, hp)[1])),
        pl.BlockSpec((1, tq, LANES), lambda b, c, hp, i: (b, i, cur_cols(c, hp)[0])),
        pl.BlockSpec((1, tq, LANES), lambda b, c, hp, i: (b, i, cur_cols(c, hp)[1])),
    ]
    out_spec = pl.BlockSpec((1, tq, LANES), lambda b, c, hp, i: (b, i, c * n_hp + hp))
    out_sds = jax.ShapeDtypeStruct((B, n_qblocks * tq, n_streams * H_B * HD_B), F32)
    return pl.pallas_call(
        functools.partial(_swa_kernel, tq=tq, n_cur=n_cur, dil=dil, group=group,
                          first_has_prev=prev_is_cache),
        grid=(B, n_streams, n_hp, n_qblocks),
        in_specs=in_specs,
        out_specs=[out_spec, out_spec],
        out_shape=[out_sds, out_sds],
        compiler_params=pltpu.CompilerParams(
            dimension_semantics=("parallel", "parallel", "parallel", "arbitrary")),
        name=name,
    )(slopes, q_arr, kv_prev_arr, kv_prev_arr, kv_cur_arr, kv_cur_arr)


def _merge_groups(o0, l0, o1, l1, o2, l2):
    m = jnp.maximum(jnp.maximum(l0, l1), l2)
    w0, w1, w2 = jnp.exp(l0 - m), jnp.exp(l1 - m), jnp.exp(l2 - m)
    return (w0 * o0 + w1 * o1 + w2 * o2) / (w0 + w1 + w2)


def _alibi_slopes():
    n = N_GROUPS * H_B
    return 2.0 ** (-8.0 * jnp.arange(1, n + 1, dtype=F32) / n)


def _pair_states(s):
    B, H = s.shape[:2]
    s = s.reshape(B, H // 2, 2, HEAD_A, HEAD_A)
    z = jnp.zeros_like(s[:, :, 0])
    top = jnp.concatenate([s[:, :, 0], z], axis=-1)
    bot = jnp.concatenate([z, s[:, :, 1]], axis=-1)
    return jnp.concatenate([top, bot], axis=-2)


def _unpair_states(s2):
    B, P = s2.shape[:2]
    return jnp.stack([s2[:, :, :HEAD_A, :HEAD_A], s2[:, :, HEAD_A:, HEAD_A:]],
                     axis=2).reshape(B, 2 * P, HEAD_A, HEAD_A)


def _mix(i):
    def pre(h, prev, mu):
        return h + (prev - h) * mu[i:i + 1, :]
    return pre


def _rwkv_block(x, shift0, wkv0, norm_g, p):
    (mu, w0, w1, w2, a0, a1, a2, g1, g2, k_k, k_a, r_k, lnx_w, lnx_b, w_r, w_k, w_v, w_o) = p
    B, T, D = x.shape
    M = B * T
    x2 = x.reshape(M, D)
    h2 = _rmsnorm_call(x2, norm_g)
    h = h2.reshape(B, T, D)
    prev2 = jnp.concatenate([shift0[:, None, :], h[:, :-1]], axis=1).reshape(M, D)
    hx = [h2, prev2, mu]
    r = _matmul_call(hx, w_r, pre=_mix(0), name="rwkv_r")
    k = _matmul_call(hx, w_k, pre=_mix(2), name="rwkv_k")
    v = _matmul_call(hx, w_v, pre=_mix(3), name="rwkv_v")
    wlog = _lora_call(hx, w1, w2, [w0.reshape(1, D)], pre=_mix(1), mid=jnp.tanh,
                      post=lambda z, b: -jax.nn.softplus(-(b + z)) - 0.5, name="rwkv_w")
    a = _lora_call(hx, a1, a2, [a0.reshape(1, D)], pre=_mix(4),
                   post=lambda z, b: jax.nn.sigmoid(b + z), name="rwkv_a")
    g = _lora_call(hx, g1, g2, pre=_mix(5), mid=jax.nn.sigmoid, name="rwkv_g")

    L = WKV_CHUNK
    Tp = -(-T // L) * L
    def seq(z, fill=0.0):
        z = z.reshape(B, T, D)
        if Tp != T:
            z = jnp.pad(z, ((0, 0), (0, Tp - T), (0, 0)), constant_values=fill)
        return z
    y, s2 = _wkv_call(seq(r), seq(k), seq(v), seq(wlog, -jnp.inf), seq(a), seq(g),
                      k_k, k_a, r_k.reshape(D), lnx_w, lnx_b, _pair_states(wkv0), pairs_per_step=D // LANES)
    y2 = y[:, :T].reshape(M, D)
    out = _matmul_call([y2], w_o, [x2], post=lambda acc, res: res + acc, name="rwkv_o")
    return out.reshape(B, T, D), _unpair_states(s2), h[:, -1]


def _ffn_block(x2, norm_g, w_up, w_down):
    hn = _rmsnorm_call(x2, norm_g, BF16)
    mid = _matmul_call([hn], w_up, post=lambda acc: jnp.square(jnp.maximum(acc, 0.0)),
                       out_dtype=BF16, bm=1024, bn=1024, name="ffn_up")
    return _matmul_call([mid], w_down, [x2], post=lambda acc, res: res + acc,
                        bm=1024, bn=512, bk=2048, name="ffn_down")


def _attn_prompt(qkv, slopes):
    B, T, W = qkv.shape
    gw = 3 * H_B * HD_B
    res = []
    for g, (window, dil) in enumerate(GROUPS):
        view = qkv.reshape(B, T // dil, dil * W)
        base = lambda c, hp, which, g=g: (c * W + g * gw + which * H_B * HD_B) // LANES + hp
        kvc = lambda c, hp, base=base: (base(c, hp, 1), base(c, hp, 2))
        o, lse = _swa_call(
            slopes, view, view, view, n_streams=dil, n_qblocks=T // dil // ATT_STEPS, tq=ATT_STEPS,
            n_cur=ATT_STEPS, dil=dil, group=g, q_cols=lambda c, hp, base=base: base(c, hp, 0),
            prev_cols=kvc, cur_cols=kvc, prev_is_cache=False, name=f"attn_prompt_g{g}")
        res += [o.reshape(B * T, H_B * HD_B), lse.reshape(B * T, H_B * HD_B)]
    return res


def _attn_sample(qkv, caches, slopes):
    B, S, W = qkv.shape
    gw = 3 * H_B * HD_B
    hw = H_B * HD_B
    TQ = 8
    res = []
    for g, (window, dil) in enumerate(GROUPS):
        dd = min(dil, S)
        nq = S // dd
        new = qkv[:, :, g * gw:(g + 1) * gw].reshape(B, nq, dd * gw)
        new = jnp.pad(new, ((0, 0), (0, TQ - nq), (0, 0)))
        cache = caches[g].reshape(B, window // dil, dil * 2 * hw)
        o, lse = _swa_call(
            slopes, new, cache, new, n_streams=dd, n_qblocks=1, tq=TQ, n_cur=nq, dil=dil, group=g,
            q_cols=lambda c, hp: (c * gw) // LANES + hp,
            prev_cols=lambda c, hp: ((c * 2 * hw) // LANES + hp, (c * 2 * hw + hw) // LANES + hp),
            cur_cols=lambda c, hp: ((c * gw + hw) // LANES + hp, (c * gw + 2 * hw) // LANES + hp),
            prev_is_cache=True, name=f"attn_sample_g{g}")
        res += [o[:, :nq].reshape(B * S, hw), lse[:, :nq].reshape(B * S, hw)]
    return res


def _attn_block(x, norm_g, w_qkv, w_o, slopes, caches=None):
    B, T, D = x.shape
    M = B * T
    x2 = x.reshape(M, D)
    hn = _rmsnorm_call(x2, norm_g, BF16)
    qkv = _matmul_call([hn], w_qkv, bm=1024, bn=512, name="attn_qkv").reshape(B, T, -1)
    q5 = qkv.reshape(B, T, N_GROUPS, 3, H_B, HD_B)
    if caches is None:
        parts = _attn_prompt(qkv, slopes)
        bufs = [q5[:, T - min(w, T):, g, 1:] for g, (w, _) in enumerate(GROUPS)]
    else:
        parts = _attn_sample(qkv, caches, slopes)
        bufs = [jnp.concatenate([caches[g], q5[:, :, g, 1:]], axis=1)[:, T:] for g in range(N_GROUPS)]
    out = _matmul_call(parts, w_o, [x2], pre=_merge_groups, post=lambda acc, res: res + acc,
                       bm=256, name="attn_o")
    return out.reshape(B, T, D), bufs


def kernel(x_prompt, x_sample, state_wkv, state_shift, cache_kv_g1, cache_kv_g2, cache_kv_g3,
           norm_mix, norm_ffn, norm_final,
           rwkv_mu, rwkv_w0, rwkv_w1, rwkv_w2, rwkv_a0, rwkv_a1, rwkv_a2, rwkv_g1, rwkv_g2,
           rwkv_k_k, rwkv_k_a, rwkv_r_k, rwkv_lnx_w, rwkv_lnx_b, rwkv_w_r, rwkv_w_k, rwkv_w_v, rwkv_w_o,
           attn_w_qkv, attn_w_o, ffn_w_up, ffn_w_down):
    rwkv_params = (rwkv_mu, rwkv_w0, rwkv_w1, rwkv_w2, rwkv_a0, rwkv_a1, rwkv_a2, rwkv_g1, rwkv_g2,
                   rwkv_k_k, rwkv_k_a, rwkv_r_k, rwkv_lnx_w, rwkv_lnx_b,
                   rwkv_w_r, rwkv_w_k, rwkv_w_v, rwkv_w_o)
    slopes = _alibi_slopes()
    Bp, Tp, D = x_prompt.shape
    Bs, Ts, _ = x_sample.shape

    def ffn(x, i):
        B, T, _ = x.shape
        return _ffn_block(x.reshape(B * T, D), norm_ffn[i], ffn_w_up[i], ffn_w_down[i]).reshape(B, T, D)

    xp, wkv_p, shift_p = _rwkv_block(x_prompt, jnp.zeros((Bp, D), F32),
                                     jnp.zeros((Bp, H_A, HEAD_A, HEAD_A), F32), norm_mix[0], rwkv_params)
    xs, wkv_s, shift_s = _rwkv_block(x_sample, state_shift, state_wkv, norm_mix[0], rwkv_params)
    xp, xs = ffn(xp, 0), ffn(xs, 0)
    xp, (kv1_p, kv2_p, kv3_p) = _attn_block(xp, norm_mix[1], attn_w_qkv, attn_w_o, slopes)
    xs, (kv1_s, kv2_s, kv3_s) = _attn_block(xs, norm_mix[1], attn_w_qkv, attn_w_o, slopes,
                                            (cache_kv_g1, cache_kv_g2, cache_kv_g3))
    xp, xs = ffn(xp, 1), ffn(xs, 1)
    y_prompt = _rmsnorm_call(xp.reshape(Bp * Tp, D), norm_final).reshape(Bp, Tp, D)
    y_sample = _rmsnorm_call(xs.reshape(Bs * Ts, D), norm_final).reshape(Bs, Ts, D)
    return (y_prompt, y_sample, wkv_p, shift_p, kv1_p, kv2_p, kv3_p,
            wkv_s, shift_s, kv1_s, kv2_s, kv3_s)
```

```python
import functools

import jax
import jax.numpy as jnp
from jax import lax
from jax.experimental import pallas as pl
from jax.experimental.pallas import tpu as pltpu

F32 = jnp.float32
BF16 = jnp.bfloat16

D_MODEL = 2048
HEAD_A = 64
H_A = D_MODEL // HEAD_A
GN_EPS = 64e-5
GROUPS = ((128, 1), (512, 4), (2048, 16))
N_GROUPS = len(GROUPS)
H_B = 16
HD_B = 64
ATT_STEPS = 128
RMS_EPS = 1e-6

LANES = 128
VMEM_LIMIT = 56 * 1024 * 1024


def _bdot(a, b):
    return jnp.dot(a.astype(BF16), b.astype(BF16), preferred_element_type=F32)


def _bdot_nt(a, b):
    return lax.dot_general(a.astype(BF16), b.astype(BF16), (((1,), (1,)), ((), ())),
                           preferred_element_type=F32)


def _bdot_tn(a, b):
    return lax.dot_general(a.astype(BF16), b.astype(BF16), (((0,), (0,)), ((), ())),
                           preferred_element_type=F32)


def _exact_dot(m, x):
    hi = x.astype(BF16)
    r1 = x - hi.astype(F32)
    mid = r1.astype(BF16)
    lo = (r1 - mid.astype(F32)).astype(BF16)
    mb = m.astype(BF16)
    acc = jnp.dot(mb, hi, preferred_element_type=F32)
    acc = acc + jnp.dot(mb, mid, preferred_element_type=F32)
    return acc + jnp.dot(mb, lo, preferred_element_type=F32)


def _lockstep(chains):
    results = [None] * len(chains)
    live = list(range(len(chains)))
    while live:
        for idx in list(live):
            try:
                next(chains[idx])
            except StopIteration as done:
                results[idx] = done.value
                live.remove(idx)
    return results


def _head_sum(x, lane_lo):
    s0 = jnp.sum(jnp.where(lane_lo, x, 0.0), axis=-1, keepdims=True)
    s1 = jnp.sum(jnp.where(lane_lo, 0.0, x), axis=-1, keepdims=True)
    return jnp.where(lane_lo, s0, s1)


WKV_CHUNK = HEAD_A
(_MK_EYE, _MK_LVL0, _MK_BD, _MK_NMASK, _MK_CAUSAL, _MK_TRI, _MK_LEVELS) = range(7)
_N_LEVELS = 5
_N_MASKS = _MK_LEVELS + _N_LEVELS


def _wkv_fill_masks(mk_ref):
    L = WKV_CHUNK
    r2 = lax.broadcasted_iota(jnp.int32, (LANES, LANES), 0)
    c2 = lax.broadcasted_iota(jnp.int32, (LANES, LANES), 1)
    f = lambda m: m.astype(F32)
    same_block = (r2 < L) == (c2 < L)
    strict = (c2 % L) < (r2 % L)
    incl = (c2 % L) <= (r2 % L)
    mk_ref[_MK_EYE] = f(r2 == c2)
    mk_ref[_MK_LVL0] = f((r2 // 2 == c2 // 2) & (c2 < r2))
    mk_ref[_MK_BD] = f(same_block)
    mk_ref[_MK_NMASK] = f(strict & same_block)
    mk_ref[_MK_CAUSAL] = f(strict | (incl & (r2 >= L)))
    mk_ref[_MK_TRI] = f(incl & (r2 < L) & (c2 < L))
    size = 2
    for lvl in range(_N_LEVELS):
        mk_ref[_MK_LEVELS + lvl] = f((r2 // (2 * size) == c2 // (2 * size))
                                     & ((r2 // size) % 2 == 1) & ((c2 // size) % 2 == 0) & same_block)
        size *= 2


def _wkv_pair_chunk(r, kraw, v, wl, a, g, k_k, k_a, r_k, lnx_w, lnx_b, s2, mk_ref):
    L = WKV_CHUNK
    lane_lo = lax.broadcasted_iota(jnp.int32, (L, LANES), 1) < HEAD_A
    lo2 = lax.broadcasted_iota(jnp.int32, (2 * L, LANES), 1) < HEAD_A

    logd = -jnp.exp(wl)
    kkraw = kraw * k_k
    kk = kkraw / jnp.maximum(jnp.sqrt(_head_sum(kkraw * kkraw, lane_lo)), 1e-12)
    k2 = kraw * (1.0 + (a - 1.0) * k_a)

    cum = _exact_dot(mk_ref[_MK_TRI][:L, :L], logd)
    w_incl = jnp.exp(cum)
    w_excl = jnp.exp(cum - logd)
    w_inv = jnp.exp(-cum)
    w_last = w_incl[L - 1:L, :]

    at = -kk * w_excl
    bt = (kk * a) * w_inv
    kt = k2 * w_inv
    rt = r * w_incl

    ar = jnp.concatenate([at, rt], axis=0)
    bk = jnp.concatenate([bt, kt], axis=0)
    kb = jnp.concatenate([kt, bt], axis=0)
    m0 = _bdot_nt(jnp.where(lo2, ar, 0.0), bk)
    m1 = _bdot_nt(jnp.where(lo2, 0.0, ar), kb)
    aprp = _bdot_nt(ar, s2)
    yield

    a0 = m0[:L]
    a1 = m1[:L]
    n2 = jnp.concatenate([a0, a1], axis=0) * mk_ref[_MK_NMASK]

    t2 = mk_ref[_MK_EYE] + n2 * mk_ref[_MK_LVL0]
    for lvl in range(_N_LEVELS):
        tc = _bdot(t2, n2 * mk_ref[_MK_LEVELS + lvl])
        yield
        t2 = t2 + _bdot(tc, t2)
        yield
    tcat = t2[:L] + t2[L:]

    causal = mk_ref[_MK_CAUSAL]
    strict, incl = causal[:L], causal[L:]
    v_lo = jnp.where(lane_lo, v, 0.0)
    v_hi = jnp.where(lane_lo, 0.0, v)
    ak_cat = jnp.where(lane_lo, a1, a0) * strict
    rhs = aprp[:L] + _bdot(ak_cat, jnp.concatenate([v_hi, v_lo], axis=0))
    yield
    u = _bdot(tcat, jnp.concatenate([jnp.where(lane_lo, rhs, 0.0), jnp.where(lane_lo, 0.0, rhs)], axis=0))
    yield
    u_lo = jnp.where(lane_lo, u, 0.0)
    u_hi = jnp.where(lane_lo, 0.0, u)

    y = (aprp[L:] + _bdot(m0[L:] * incl, jnp.concatenate([u_lo, v_lo], axis=0))
         + _bdot(m1[L:] * incl, jnp.concatenate([v_hi, u_hi], axis=0)))

    delta = _bdot_tn(jnp.concatenate([u, v], axis=0), bk)
    yield
    s_new = (s2 + delta * mk_ref[_MK_BD]) * w_last

    mean = _head_sum(y, lane_lo) * (1.0 / HEAD_A)
    yc = y - mean
    var = _head_sum(yc * yc, lane_lo) * (1.0 / HEAD_A)
    yn = yc * lax.rsqrt(var + GN_EPS) * lnx_w + lnx_b
    bonus = _head_sum(r * k2 * r_k, lane_lo)
    return (yn + bonus * v) * g, s_new


def _wkv_kernel(r_ref, k_ref, v_ref, wl_ref, a_ref, g_ref, kk_ref, ka_ref, rk_ref, lw_ref, lb_ref,
                s0_ref, y_ref, sout_ref, s_scr, mk_scr, *, n_pairs):
    b, pb, c = pl.program_id(0), pl.program_id(1), pl.program_id(2)

    @pl.when((b == 0) & (pb == 0) & (c == 0))
    def _():
        _wkv_fill_masks(mk_scr)

    @pl.when(c == 0)
    def _():
        s_scr[...] = s0_ref[0]

    lanes = [slice(p * LANES, (p + 1) * LANES) for p in range(n_pairs)]
    chains = [_wkv_pair_chunk(
        r_ref[0, :, sl], k_ref[0, :, sl], v_ref[0, :, sl], wl_ref[0, :, sl], a_ref[0, :, sl],
        g_ref[0, :, sl], kk_ref[:, sl], ka_ref[:, sl], rk_ref[:, sl], lw_ref[:, sl], lb_ref[:, sl],
        s_scr[p], mk_scr) for p, sl in enumerate(lanes)]
    for (y, s_new), p, sl in zip(_lockstep(chains), range(n_pairs), lanes):
        y_ref[0, :, sl] = y
        s_scr[p] = s_new

    @pl.when(c == pl.num_programs(2) - 1)
    def _():
        sout_ref[0] = s_scr[...]


def _wkv_call(r, k, v, wlog, a, g, k_k, k_a, r_k, lnx_w, lnx_b, s2_0, *, pairs_per_step):
    B, T, D = r.shape
    L, P = WKV_CHUNK, pairs_per_step
    assert T % L == 0 and 2 * L == LANES and (D // LANES) % P == 0
    seq = pl.BlockSpec((1, L, P * LANES), lambda b, p, c: (b, c, p))
    par = pl.BlockSpec((1, P * LANES), lambda b, p, c: (0, p))
    st = pl.BlockSpec((1, P, LANES, LANES), lambda b, p, c: (b, p, 0, 0))
    row = lambda x: x.reshape(1, D)
    return pl.pallas_call(
        functools.partial(_wkv_kernel, n_pairs=P),
        grid=(B, D // LANES // P, T // L),
        in_specs=[seq] * 6 + [par] * 5 + [st],
        out_specs=[seq, st],
        out_shape=[jax.ShapeDtypeStruct((B, T, D), F32),
                   jax.ShapeDtypeStruct((B, D // LANES, LANES, LANES), F32)],
        scratch_shapes=[pltpu.VMEM((P, LANES, LANES), F32), pltpu.VMEM((_N_MASKS, LANES, LANES), F32)],
        compiler_params=pltpu.CompilerParams(
            dimension_semantics=("arbitrary", "arbitrary", "arbitrary")),
        name="wkv7_chunk_scan",
    )(r, k, v, wlog, a, g, row(k_k), row(k_a), row(r_k), row(lnx_w), row(lnx_b), s2_0)


def _rmsnorm_kernel(x_ref, g_ref, o_ref):
    x = x_ref[...]
    y = x * lax.rsqrt(jnp.mean(x * x, axis=-1, keepdims=True) + RMS_EPS)
    o_ref[...] = (y * g_ref[...]).astype(o_ref.dtype)


def _rmsnorm_call(x, g, out_dtype=F32):
    M, D = x.shape
    bm = min(M, 512)
    return pl.pallas_call(
        _rmsnorm_kernel,
        grid=(M // bm,),
        in_specs=[pl.BlockSpec((bm, D), lambda m: (m, 0)), pl.BlockSpec((1, D), lambda m: (0, 0))],
        out_specs=pl.BlockSpec((bm, D), lambda m: (m, 0)),
        out_shape=jax.ShapeDtypeStruct((M, D), out_dtype),
        compiler_params=pltpu.CompilerParams(dimension_semantics=("parallel",)),
        name="rmsnorm",
    )(x, g.reshape(1, D))


def _mm_kernel(*refs, n_x, n_e, pre, post, nk):
    x_refs = refs[:n_x]
    w_ref = refs[n_x]
    e_refs = refs[n_x + 1:n_x + 1 + n_e]
    o_ref = refs[n_x + 1 + n_e]
    x = pre(*[r[...] for r in x_refs])
    part = jnp.dot(x.astype(BF16), w_ref[...].astype(BF16), preferred_element_type=F32)
    if nk == 1:
        o_ref[...] = post(part, *[e[...] for e in e_refs]).astype(o_ref.dtype)
        return
    acc_ref = refs[-1]
    k = pl.program_id(2)

    @pl.when(k == 0)
    def _():
        acc_ref[...] = part

    @pl.when(k > 0)
    def _():
        acc_ref[...] += part

    @pl.when(k == nk - 1)
    def _():
        o_ref[...] = post(acc_ref[...], *[e[...] for e in e_refs]).astype(o_ref.dtype)


def _mm_rows_kernel(*refs, n_x, n_e, pre, post):
    x_refs = refs[:n_x]
    w_ref = refs[n_x]
    e_refs = refs[n_x + 1:n_x + 1 + n_e]
    o_ref = refs[n_x + 1 + n_e]
    x_scr = refs[-1]

    @pl.when(pl.program_id(1) == 0)
    def _():
        x_scr[...] = pre(*[r[...] for r in x_refs]).astype(BF16)

    part = jnp.dot(x_scr[...], w_ref[...].astype(BF16), preferred_element_type=F32)
    o_ref[...] = post(part, *[e[...] for e in e_refs]).astype(o_ref.dtype)


def _identity(x):
    return x


def _matmul_rows_call(xs, w, extras=(), *, pre=_identity, post=_identity, out_dtype=F32,
                      bm=512, bn=512, name="matmul"):
    K, N = w.shape
    M = max(x.shape[0] for x in xs)
    bm = min(bm, M)
    bn = min(bn, N)
    assert M % bm == 0 and N % bn == 0
    x_specs = [pl.BlockSpec((bm, K), lambda m, n: (m, 0)) if x.shape[0] == M
               else pl.BlockSpec((x.shape[0], K), lambda m, n: (0, 0)) for x in xs]
    e_specs = [pl.BlockSpec((bm, bn), lambda m, n: (m, n)) if e.shape[0] == M
               else pl.BlockSpec((1, bn), lambda m, n: (0, n)) for e in extras]
    return pl.pallas_call(
        functools.partial(_mm_rows_kernel, n_x=len(xs), n_e=len(extras), pre=pre, post=post),
        grid=(M // bm, N // bn),
        in_specs=x_specs + [pl.BlockSpec((K, bn), lambda m, n: (0, n))] + e_specs,
        out_specs=pl.BlockSpec((bm, bn), lambda m, n: (m, n)),
        out_shape=jax.ShapeDtypeStruct((M, N), out_dtype),
        scratch_shapes=[pltpu.VMEM((bm, K), BF16)],
        compiler_params=pltpu.CompilerParams(
            dimension_semantics=("parallel", "arbitrary"), vmem_limit_bytes=VMEM_LIMIT),
        name=name,
    )(*xs, w, *extras)


def _matmul_call(xs, w, extras=(), *, pre=_identity, post=_identity, out_dtype=F32,
                 bm=512, bn=512, bk=None, name="matmul"):
    K, N = w.shape
    M = max(x.shape[0] for x in xs)
    bm = min(bm, M)
    bn = min(bn, N)
    bk = K if bk is None else min(bk, K)
    assert M % bm == 0 and N % bn == 0 and K % bk == 0
    nk = K // bk
    x_specs = [pl.BlockSpec((bm, bk), lambda n, m, k: (m, k)) if x.shape[0] == M
               else pl.BlockSpec((x.shape[0], bk), lambda n, m, k: (0, k)) for x in xs]
    e_specs = [pl.BlockSpec((bm, bn), lambda n, m, k: (m, n)) if e.shape[0] == M
               else pl.BlockSpec((1, bn), lambda n, m, k: (0, n)) for e in extras]
    return pl.pallas_call(
        functools.partial(_mm_kernel, n_x=len(xs), n_e=len(extras), pre=pre, post=post, nk=nk),
        grid=(N // bn, M // bm, nk),
        in_specs=x_specs + [pl.BlockSpec((bk, bn), lambda n, m, k: (k, n))] + e_specs,
        out_specs=pl.BlockSpec((bm, bn), lambda n, m, k: (m, n)),
        out_shape=jax.ShapeDtypeStruct((M, N), out_dtype),
        scratch_shapes=[pltpu.VMEM((bm, bn), F32)] if nk > 1 else [],
        compiler_params=pltpu.CompilerParams(
            dimension_semantics=("parallel", "parallel", "arbitrary"),
            vmem_limit_bytes=VMEM_LIMIT),
        name=name,
    )(*xs, w, *extras)


def _lora_kernel(*refs, n_x, n_e, pre, mid, post):
    x_refs = refs[:n_x]
    w1_ref, w2_ref = refs[n_x], refs[n_x + 1]
    e_refs = refs[n_x + 2:n_x + 2 + n_e]
    o_ref = refs[n_x + 2 + n_e]
    x = pre(*[r[...] for r in x_refs])
    z = mid(_bdot(x, w1_ref[...]))
    o_ref[...] = post(_bdot(z, w2_ref[...]), *[e[...] for e in e_refs]).astype(o_ref.dtype)


def _lora_call(xs, w1, w2, extras=(), *, pre=_identity, mid=_identity, post=_identity, bm=512, name="lora"):
    K, R = w1.shape
    N = w2.shape[1]
    M = max(x.shape[0] for x in xs)
    bm = min(bm, M)
    x_specs = [pl.BlockSpec((bm, K), lambda m: (m, 0)) if x.shape[0] == M
               else pl.BlockSpec((x.shape[0], K), lambda m: (0, 0)) for x in xs]
    e_specs = [pl.BlockSpec((bm, N), lambda m: (m, 0)) if e.shape[0] == M
               else pl.BlockSpec((1, N), lambda m: (0, 0)) for e in extras]
    return pl.pallas_call(
        functools.partial(_lora_kernel, n_x=len(xs), n_e=len(extras), pre=pre, mid=mid, post=post),
        grid=(M // bm,),
        in_specs=x_specs + [pl.BlockSpec((K, R), lambda m: (0, 0)), pl.BlockSpec((R, N), lambda m: (0, 0))]
        + e_specs,
        out_specs=pl.BlockSpec((bm, N), lambda m: (m, 0)),
        out_shape=jax.ShapeDtypeStruct((M, N), F32),
        compiler_params=pltpu.CompilerParams(dimension_semantics=("parallel",),
                                             vmem_limit_bytes=VMEM_LIMIT),
        name=name,
    )(*xs, w1, w2, *extras)


def _swa_kernel(slopes_ref, q_ref, kp_ref, vp_ref, kc_ref, vc_ref, o_ref, lse_ref, *, tq, n_cur, dil, group):
    tp = ATT_STEPS
    lane_q = lax.broadcasted_iota(jnp.int32, (tq, LANES), 1) < HD_B
    iq_p = lax.broadcasted_iota(jnp.int32, (tq, tp), 0)
    jk_p = lax.broadcasted_iota(jnp.int32, (tq, tp), 1)
    steps_p = iq_p + tp - jk_p
    valid_p = steps_p <= ATT_STEPS
    iq_c = lax.broadcasted_iota(jnp.int32, (tq, tq), 0)
    jk_c = lax.broadcasted_iota(jnp.int32, (tq, tq), 1)
    steps_c = iq_c - jk_c
    valid_c = (steps_c >= 0) & (jk_c < n_cur)
    scale = HD_B ** -0.5

    def head(hp, h):
        sl = slice(hp * LANES, (hp + 1) * LANES)
        slope = slopes_ref[group * H_B + 2 * hp + h]
        q = q_ref[0, :, sl]
        qh = jnp.where(lane_q, q, 0.0) if h == 0 else jnp.where(lane_q, 0.0, q)
        sp = _bdot_nt(qh, kp_ref[0, :, sl])
        sc = _bdot_nt(qh, kc_ref[0, :, sl])
        yield
        sp = sp * scale - slope * (steps_p * dil).astype(F32)
        sc = sc * scale - slope * (steps_c * dil).astype(F32)
        sp = jnp.where(valid_p, sp, -jnp.inf)
        sc = jnp.where(valid_c, sc, -jnp.inf)
        m = jnp.maximum(jnp.max(sp, axis=-1, keepdims=True), jnp.max(sc, axis=-1, keepdims=True))
        pp = jnp.exp(sp - m)
        pc = jnp.exp(sc - m)
        l = jnp.sum(pp, axis=-1, keepdims=True) + jnp.sum(pc, axis=-1, keepdims=True)
        acc = _bdot(pp, vp_ref[0, :, sl]) + _bdot(pc, vc_ref[0, :, sl])
        yield
        return acc / l, jnp.broadcast_to(m + jnp.log(l), (tq, LANES))

    n_hp = H_B // 2
    res = _lockstep([head(hp, h) for hp in range(n_hp) for h in range(2)])
    for hp in range(n_hp):
        sl = slice(hp * LANES, (hp + 1) * LANES)
        (o0, l0), (o1, l1) = res[2 * hp], res[2 * hp + 1]
        o_ref[0, :, sl] = jnp.where(lane_q, o0, o1)
        lse_ref[0, :, sl] = jnp.where(lane_q, l0, l1)


def _swa_call(slopes, new, cache, *, n_streams, tq, n_cur, dil, group, name):
    B = new.shape[0]
    hw = H_B * HD_B
    in_specs = [
        pl.BlockSpec(memory_space=pltpu.SMEM),
        pl.BlockSpec((1, tq, hw), lambda b, c: (b, 0, 3 * c)),
        pl.BlockSpec((1, ATT_STEPS, hw), lambda b, c: (b, 0, 2 * c)),
        pl.BlockSpec((1, ATT_STEPS, hw), lambda b, c: (b, 0, 2 * c + 1)),
        pl.BlockSpec((1, tq, hw), lambda b, c: (b, 0, 3 * c + 1)),
        pl.BlockSpec((1, tq, hw), lambda b, c: (b, 0, 3 * c + 2)),
    ]
    out_spec = pl.BlockSpec((1, tq, hw), lambda b, c: (b, 0, c))
    out_sds = jax.ShapeDtypeStruct((B, tq, n_streams * hw), F32)
    return pl.pallas_call(
        functools.partial(_swa_kernel, tq=tq, n_cur=n_cur, dil=dil, group=group),
        grid=(B, n_streams),
        in_specs=in_specs,
        out_specs=[out_spec, out_spec],
        out_shape=[out_sds, out_sds],
        compiler_params=pltpu.CompilerParams(dimension_semantics=("parallel", "parallel")),
        name=name,
    )(slopes, new, cache, cache, new, new)


ATT_BLOCK = ATT_STEPS * max(d for _, d in GROUPS)


def _attn_prompt_kernel(slopes_ref, *refs):
    ng = N_GROUPS
    q_refs, kc_refs, vc_refs = refs[0:ng], refs[ng:2 * ng], refs[2 * ng:3 * ng]
    kp_refs, vp_refs = refs[3 * ng:4 * ng], refs[4 * ng:5 * ng]
    o_ref = refs[5 * ng]
    m_scr, l_scr, acc_scr = refs[5 * ng + 1:]
    hp = pl.program_id(1)
    j = pl.program_id(2)
    S = ATT_STEPS
    lane_lo = lax.broadcasted_iota(jnp.int32, (S, LANES), 1) < HD_B
    iq = lax.broadcasted_iota(jnp.int32, (S, S), 0)
    jk = lax.broadcasted_iota(jnp.int32, (S, S), 1)
    steps_p = iq + S - jk
    steps_c = iq - jk
    mask_p = jnp.where(steps_p <= S, 0.0, -jnp.inf)
    mask_c = jnp.where(steps_c >= 0, 0.0, -jnp.inf)
    scale = HD_B ** -0.5
    no_prev = jnp.where(j > 0, 0.0, -jnp.inf)

    for g, (_, dil) in enumerate(GROUPS):
        biases = []
        for h in range(2):
            slope = slopes_ref[g * H_B + 2 * hp + h]
            biases.append((mask_p - slope * (steps_p * dil).astype(F32),
                           mask_c - slope * (steps_c * dil).astype(F32)))
        q_ref, kc_ref, vc_ref, kp_ref, vp_ref = q_refs[g], kc_refs[g], vc_refs[g], kp_refs[g], vp_refs[g]

        def attend(cur_start, prev_ref, prev_start, prev_bias, dil=dil, g=g, biases=biases, q_ref=q_ref,
                   kc_ref=kc_ref, vc_ref=vc_ref):
            cur = pl.ds(cur_start, S, stride=dil) if dil > 1 else pl.ds(cur_start, S)
            prv = pl.ds(prev_start, S, stride=dil) if dil > 1 else pl.ds(prev_start, S)
            q, kc, vc = q_ref[0, cur, :], kc_ref[0, cur, :], vc_ref[0, cur, :]
            kp, vp = prev_ref[0][0, prv, :], prev_ref[1][0, prv, :]
            qs = [jnp.where(lane_lo, q, 0.0), jnp.where(lane_lo, 0.0, q)]
            scores = [(_bdot_nt(qh, kp), _bdot_nt(qh, kc)) for qh in qs]
            yield
            ms, ls, probs = [], [], []
            for h in range(2):
                sp = scores[h][0] * scale + biases[h][0]
                if prev_bias is not None:
                    sp = sp + prev_bias
                sc = scores[h][1] * scale + biases[h][1]
                m = jnp.max(jnp.maximum(sp, sc), axis=-1, keepdims=True)
                pp = jnp.exp(sp - m)
                pc = jnp.exp(sc - m)
                ls.append(jnp.sum(pp + pc, axis=-1, keepdims=True))
                probs.append((pp, pc))
                ms.append(m)
            accs = [_bdot(pp, vp) + _bdot(pc, vc) for pp, pc in probs]
            yield
            m = jnp.where(lane_lo, ms[0], ms[1])
            l = jnp.where(lane_lo, ls[0], ls[1])
            acc = jnp.where(lane_lo, accs[0], accs[1])
            if g > 0:
                m_old = m_scr[cur, :]
                m_new = jnp.maximum(m_old, m)
                w_old, w_new = jnp.exp(m_old - m_new), jnp.exp(m - m_new)
                l = l_scr[cur, :] * w_old + l * w_new
                acc = acc_scr[cur, :] * w_old + acc * w_new
                m = m_new
            m_scr[cur, :] = m
            l_scr[cur, :] = l
            acc_scr[cur, :] = acc

        sd = S * dil
        n_sub = ATT_BLOCK // sd
        run = lambda items, attend=attend: _lockstep([attend(*it) for it in items])
        first = lambda c, kp_ref=kp_ref, vp_ref=vp_ref: (c, (kp_ref, vp_ref), c, no_prev)
        later = lambda c, i, sd=sd, kc_ref=kc_ref, vc_ref=vc_ref: (
            i * sd + c, (kc_ref, vc_ref), (i - 1) * sd + c, None)

        def loop(lo, hi, fn):
            def body(idx, carry):
                fn(idx)
                return carry
            lax.fori_loop(lo, hi, body, 0)

        if dil == 1:
            hb = n_sub // 2
            run([first(0), later(0, hb)])
            loop(1, hb, lambda i, hb=hb: run([later(0, i), later(0, i + hb)]))
        elif n_sub == 1:
            loop(0, dil // 2, lambda c, dil=dil: run([first(c), first(c + dil // 2)]))
        else:
            for c in range(dil // 2):
                c2 = c + dil // 2
                run([first(c), first(c2)])
                loop(1, n_sub, lambda i, c=c, c2=c2: run([later(c, i), later(c2, i)]))

    o_ref[0] = (acc_scr[...] / l_scr[...]).astype(o_ref.dtype)


def _attn_prompt_call(qkv, slopes):
    B, T, W = qkv.shape
    hw = H_B * HD_B
    n_hp = H_B // 2
    TB = ATT_BLOCK
    assert T % TB == 0
    col = lambda g, which, hp: (g * 3 * hw + which * hw) // LANES + hp
    cur = lambda g, which: pl.BlockSpec((1, TB, LANES), lambda b, hp, j: (b, j, col(g, which, hp)))
    def prev(g, which):
        rows = ATT_STEPS * GROUPS[g][1]
        per = TB // rows
        return pl.BlockSpec((1, rows, LANES),
                            lambda b, hp, j: (b, jnp.maximum(j * per - 1, 0), col(g, which, hp)))
    gs = range(N_GROUPS)
    in_specs = ([pl.BlockSpec(memory_space=pltpu.SMEM)]
                + [cur(g, 0) for g in gs] + [cur(g, 1) for g in gs] + [cur(g, 2) for g in gs]
                + [prev(g, 1) for g in gs] + [prev(g, 2) for g in gs])
    return pl.pallas_call(
        _attn_prompt_kernel,
        grid=(B, n_hp, T // TB),
        in_specs=in_specs,
        out_specs=pl.BlockSpec((1, TB, LANES), lambda b, hp, j: (b, j, hp)),
        out_shape=jax.ShapeDtypeStruct((B, T, hw), BF16),
        scratch_shapes=[pltpu.VMEM((TB, LANES), F32)] * 3,
        compiler_params=pltpu.CompilerParams(
            dimension_semantics=("parallel", "parallel", "arbitrary"), vmem_limit_bytes=VMEM_LIMIT),
        name="attn_prompt",
    )(slopes, *([qkv] * (5 * N_GROUPS)))


def _merge_groups(o0, l0, o1, l1, o2, l2):
    m = jnp.maximum(jnp.maximum(l0, l1), l2)
    w0, w1, w2 = jnp.exp(l0 - m), jnp.exp(l1 - m), jnp.exp(l2 - m)
    return (w0 * o0 + w1 * o1 + w2 * o2) / (w0 + w1 + w2)


def _alibi_slopes():
    n = N_GROUPS * H_B
    return 2.0 ** (-8.0 * jnp.arange(1, n + 1, dtype=F32) / n)


def _pair_states(s):
    B, H = s.shape[:2]
    s = s.reshape(B, H // 2, 2, HEAD_A, HEAD_A)
    z = jnp.zeros_like(s[:, :, 0])
    top = jnp.concatenate([s[:, :, 0], z], axis=-1)
    bot = jnp.concatenate([z, s[:, :, 1]], axis=-1)
    return jnp.concatenate([top, bot], axis=-2)


def _unpair_states(s2):
    B, P = s2.shape[:2]
    return jnp.stack([s2[:, :, :HEAD_A, :HEAD_A], s2[:, :, HEAD_A:, HEAD_A:]],
                     axis=2).reshape(B, 2 * P, HEAD_A, HEAD_A)


def _mix(i):
    def pre(h, prev, mu):
        return h + (prev - h) * mu[i:i + 1, :]
    return pre


def _rwkv_block(x, shift0, wkv0, norm_g, p):
    (mu, w0, w1, w2, a0, a1, a2, g1, g2, k_k, k_a, r_k, lnx_w, lnx_b, w_r, w_k, w_v, w_o) = p
    B, T, D = x.shape
    M = B * T
    x2 = x.reshape(M, D)
    h2 = _rmsnorm_call(x2, norm_g)
    h = h2.reshape(B, T, D)
    prev2 = jnp.concatenate([shift0[:, None, :], h[:, :-1]], axis=1).reshape(M, D)
    hx = [h2, prev2, mu]
    r = _matmul_rows_call(hx, w_r, pre=_mix(0), name="rwkv_r")
    k = _matmul_rows_call(hx, w_k, pre=_mix(2), name="rwkv_k")
    v = _matmul_rows_call(hx, w_v, pre=_mix(3), name="rwkv_v")
    wlog = _lora_call(hx, w1, w2, [w0.reshape(1, D)], pre=_mix(1), mid=jnp.tanh,
                      post=lambda z, b: -jax.nn.softplus(-(b + z)) - 0.5, name="rwkv_w")
    a = _lora_call(hx, a1, a2, [a0.reshape(1, D)], pre=_mix(4),
                   post=lambda z, b: jax.nn.sigmoid(b + z), name="rwkv_a")
    g = _lora_call(hx, g1, g2, pre=_mix(5), mid=jax.nn.sigmoid, name="rwkv_g")

    L = WKV_CHUNK
    Tp = -(-T // L) * L
    def seq(z, fill=0.0):
        z = z.reshape(B, T, D)
        if Tp != T:
            z = jnp.pad(z, ((0, 0), (0, Tp - T), (0, 0)), constant_values=fill)
        return z
    y, s2 = _wkv_call(seq(r), seq(k), seq(v), seq(wlog, -jnp.inf), seq(a), seq(g),
                      k_k, k_a, r_k.reshape(D), lnx_w, lnx_b, _pair_states(wkv0), pairs_per_step=D // LANES)
    y2 = y[:, :T].reshape(M, D)
    out = _matmul_call([y2], w_o, [x2], post=lambda acc, res: res + acc, name="rwkv_o")
    return out.reshape(B, T, D), _unpair_states(s2), h[:, -1]


def _ffn_block(x2, norm_g, w_up, w_down):
    hn = _rmsnorm_call(x2, norm_g, BF16)
    mid = _matmul_call([hn], w_up, post=lambda acc: jnp.square(jnp.maximum(acc, 0.0)),
                       out_dtype=BF16, bm=1024, bn=1024, name="ffn_up")
    return _matmul_call([mid], w_down, [x2], post=lambda acc, res: res + acc,
                        bm=1024, bn=1024, bk=2048, name="ffn_down")


def _attn_sample(qkv, caches, slopes):
    B, S, W = qkv.shape
    gw = 3 * H_B * HD_B
    hw = H_B * HD_B
    TQ = 8
    res = []
    for g, (window, dil) in enumerate(GROUPS):
        dd = min(dil, S)
        nq = S // dd
        new = qkv[:, :, g * gw:(g + 1) * gw].reshape(B, nq, dd * gw)
        new = jnp.pad(new, ((0, 0), (0, TQ - nq), (0, 0)))
        cache = caches[g].reshape(B, window // dil, dil * 2 * hw)
        o, lse = _swa_call(slopes, new, cache, n_streams=dd, tq=TQ, n_cur=nq, dil=dil, group=g,
                           name=f"attn_sample_g{g}")
        res += [o[:, :nq].reshape(B * S, hw), lse[:, :nq].reshape(B * S, hw)]
    return res


def _attn_block(x, norm_g, w_qkv, w_o, slopes, caches=None):
    B, T, D = x.shape
    M = B * T
    x2 = x.reshape(M, D)
    hn = _rmsnorm_call(x2, norm_g, BF16)
    qkv = _matmul_call([hn], w_qkv, bm=1024, bn=1024, name="attn_qkv").reshape(B, T, -1)
    q5 = qkv.reshape(B, T, N_GROUPS, 3, H_B, HD_B)
    add_residual = lambda acc, res: res + acc
    if caches is None:
        merged = _attn_prompt_call(qkv, slopes).reshape(M, H_B * HD_B)
        bufs = [q5[:, T - min(w, T):, g, 1:] for g, (w, _) in enumerate(GROUPS)]
        out = _matmul_call([merged], w_o, [x2], post=add_residual, bm=1024, name="attn_o")
    else:
        parts = _attn_sample(qkv, caches, slopes)
        bufs = [jnp.concatenate([caches[g], q5[:, :, g, 1:]], axis=1)[:, T:] for g in range(N_GROUPS)]
        out = _matmul_rows_call(parts, w_o, [x2], pre=_merge_groups, post=add_residual, name="attn_o_merge")
    return out.reshape(B, T, D), bufs


def kernel(x_prompt, x_sample, state_wkv, state_shift, cache_kv_g1, cache_kv_g2, cache_kv_g3,
           norm_mix, norm_ffn, norm_final,
           rwkv_mu, rwkv_w0, rwkv_w1, rwkv_w2, rwkv_a0, rwkv_a1, rwkv_a2, rwkv_g1, rwkv_g2,
           rwkv_k_k, rwkv_k_a, rwkv_r_k, rwkv_lnx_w, rwkv_lnx_b, rwkv_w_r, rwkv_w_k, rwkv_w_v, rwkv_w_o,
           attn_w_qkv, attn_w_o, ffn_w_up, ffn_w_down):
    rwkv_params = (rwkv_mu, rwkv_w0, rwkv_w1, rwkv_w2, rwkv_a0, rwkv_a1, rwkv_a2, rwkv_g1, rwkv_g2,
                   rwkv_k_k, rwkv_k_a, rwkv_r_k, rwkv_lnx_w, rwkv_lnx_b,
                   rwkv_w_r, rwkv_w_k, rwkv_w_v, rwkv_w_o)
    slopes = _alibi_slopes()
    Bp, Tp, D = x_prompt.shape
    Bs, Ts, _ = x_sample.shape

    def ffn(x, i):
        B, T, _ = x.shape
        return _ffn_block(x.reshape(B * T, D), norm_ffn[i], ffn_w_up[i], ffn_w_down[i]).reshape(B, T, D)

    xp, wkv_p, shift_p = _rwkv_block(x_prompt, jnp.zeros((Bp, D), F32),
                                     jnp.zeros((Bp, H_A, HEAD_A, HEAD_A), F32), norm_mix[0], rwkv_params)
    xs, wkv_s, shift_s = _rwkv_block(x_sample, state_shift, state_wkv, norm_mix[0], rwkv_params)
    xp, xs = ffn(xp, 0), ffn(xs, 0)
    xp, (kv1_p, kv2_p, kv3_p) = _attn_block(xp, norm_mix[1], attn_w_qkv, attn_w_o, slopes)
    xs, (kv1_s, kv2_s, kv3_s) = _attn_block(xs, norm_mix[1], attn_w_qkv, attn_w_o, slopes,
                                            (cache_kv_g1, cache_kv_g2, cache_kv_g3))
    xp, xs = ffn(xp, 1), ffn(xs, 1)
    y_prompt = _rmsnorm_call(xp.reshape(Bp * Tp, D), norm_final).reshape(Bp, Tp, D)
    y_sample = _rmsnorm_call(xs.reshape(Bs * Ts, D), norm_final).reshape(Bs, Ts, D)
    return (y_prompt, y_sample, wkv_p, shift_p, kv1_p, kv2_p, kv3_p,
            wkv_s, shift_s, kv1_s, kv2_s, kv3_s)
```

```python
import functools

import jax
import jax.numpy as jnp
from jax import lax
from jax.experimental import pallas as pl
from jax.experimental.pallas import tpu as pltpu

F32 = jnp.float32
BF16 = jnp.bfloat16

D_MODEL = 2048
HEAD_A = 64
H_A = D_MODEL // HEAD_A
GN_EPS = 64e-5
GROUPS = ((128, 1), (512, 4), (2048, 16))
N_GROUPS = len(GROUPS)
H_B = 16
HD_B = 64
ATT_STEPS = 128
RMS_EPS = 1e-6

LANES = 128
VMEM_LIMIT = 56 * 1024 * 1024


def _bdot(a, b):
    return jnp.dot(a.astype(BF16), b.astype(BF16), preferred_element_type=F32)


def _bdot_nt(a, b):
    return lax.dot_general(a.astype(BF16), b.astype(BF16), (((1,), (1,)), ((), ())),
                           preferred_element_type=F32)


def _bdot_tn(a, b):
    return lax.dot_general(a.astype(BF16), b.astype(BF16), (((0,), (0,)), ((), ())),
                           preferred_element_type=F32)


def _exact_dot(m, x):
    hi = x.astype(BF16)
    r1 = x - hi.astype(F32)
    mid = r1.astype(BF16)
    lo = (r1 - mid.astype(F32)).astype(BF16)
    mb = m.astype(BF16)
    acc = jnp.dot(mb, hi, preferred_element_type=F32)
    acc = acc + jnp.dot(mb, mid, preferred_element_type=F32)
    return acc + jnp.dot(mb, lo, preferred_element_type=F32)


def _lockstep(chains):
    results = [None] * len(chains)
    live = list(range(len(chains)))
    while live:
        for idx in list(live):
            try:
                next(chains[idx])
            except StopIteration as done:
                results[idx] = done.value
                live.remove(idx)
    return results


def _head_sum(x, lane_lo):
    s0 = jnp.sum(jnp.where(lane_lo, x, 0.0), axis=-1, keepdims=True)
    s1 = jnp.sum(jnp.where(lane_lo, 0.0, x), axis=-1, keepdims=True)
    return jnp.where(lane_lo, s0, s1)


WKV_CHUNK = HEAD_A
(_MK_EYE, _MK_LVL0, _MK_BD, _MK_NMASK, _MK_CAUSAL, _MK_TRI, _MK_LEVELS) = range(7)
_N_LEVELS = 5
_N_MASKS = _MK_LEVELS + _N_LEVELS


def _wkv_fill_masks(mk_ref):
    L = WKV_CHUNK
    r2 = lax.broadcasted_iota(jnp.int32, (LANES, LANES), 0)
    c2 = lax.broadcasted_iota(jnp.int32, (LANES, LANES), 1)
    f = lambda m: m.astype(F32)
    same_block = (r2 < L) == (c2 < L)
    strict = (c2 % L) < (r2 % L)
    incl = (c2 % L) <= (r2 % L)
    mk_ref[_MK_EYE] = f(r2 == c2)
    mk_ref[_MK_LVL0] = f((r2 // 2 == c2 // 2) & (c2 < r2))
    mk_ref[_MK_BD] = f(same_block)
    mk_ref[_MK_NMASK] = f(strict & same_block)
    mk_ref[_MK_CAUSAL] = f(strict | (incl & (r2 >= L)))
    mk_ref[_MK_TRI] = f(incl & (r2 < L) & (c2 < L))
    size = 2
    for lvl in range(_N_LEVELS):
        mk_ref[_MK_LEVELS + lvl] = f((r2 // (2 * size) == c2 // (2 * size))
                                     & ((r2 // size) % 2 == 1) & ((c2 // size) % 2 == 0) & same_block)
        size *= 2


def _wkv_pair_chunk(r, kraw, v, wl, a, g, k_k, k_a, r_k, lnx_w, lnx_b, s2, mk_ref):
    L = WKV_CHUNK
    lane_lo = lax.broadcasted_iota(jnp.int32, (L, LANES), 1) < HEAD_A
    lo2 = lax.broadcasted_iota(jnp.int32, (2 * L, LANES), 1) < HEAD_A

    logd = -jnp.exp(wl)
    kkraw = kraw * k_k
    kk = kkraw / jnp.maximum(jnp.sqrt(_head_sum(kkraw * kkraw, lane_lo)), 1e-12)
    k2 = kraw * (1.0 + (a - 1.0) * k_a)

    cum = _exact_dot(mk_ref[_MK_TRI][:L, :L], logd)
    w_incl = jnp.exp(cum)
    w_excl = jnp.exp(cum - logd)
    w_inv = jnp.exp(-cum)
    w_last = w_incl[L - 1:L, :]

    at = -kk * w_excl
    bt = (kk * a) * w_inv
    kt = k2 * w_inv
    rt = r * w_incl

    ar = jnp.concatenate([at, rt], axis=0)
    bk = jnp.concatenate([bt, kt], axis=0)
    kb = jnp.concatenate([kt, bt], axis=0)
    m0 = _bdot_nt(jnp.where(lo2, ar, 0.0), bk)
    m1 = _bdot_nt(jnp.where(lo2, 0.0, ar), kb)
    aprp = _bdot_nt(ar, s2)
    yield

    a0 = m0[:L]
    a1 = m1[:L]
    n2 = jnp.concatenate([a0, a1], axis=0) * mk_ref[_MK_NMASK]

    t2 = mk_ref[_MK_EYE] + n2 * mk_ref[_MK_LVL0]
    for lvl in range(_N_LEVELS):
        tc = _bdot(t2, n2 * mk_ref[_MK_LEVELS + lvl])
        yield
        t2 = t2 + _bdot(tc, t2)
        yield
    tcat = t2[:L] + t2[L:]

    causal = mk_ref[_MK_CAUSAL]
    strict, incl = causal[:L], causal[L:]
    v_lo = jnp.where(lane_lo, v, 0.0)
    v_hi = jnp.where(lane_lo, 0.0, v)
    ak_cat = jnp.where(lane_lo, a1, a0) * strict
    rhs = aprp[:L] + _bdot(ak_cat, jnp.concatenate([v_hi, v_lo], axis=0))
    yield
    u = _bdot(tcat, jnp.concatenate([jnp.where(lane_lo, rhs, 0.0), jnp.where(lane_lo, 0.0, rhs)], axis=0))
    yield
    u_lo = jnp.where(lane_lo, u, 0.0)
    u_hi = jnp.where(lane_lo, 0.0, u)

    y = (aprp[L:] + _bdot(m0[L:] * incl, jnp.concatenate([u_lo, v_lo], axis=0))
         + _bdot(m1[L:] * incl, jnp.concatenate([v_hi, u_hi], axis=0)))

    delta = _bdot_tn(jnp.concatenate([u, v], axis=0), bk)
    yield
    s_new = (s2 + delta * mk_ref[_MK_BD]) * w_last

    mean = _head_sum(y, lane_lo) * (1.0 / HEAD_A)
    yc = y - mean
    var = _head_sum(yc * yc, lane_lo) * (1.0 / HEAD_A)
    yn = yc * lax.rsqrt(var + GN_EPS) * lnx_w + lnx_b
    bonus = _head_sum(r * k2 * r_k, lane_lo)
    return (yn + bonus * v) * g, s_new


def _wkv_kernel(r_ref, k_ref, v_ref, wl_ref, a_ref, g_ref, kk_ref, ka_ref, rk_ref, lw_ref, lb_ref,
                s0_ref, y_ref, sout_ref, s_scr, mk_scr, *, n_pairs):
    b, pb, c = pl.program_id(0), pl.program_id(1), pl.program_id(2)

    @pl.when((b == 0) & (pb == 0) & (c == 0))
    def _():
        _wkv_fill_masks(mk_scr)

    @pl.when(c == 0)
    def _():
        s_scr[...] = s0_ref[0]

    lanes = [slice(p * LANES, (p + 1) * LANES) for p in range(n_pairs)]
    chains = [_wkv_pair_chunk(
        r_ref[0, :, sl], k_ref[0, :, sl], v_ref[0, :, sl], wl_ref[0, :, sl], a_ref[0, :, sl],
        g_ref[0, :, sl], kk_ref[:, sl], ka_ref[:, sl], rk_ref[:, sl], lw_ref[:, sl], lb_ref[:, sl],
        s_scr[p], mk_scr) for p, sl in enumerate(lanes)]
    for (y, s_new), p, sl in zip(_lockstep(chains), range(n_pairs), lanes):
        y_ref[0, :, sl] = y.astype(y_ref.dtype)
        s_scr[p] = s_new

    @pl.when(c == pl.num_programs(2) - 1)
    def _():
        sout_ref[0] = s_scr[...]


def _wkv_call(r, k, v, wlog, a, g, k_k, k_a, r_k, lnx_w, lnx_b, s2_0, *, pairs_per_step):
    B, T, D = r.shape
    L, P = WKV_CHUNK, pairs_per_step
    assert T % L == 0 and 2 * L == LANES and (D // LANES) % P == 0
    seq = pl.BlockSpec((1, L, P * LANES), lambda b, p, c: (b, c, p))
    par = pl.BlockSpec((1, P * LANES), lambda b, p, c: (0, p))
    st = pl.BlockSpec((1, P, LANES, LANES), lambda b, p, c: (b, p, 0, 0))
    row = lambda x: x.reshape(1, D)
    return pl.pallas_call(
        functools.partial(_wkv_kernel, n_pairs=P),
        grid=(B, D // LANES // P, T // L),
        in_specs=[seq] * 6 + [par] * 5 + [st],
        out_specs=[seq, st],
        out_shape=[jax.ShapeDtypeStruct((B, T, D), BF16),
                   jax.ShapeDtypeStruct((B, D // LANES, LANES, LANES), F32)],
        scratch_shapes=[pltpu.VMEM((P, LANES, LANES), F32), pltpu.VMEM((_N_MASKS, LANES, LANES), F32)],
        compiler_params=pltpu.CompilerParams(
            dimension_semantics=("arbitrary", "arbitrary", "arbitrary")),
        name="wkv7_chunk_scan",
    )(r, k, v, wlog, a, g, row(k_k), row(k_a), row(r_k), row(lnx_w), row(lnx_b), s2_0)


def _rmsnorm_kernel(x_ref, g_ref, o_ref):
    x = x_ref[...]
    y = x * lax.rsqrt(jnp.mean(x * x, axis=-1, keepdims=True) + RMS_EPS)
    o_ref[...] = (y * g_ref[...]).astype(o_ref.dtype)


def _rmsnorm_call(x, g, out_dtype=F32):
    M, D = x.shape
    bm = min(M, 512)
    return pl.pallas_call(
        _rmsnorm_kernel,
        grid=(M // bm,),
        in_specs=[pl.BlockSpec((bm, D), lambda m: (m, 0)), pl.BlockSpec((1, D), lambda m: (0, 0))],
        out_specs=pl.BlockSpec((bm, D), lambda m: (m, 0)),
        out_shape=jax.ShapeDtypeStruct((M, D), out_dtype),
        compiler_params=pltpu.CompilerParams(dimension_semantics=("parallel",)),
        name="rmsnorm",
    )(x, g.reshape(1, D))


def _mm_kernel(*refs, n_x, n_e, pre, post, nk):
    x_refs = refs[:n_x]
    w_ref = refs[n_x]
    e_refs = refs[n_x + 1:n_x + 1 + n_e]
    o_ref = refs[n_x + 1 + n_e]
    x = pre(*[r[...] for r in x_refs])
    part = jnp.dot(x.astype(BF16), w_ref[...].astype(BF16), preferred_element_type=F32)
    if nk == 1:
        o_ref[...] = post(part, *[e[...] for e in e_refs]).astype(o_ref.dtype)
        return
    acc_ref = refs[-1]
    k = pl.program_id(2)

    @pl.when(k == 0)
    def _():
        acc_ref[...] = part

    @pl.when(k > 0)
    def _():
        acc_ref[...] += part

    @pl.when(k == nk - 1)
    def _():
        o_ref[...] = post(acc_ref[...], *[e[...] for e in e_refs]).astype(o_ref.dtype)


def _mm_rows_kernel(*refs, n_x, n_e, pre, post):
    x_refs = refs[:n_x]
    w_ref = refs[n_x]
    e_refs = refs[n_x + 1:n_x + 1 + n_e]
    o_ref = refs[n_x + 1 + n_e]
    x_scr = refs[-1]

    @pl.when(pl.program_id(1) == 0)
    def _():
        x_scr[...] = pre(*[r[...] for r in x_refs]).astype(BF16)

    part = jnp.dot(x_scr[...], w_ref[...].astype(BF16), preferred_element_type=F32)
    o_ref[...] = post(part, *[e[...] for e in e_refs]).astype(o_ref.dtype)


def _identity(x):
    return x


def _matmul_rows_call(xs, w, extras=(), *, pre=_identity, post=_identity, out_dtype=F32,
                      bm=512, bn=512, name="matmul"):
    K, N = w.shape
    M = max(x.shape[0] for x in xs)
    bm = min(bm, M)
    bn = min(bn, N)
    assert M % bm == 0 and N % bn == 0
    x_specs = [pl.BlockSpec((bm, K), lambda m, n: (m, 0)) if x.shape[0] == M
               else pl.BlockSpec((x.shape[0], K), lambda m, n: (0, 0)) for x in xs]
    e_specs = [pl.BlockSpec((bm, bn), lambda m, n: (m, n)) if e.shape[0] == M
               else pl.BlockSpec((1, bn), lambda m, n: (0, n)) for e in extras]
    return pl.pallas_call(
        functools.partial(_mm_rows_kernel, n_x=len(xs), n_e=len(extras), pre=pre, post=post),
        grid=(M // bm, N // bn),
        in_specs=x_specs + [pl.BlockSpec((K, bn), lambda m, n: (0, n))] + e_specs,
        out_specs=pl.BlockSpec((bm, bn), lambda m, n: (m, n)),
        out_shape=jax.ShapeDtypeStruct((M, N), out_dtype),
        scratch_shapes=[pltpu.VMEM((bm, K), BF16)],
        compiler_params=pltpu.CompilerParams(
            dimension_semantics=("parallel", "arbitrary"), vmem_limit_bytes=VMEM_LIMIT),
        name=name,
    )(*xs, w, *extras)


def _matmul_call(xs, w, extras=(), *, pre=_identity, post=_identity, out_dtype=F32,
                 bm=512, bn=512, bk=None, name="matmul"):
    K, N = w.shape
    M = max(x.shape[0] for x in xs)
    bm = min(bm, M)
    bn = min(bn, N)
    bk = K if bk is None else min(bk, K)
    assert M % bm == 0 and N % bn == 0 and K % bk == 0
    nk = K // bk
    x_specs = [pl.BlockSpec((bm, bk), lambda n, m, k: (m, k)) if x.shape[0] == M
               else pl.BlockSpec((x.shape[0], bk), lambda n, m, k: (0, k)) for x in xs]
    e_specs = [pl.BlockSpec((bm, bn), lambda n, m, k: (m, n)) if e.shape[0] == M
               else pl.BlockSpec((1, bn), lambda n, m, k: (0, n)) for e in extras]
    return pl.pallas_call(
        functools.partial(_mm_kernel, n_x=len(xs), n_e=len(extras), pre=pre, post=post, nk=nk),
        grid=(N // bn, M // bm, nk),
        in_specs=x_specs + [pl.BlockSpec((bk, bn), lambda n, m, k: (k, n))] + e_specs,
        out_specs=pl.BlockSpec((bm, bn), lambda n, m, k: (m, n)),
        out_shape=jax.ShapeDtypeStruct((M, N), out_dtype),
        scratch_shapes=[pltpu.VMEM((bm, bn), F32)] if nk > 1 else [],
        compiler_params=pltpu.CompilerParams(
            dimension_semantics=("parallel", "parallel", "arbitrary"),
            vmem_limit_bytes=VMEM_LIMIT),
        name=name,
    )(*xs, w, *extras)


def _lora_kernel(*refs, n_x, n_e, pre, mid, post):
    x_refs = refs[:n_x]
    w1_ref, w2_ref = refs[n_x], refs[n_x + 1]
    e_refs = refs[n_x + 2:n_x + 2 + n_e]
    o_ref = refs[n_x + 2 + n_e]
    x = pre(*[r[...] for r in x_refs])
    z = mid(_bdot(x, w1_ref[...]))
    o_ref[...] = post(_bdot(z, w2_ref[...]), *[e[...] for e in e_refs]).astype(o_ref.dtype)


def _lora_call(xs, w1, w2, extras=(), *, pre=_identity, mid=_identity, post=_identity, bm=512, name="lora"):
    K, R = w1.shape
    N = w2.shape[1]
    M = max(x.shape[0] for x in xs)
    bm = min(bm, M)
    x_specs = [pl.BlockSpec((bm, K), lambda m: (m, 0)) if x.shape[0] == M
               else pl.BlockSpec((x.shape[0], K), lambda m: (0, 0)) for x in xs]
    e_specs = [pl.BlockSpec((bm, N), lambda m: (m, 0)) if e.shape[0] == M
               else pl.BlockSpec((1, N), lambda m: (0, 0)) for e in extras]
    return pl.pallas_call(
        functools.partial(_lora_kernel, n_x=len(xs), n_e=len(extras), pre=pre, mid=mid, post=post),
        grid=(M // bm,),
        in_specs=x_specs + [pl.BlockSpec((K, R), lambda m: (0, 0)), pl.BlockSpec((R, N), lambda m: (0, 0))]
        + e_specs,
        out_specs=pl.BlockSpec((bm, N), lambda m: (m, 0)),
        out_shape=jax.ShapeDtypeStruct((M, N), F32),
        compiler_params=pltpu.CompilerParams(dimension_semantics=("parallel",),
                                             vmem_limit_bytes=VMEM_LIMIT),
        name=name,
    )(*xs, w1, w2, *extras)


def _swa_kernel(slopes_ref, q_ref, kp_ref, vp_ref, kc_ref, vc_ref, o_ref, lse_ref, *, tq, n_cur, dil, group):
    tp = ATT_STEPS
    lane_q = lax.broadcasted_iota(jnp.int32, (tq, LANES), 1) < HD_B
    iq_p = lax.broadcasted_iota(jnp.int32, (tq, tp), 0)
    jk_p = lax.broadcasted_iota(jnp.int32, (tq, tp), 1)
    steps_p = iq_p + tp - jk_p
    valid_p = steps_p <= ATT_STEPS
    iq_c = lax.broadcasted_iota(jnp.int32, (tq, tq), 0)
    jk_c = lax.broadcasted_iota(jnp.int32, (tq, tq), 1)
    steps_c = iq_c - jk_c
    valid_c = (steps_c >= 0) & (jk_c < n_cur)
    scale = HD_B ** -0.5

    def head(hp, h):
        sl = slice(hp * LANES, (hp + 1) * LANES)
        slope = slopes_ref[group * H_B + 2 * hp + h]
        q = q_ref[0, :, sl]
        qh = jnp.where(lane_q, q, 0.0) if h == 0 else jnp.where(lane_q, 0.0, q)
        sp = _bdot_nt(qh, kp_ref[0, :, sl])
        sc = _bdot_nt(qh, kc_ref[0, :, sl])
        yield
        sp = sp * scale - slope * (steps_p * dil).astype(F32)
        sc = sc * scale - slope * (steps_c * dil).astype(F32)
        sp = jnp.where(valid_p, sp, -jnp.inf)
        sc = jnp.where(valid_c, sc, -jnp.inf)
        m = jnp.maximum(jnp.max(sp, axis=-1, keepdims=True), jnp.max(sc, axis=-1, keepdims=True))
        pp = jnp.exp(sp - m)
        pc = jnp.exp(sc - m)
        l = jnp.sum(pp, axis=-1, keepdims=True) + jnp.sum(pc, axis=-1, keepdims=True)
        acc = _bdot(pp, vp_ref[0, :, sl]) + _bdot(pc, vc_ref[0, :, sl])
        yield
        return acc / l, jnp.broadcast_to(m + jnp.log(l), (tq, LANES))

    n_hp = H_B // 2
    res = _lockstep([head(hp, h) for hp in range(n_hp) for h in range(2)])
    for hp in range(n_hp):
        sl = slice(hp * LANES, (hp + 1) * LANES)
        (o0, l0), (o1, l1) = res[2 * hp], res[2 * hp + 1]
        o_ref[0, :, sl] = jnp.where(lane_q, o0, o1)
        lse_ref[0, :, sl] = jnp.where(lane_q, l0, l1)


def _swa_call(slopes, new, cache, *, n_streams, tq, n_cur, dil, group, name):
    B = new.shape[0]
    hw = H_B * HD_B
    in_specs = [
        pl.BlockSpec(memory_space=pltpu.SMEM),
        pl.BlockSpec((1, tq, hw), lambda b, c: (b, 0, 3 * c)),
        pl.BlockSpec((1, ATT_STEPS, hw), lambda b, c: (b, 0, 2 * c)),
        pl.BlockSpec((1, ATT_STEPS, hw), lambda b, c: (b, 0, 2 * c + 1)),
        pl.BlockSpec((1, tq, hw), lambda b, c: (b, 0, 3 * c + 1)),
        pl.BlockSpec((1, tq, hw), lambda b, c: (b, 0, 3 * c + 2)),
    ]
    out_spec = pl.BlockSpec((1, tq, hw), lambda b, c: (b, 0, c))
    out_sds = jax.ShapeDtypeStruct((B, tq, n_streams * hw), F32)
    return pl.pallas_call(
        functools.partial(_swa_kernel, tq=tq, n_cur=n_cur, dil=dil, group=group),
        grid=(B, n_streams),
        in_specs=in_specs,
        out_specs=[out_spec, out_spec],
        out_shape=[out_sds, out_sds],
        compiler_params=pltpu.CompilerParams(dimension_semantics=("parallel", "parallel")),
        name=name,
    )(slopes, new, cache, cache, new, new)


ATT_BLOCK = ATT_STEPS * max(d for _, d in GROUPS)
ATT_LANES = 4


def _attn_prompt_kernel(slopes_ref, *refs):
    ng = N_GROUPS
    q_refs, kc_refs, vc_refs = refs[0:ng], refs[ng:2 * ng], refs[2 * ng:3 * ng]
    kp_refs, vp_refs = refs[3 * ng:4 * ng], refs[4 * ng:5 * ng]
    o_ref = refs[5 * ng]
    m_scr, l_scr, acc_scr = refs[5 * ng + 1:]
    hp = pl.program_id(1)
    j = pl.program_id(2)
    S = ATT_STEPS
    lane_lo = lax.broadcasted_iota(jnp.int32, (S, LANES), 1) < HD_B
    iq = lax.broadcasted_iota(jnp.int32, (S, S), 0)
    jk = lax.broadcasted_iota(jnp.int32, (S, S), 1)
    steps_p = iq + S - jk
    steps_c = iq - jk
    mask_p = jnp.where(steps_p <= S, 0.0, -jnp.inf)
    mask_c = jnp.where(steps_c >= 0, 0.0, -jnp.inf)
    scale = HD_B ** -0.5
    no_prev = jnp.where(j > 0, 0.0, -jnp.inf)

    for g, (_, dil) in enumerate(GROUPS):
        biases = []
        for h in range(2):
            slope = slopes_ref[g * H_B + 2 * hp + h]
            biases.append((mask_p - slope * (steps_p * dil).astype(F32),
                           mask_c - slope * (steps_c * dil).astype(F32)))
        q_ref, kc_ref, vc_ref, kp_ref, vp_ref = q_refs[g], kc_refs[g], vc_refs[g], kp_refs[g], vp_refs[g]

        def attend(cur_start, prev_ref, prev_start, prev_bias, dil=dil, g=g, biases=biases, q_ref=q_ref,
                   kc_ref=kc_ref, vc_ref=vc_ref):
            cur = pl.ds(cur_start, S, stride=dil) if dil > 1 else pl.ds(cur_start, S)
            prv = pl.ds(prev_start, S, stride=dil) if dil > 1 else pl.ds(prev_start, S)
            q, kc, vc = q_ref[0, cur, :], kc_ref[0, cur, :], vc_ref[0, cur, :]
            kp, vp = prev_ref[0][0, prv, :], prev_ref[1][0, prv, :]
            qs = [jnp.where(lane_lo, q, 0.0), jnp.where(lane_lo, 0.0, q)]
            scores = [(_bdot_nt(qh, kp), _bdot_nt(qh, kc)) for qh in qs]
            yield
            ms, ls, probs = [], [], []
            for h in range(2):
                sp = scores[h][0] * scale + biases[h][0]
                if prev_bias is not None:
                    sp = sp + prev_bias
                sc = scores[h][1] * scale + biases[h][1]
                m = jnp.max(jnp.maximum(sp, sc), axis=-1, keepdims=True)
                pp = jnp.exp(sp - m)
                pc = jnp.exp(sc - m)
                ls.append(jnp.sum(pp + pc, axis=-1, keepdims=True))
                probs.append((pp, pc))
                ms.append(m)
            accs = [_bdot(pp, vp) + _bdot(pc, vc) for pp, pc in probs]
            yield
            m = jnp.where(lane_lo, ms[0], ms[1])
            l = jnp.where(lane_lo, ls[0], ls[1])
            acc = jnp.where(lane_lo, accs[0], accs[1])
            if g > 0:
                m_old = m_scr[cur, :]
                m_new = jnp.maximum(m_old, m)
                w_old, w_new = jnp.exp(m_old - m_new), jnp.exp(m - m_new)
                l = l_scr[cur, :] * w_old + l * w_new
                acc = acc_scr[cur, :] * w_old + acc * w_new
                m = m_new
            m_scr[cur, :] = m
            l_scr[cur, :] = l
            acc_scr[cur, :] = acc

        sd = S * dil
        n_sub = ATT_BLOCK // sd
        run = lambda items, attend=attend: _lockstep([attend(*it) for it in items])
        first = lambda c, kp_ref=kp_ref, vp_ref=vp_ref: (c, (kp_ref, vp_ref), c, no_prev)
        later = lambda c, i, sd=sd, kc_ref=kc_ref, vc_ref=vc_ref: (
            i * sd + c, (kc_ref, vc_ref), (i - 1) * sd + c, None)

        def loop(lo, hi, fn):
            def body(idx, carry):
                fn(idx)
                return carry
            lax.fori_loop(lo, hi, body, 0)

        W = ATT_LANES
        if dil == 1:
            hb = n_sub // W
            run([first(0)] + [later(0, k * hb) for k in range(1, W)])
            loop(1, hb, lambda i, hb=hb: run([later(0, i + k * hb) for k in range(W)]))
        else:
            cs = dil // W
            def streams(c, cs=cs, n_sub=n_sub):
                run([first(c + k * cs) for k in range(W)])
                if n_sub > 1:
                    loop(1, n_sub, lambda i: run([later(c + k * cs, i) for k in range(W)]))
            if cs == 1:
                streams(0)
            else:
                loop(0, cs, streams)

    o_ref[0] = (acc_scr[...] / l_scr[...]).astype(o_ref.dtype)


def _attn_prompt_call(qkv, slopes):
    B, T, W = qkv.shape
    hw = H_B * HD_B
    n_hp = H_B // 2
    TB = ATT_BLOCK
    assert T % TB == 0
    col = lambda g, which, hp: (g * 3 * hw + which * hw) // LANES + hp
    cur = lambda g, which: pl.BlockSpec((1, TB, LANES), lambda b, hp, j: (b, j, col(g, which, hp)))
    def prev(g, which):
        rows = ATT_STEPS * GROUPS[g][1]
        per = TB // rows
        return pl.BlockSpec((1, rows, LANES),
                            lambda b, hp, j: (b, jnp.maximum(j * per - 1, 0), col(g, which, hp)))
    gs = range(N_GROUPS)
    in_specs = ([pl.BlockSpec(memory_space=pltpu.SMEM)]
                + [cur(g, 0) for g in gs] + [cur(g, 1) for g in gs] + [cur(g, 2) for g in gs]
                + [prev(g, 1) for g in gs] + [prev(g, 2) for g in gs])
    return pl.pallas_call(
        _attn_prompt_kernel,
        grid=(B, n_hp, T // TB),
        in_specs=in_specs,
        out_specs=pl.BlockSpec((1, TB, LANES), lambda b, hp, j: (b, j, hp)),
        out_shape=jax.ShapeDtypeStruct((B, T, hw), BF16),
        scratch_shapes=[pltpu.VMEM((TB, LANES), F32)] * 3,
        compiler_params=pltpu.CompilerParams(
            dimension_semantics=("parallel", "parallel", "arbitrary"), vmem_limit_bytes=VMEM_LIMIT),
        name="attn_prompt",
    )(slopes, *([qkv] * (5 * N_GROUPS)))


def _merge_groups(o0, l0, o1, l1, o2, l2):
    m = jnp.maximum(jnp.maximum(l0, l1), l2)
    w0, w1, w2 = jnp.exp(l0 - m), jnp.exp(l1 - m), jnp.exp(l2 - m)
    return (w0 * o0 + w1 * o1 + w2 * o2) / (w0 + w1 + w2)


def _alibi_slopes():
    n = N_GROUPS * H_B
    return 2.0 ** (-8.0 * jnp.arange(1, n + 1, dtype=F32) / n)


def _pair_states(s):
    B, H = s.shape[:2]
    s = s.reshape(B, H // 2, 2, HEAD_A, HEAD_A)
    z = jnp.zeros_like(s[:, :, 0])
    top = jnp.concatenate([s[:, :, 0], z], axis=-1)
    bot = jnp.concatenate([z, s[:, :, 1]], axis=-1)
    return jnp.concatenate([top, bot], axis=-2)


def _unpair_states(s2):
    B, P = s2.shape[:2]
    return jnp.stack([s2[:, :, :HEAD_A, :HEAD_A], s2[:, :, HEAD_A:, HEAD_A:]],
                     axis=2).reshape(B, 2 * P, HEAD_A, HEAD_A)


def _norm_mix_kernel(x_ref, g_ref, mu_ref, shift_ref, *refs, tiles_per_seq):
    out_refs, hlast_ref, carry_scr = refs[:-2], refs[-2], refs[-1]
    @pl.when(pl.program_id(0) == 0)
    def _():
        carry_scr[...] = jnp.zeros_like(carry_scr)

    x = x_ref[0]
    h = x * lax.rsqrt(jnp.mean(x * x, axis=-1, keepdims=True) + RMS_EPS) * g_ref[...]
    rows = h.shape[0]
    first_tile = pl.program_id(0) % tiles_per_seq == 0
    row0 = jnp.where(first_tile, shift_ref[0], carry_scr[...])
    prev = pltpu.roll(h, 1, 0) if rows > 1 else h
    prev = jnp.where(lax.broadcasted_iota(jnp.int32, h.shape, 0) == 0, row0, prev)
    carry_scr[...] = h[rows - 1:rows, :]
    hlast_ref[0] = h[rows - 1:rows, :]
    diff = prev - h
    for i, o_ref in enumerate(out_refs):
        o_ref[0] = (h + diff * mu_ref[i:i + 1, :]).astype(o_ref.dtype)


def _norm_mix_call(x, g, mu, shift0, bt=512):
    B, T, D = x.shape
    bt = min(bt, T)
    assert T % bt == 0
    n_mix = mu.shape[0]
    tiles = T // bt
    seq = pl.BlockSpec((1, bt, D), lambda m: (m // tiles, m % tiles, 0))
    per_seq = pl.BlockSpec((1, 1, D), lambda m: (m // tiles, 0, 0))
    outs = pl.pallas_call(
        functools.partial(_norm_mix_kernel, tiles_per_seq=tiles),
        grid=(B * tiles,),
        in_specs=[seq, pl.BlockSpec((1, D), lambda m: (0, 0)), pl.BlockSpec((n_mix, D), lambda m: (0, 0)),
                  per_seq],
        out_specs=[seq] * n_mix + [per_seq],
        out_shape=[jax.ShapeDtypeStruct((B, T, D), BF16)] * n_mix + [jax.ShapeDtypeStruct((B, 1, D), F32)],
        scratch_shapes=[pltpu.VMEM((1, D), F32)],
        compiler_params=pltpu.CompilerParams(dimension_semantics=("arbitrary",),
                                             vmem_limit_bytes=VMEM_LIMIT),
        name="rwkv_norm_mix",
    )(x, g.reshape(1, D), mu, shift0.reshape(B, 1, D))
    return [o.reshape(B * T, D) for o in outs[:n_mix]], outs[n_mix].reshape(B, D)


def _rwkv_block(x, shift0, wkv0, norm_g, p):
    (mu, w0, w1, w2, a0, a1, a2, g1, g2, k_k, k_a, r_k, lnx_w, lnx_b, w_r, w_k, w_v, w_o) = p
    B, T, D = x.shape
    M = B * T
    x2 = x.reshape(M, D)
    (xr, xw, xk, xv, xa, xg), h_last = _norm_mix_call(x, norm_g, mu, shift0)
    big = dict(bm=1024, bn=1024)
    r = _matmul_call([xr], w_r, name="rwkv_r", **big)
    k = _matmul_call([xk], w_k, name="rwkv_k", **big)
    v = _matmul_call([xv], w_v, name="rwkv_v", **big)
    wlog = _lora_call([xw], w1, w2, [w0.reshape(1, D)], mid=jnp.tanh,
                      post=lambda z, b: -jax.nn.softplus(-(b + z)) - 0.5, name="rwkv_w")
    a = _lora_call([xa], a1, a2, [a0.reshape(1, D)],
                   post=lambda z, b: jax.nn.sigmoid(b + z), name="rwkv_a")
    g = _lora_call([xg], g1, g2, mid=jax.nn.sigmoid, name="rwkv_g")

    L = WKV_CHUNK
    Tp = -(-T // L) * L
    def seq(z, fill=0.0):
        z = z.reshape(B, T, D)
        if Tp != T:
            z = jnp.pad(z, ((0, 0), (0, Tp - T), (0, 0)), constant_values=fill)
        return z
    y, s2 = _wkv_call(seq(r), seq(k), seq(v), seq(wlog, -jnp.inf), seq(a), seq(g),
                      k_k, k_a, r_k.reshape(D), lnx_w, lnx_b, _pair_states(wkv0), pairs_per_step=D // LANES)
    y2 = y[:, :T].reshape(M, D)
    out = _matmul_call([y2], w_o, [x2], post=lambda acc, res: res + acc, name="rwkv_o", **big)
    return out.reshape(B, T, D), _unpair_states(s2), h_last


def _ffn_block(x2, norm_g, w_up, w_down):
    hn = _rmsnorm_call(x2, norm_g, BF16)
    mid = _matmul_call([hn], w_up, post=lambda acc: jnp.square(jnp.maximum(acc, 0.0)),
                       out_dtype=BF16, bm=1024, bn=1024, name="ffn_up")
    return _matmul_call([mid], w_down, [x2], post=lambda acc, res: res + acc,
                        bm=1024, bn=1024, bk=2048, name="ffn_down")


def _attn_sample(qkv, caches, slopes):
    B, S, W = qkv.shape
    gw = 3 * H_B * HD_B
    hw = H_B * HD_B
    TQ = 8
    res = []
    for g, (window, dil) in enumerate(GROUPS):
        dd = min(dil, S)
        nq = S // dd
        new = qkv[:, :, g * gw:(g + 1) * gw].reshape(B, nq, dd * gw)
        new = jnp.pad(new, ((0, 0), (0, TQ - nq), (0, 0)))
        cache = caches[g].reshape(B, window // dil, dil * 2 * hw)
        o, lse = _swa_call(slopes, new, cache, n_streams=dd, tq=TQ, n_cur=nq, dil=dil, group=g,
                           name=f"attn_sample_g{g}")
        res += [o[:, :nq].reshape(B * S, hw), lse[:, :nq].reshape(B * S, hw)]
    return res


def _attn_block(x, norm_g, w_qkv, w_o, slopes, caches=None):
    B, T, D = x.shape
    M = B * T
    x2 = x.reshape(M, D)
    hn = _rmsnorm_call(x2, norm_g, BF16)
    qkv = _matmul_call([hn], w_qkv, bm=1024, bn=1024, name="attn_qkv").reshape(B, T, -1)
    hw = H_B * HD_B
    def new_kv(g, t0):
        return qkv[:, t0:, (3 * g + 1) * hw:(3 * g + 3) * hw].reshape(B, T - t0, 2, H_B, HD_B)
    add_residual = lambda acc, res: res + acc
    if caches is None:
        merged = _attn_prompt_call(qkv, slopes).reshape(M, hw)
        bufs = [new_kv(g, T - min(w, T)) for g, (w, _) in enumerate(GROUPS)]
        out = _matmul_call([merged], w_o, [x2], post=add_residual, bm=1024, name="attn_o")
    else:
        parts = _attn_sample(qkv, caches, slopes)
        bufs = [jnp.concatenate([caches[g][:, T:], new_kv(g, 0)], axis=1) for g in range(N_GROUPS)]
        out = _matmul_rows_call(parts, w_o, [x2], pre=_merge_groups, post=add_residual, name="attn_o_merge")
    return out.reshape(B, T, D), bufs


def kernel(x_prompt, x_sample, state_wkv, state_shift, cache_kv_g1, cache_kv_g2, cache_kv_g3,
           norm_mix, norm_ffn, norm_final,
           rwkv_mu, rwkv_w0, rwkv_w1, rwkv_w2, rwkv_a0, rwkv_a1, rwkv_a2, rwkv_g1, rwkv_g2,
           rwkv_k_k, rwkv_k_a, rwkv_r_k, rwkv_lnx_w, rwkv_lnx_b, rwkv_w_r, rwkv_w_k, rwkv_w_v, rwkv_w_o,
           attn_w_qkv, attn_w_o, ffn_w_up, ffn_w_down):
    rwkv_params = (rwkv_mu, rwkv_w0, rwkv_w1, rwkv_w2, rwkv_a0, rwkv_a1, rwkv_a2, rwkv_g1, rwkv_g2,
                   rwkv_k_k, rwkv_k_a, rwkv_r_k, rwkv_lnx_w, rwkv_lnx_b,
                   rwkv_w_r, rwkv_w_k, rwkv_w_v, rwkv_w_o)
    slopes = _alibi_slopes()
    Bp, Tp, D = x_prompt.shape
    Bs, Ts, _ = x_sample.shape

    def ffn(x, i):
        B, T, _ = x.shape
        return _ffn_block(x.reshape(B * T, D), norm_ffn[i], ffn_w_up[i], ffn_w_down[i]).reshape(B, T, D)

    xp, wkv_p, shift_p = _rwkv_block(x_prompt, jnp.zeros((Bp, D), F32),
                                     jnp.zeros((Bp, H_A, HEAD_A, HEAD_A), F32), norm_mix[0], rwkv_params)
    xs, wkv_s, shift_s = _rwkv_block(x_sample, state_shift, state_wkv, norm_mix[0], rwkv_params)
    xp, xs = ffn(xp, 0), ffn(xs, 0)
    xp, (kv1_p, kv2_p, kv3_p) = _attn_block(xp, norm_mix[1], attn_w_qkv, attn_w_o, slopes)
    xs, (kv1_s, kv2_s, kv3_s) = _attn_block(xs, norm_mix[1], attn_w_qkv, attn_w_o, slopes,
                                            (cache_kv_g1, cache_kv_g2, cache_kv_g3))
    xp, xs = ffn(xp, 1), ffn(xs, 1)
    y_prompt = _rmsnorm_call(xp.reshape(Bp * Tp, D), norm_final).reshape(Bp, Tp, D)
    y_sample = _rmsnorm_call(xs.reshape(Bs * Ts, D), norm_final).reshape(Bs, Ts, D)
    return (y_prompt, y_sample, wkv_p, shift_p, kv1_p, kv2_p, kv3_p,
            wkv_s, shift_s, kv1_s, kv2_s, kv3_s)
```

```python
import functools

import jax
import jax.numpy as jnp
from jax import lax
from jax.experimental import pallas as pl
from jax.experimental.pallas import tpu as pltpu

F32 = jnp.float32
BF16 = jnp.bfloat16

D_MODEL = 2048
HEAD_A = 64
H_A = D_MODEL // HEAD_A
GN_EPS = 64e-5
GROUPS = ((128, 1), (512, 4), (2048, 16))
N_GROUPS = len(GROUPS)
H_B = 16
HD_B = 64
ATT_STEPS = 128
RMS_EPS = 1e-6

LANES = 128
VMEM_LIMIT = 56 * 1024 * 1024


def _bdot(a, b):
    return jnp.dot(a.astype(BF16), b.astype(BF16), preferred_element_type=F32)


def _bdot_nt(a, b):
    return lax.dot_general(a.astype(BF16), b.astype(BF16), (((1,), (1,)), ((), ())),
                           preferred_element_type=F32)


def _bdot_tn(a, b):
    return lax.dot_general(a.astype(BF16), b.astype(BF16), (((0,), (0,)), ((), ())),
                           preferred_element_type=F32)


def _exact_dot(m, x):
    hi = x.astype(BF16)
    r1 = x - hi.astype(F32)
    mid = r1.astype(BF16)
    lo = (r1 - mid.astype(F32)).astype(BF16)
    mb = m.astype(BF16)
    acc = jnp.dot(mb, hi, preferred_element_type=F32)
    acc = acc + jnp.dot(mb, mid, preferred_element_type=F32)
    return acc + jnp.dot(mb, lo, preferred_element_type=F32)


def _lockstep(chains):
    results = [None] * len(chains)
    live = list(range(len(chains)))
    while live:
        for idx in list(live):
            try:
                next(chains[idx])
            except StopIteration as done:
                results[idx] = done.value
                live.remove(idx)
    return results


def _head_sum(x, lane_lo):
    s0 = jnp.sum(jnp.where(lane_lo, x, 0.0), axis=-1, keepdims=True)
    s1 = jnp.sum(jnp.where(lane_lo, 0.0, x), axis=-1, keepdims=True)
    return jnp.where(lane_lo, s0, s1)


WKV_CHUNK = HEAD_A
(_MK_EYE, _MK_LVL0, _MK_BD, _MK_NMASK, _MK_CAUSAL, _MK_TRI, _MK_LEVELS) = range(7)
_N_LEVELS = 5
_N_MASKS = _MK_LEVELS + _N_LEVELS


def _wkv_fill_masks(mk_ref):
    L = WKV_CHUNK
    r2 = lax.broadcasted_iota(jnp.int32, (LANES, LANES), 0)
    c2 = lax.broadcasted_iota(jnp.int32, (LANES, LANES), 1)
    f = lambda m: m.astype(F32)
    same_block = (r2 < L) == (c2 < L)
    strict = (c2 % L) < (r2 % L)
    incl = (c2 % L) <= (r2 % L)
    mk_ref[_MK_EYE] = f(r2 == c2)
    mk_ref[_MK_LVL0] = f((r2 // 2 == c2 // 2) & (c2 < r2))
    mk_ref[_MK_BD] = f(same_block)
    mk_ref[_MK_NMASK] = f(strict & same_block)
    mk_ref[_MK_CAUSAL] = f(strict | (incl & (r2 >= L)))
    mk_ref[_MK_TRI] = f(incl & (r2 < L) & (c2 < L))
    size = 2
    for lvl in range(_N_LEVELS):
        mk_ref[_MK_LEVELS + lvl] = f((r2 // (2 * size) == c2 // (2 * size))
                                     & ((r2 // size) % 2 == 1) & ((c2 // size) % 2 == 0) & same_block)
        size *= 2


def _wkv_pair_chunk(r, kraw, v, wl, a, g, k_k, k_a, r_k, lnx_w, lnx_b, s2, mk_ref):
    L = WKV_CHUNK
    lane_lo = lax.broadcasted_iota(jnp.int32, (L, LANES), 1) < HEAD_A
    lo2 = lax.broadcasted_iota(jnp.int32, (2 * L, LANES), 1) < HEAD_A

    logd = -jnp.exp(wl)
    kkraw = kraw * k_k
    kk = kkraw / jnp.maximum(jnp.sqrt(_head_sum(kkraw * kkraw, lane_lo)), 1e-12)
    k2 = kraw * (1.0 + (a - 1.0) * k_a)

    cum = _exact_dot(mk_ref[_MK_TRI][:L, :L], logd)
    w_incl = jnp.exp(cum)
    w_excl = jnp.exp(cum - logd)
    w_inv = jnp.exp(-cum)
    w_last = w_incl[L - 1:L, :]

    at = -kk * w_excl
    bt = (kk * a) * w_inv
    kt = k2 * w_inv
    rt = r * w_incl

    ar = jnp.concatenate([at, rt], axis=0)
    bk = jnp.concatenate([bt, kt], axis=0)
    kb = jnp.concatenate([kt, bt], axis=0)
    m0 = _bdot_nt(jnp.where(lo2, ar, 0.0), bk)
    m1 = _bdot_nt(jnp.where(lo2, 0.0, ar), kb)
    aprp = _bdot_nt(ar, s2)
    yield

    a0 = m0[:L]
    a1 = m1[:L]
    n2 = jnp.concatenate([a0, a1], axis=0) * mk_ref[_MK_NMASK]

    t2 = mk_ref[_MK_EYE] + n2 * mk_ref[_MK_LVL0]
    for lvl in range(_N_LEVELS):
        tc = _bdot(t2, n2 * mk_ref[_MK_LEVELS + lvl])
        yield
        t2 = t2 + _bdot(tc, t2)
        yield
    tcat = t2[:L] + t2[L:]

    causal = mk_ref[_MK_CAUSAL]
    strict, incl = causal[:L], causal[L:]
    v_lo = jnp.where(lane_lo, v, 0.0)
    v_hi = jnp.where(lane_lo, 0.0, v)
    ak_cat = jnp.where(lane_lo, a1, a0) * strict
    rhs = aprp[:L] + _bdot(ak_cat, jnp.concatenate([v_hi, v_lo], axis=0))
    yield
    u = _bdot(tcat, jnp.concatenate([jnp.where(lane_lo, rhs, 0.0), jnp.where(lane_lo, 0.0, rhs)], axis=0))
    yield
    u_lo = jnp.where(lane_lo, u, 0.0)
    u_hi = jnp.where(lane_lo, 0.0, u)

    y = (aprp[L:] + _bdot(m0[L:] * incl, jnp.concatenate([u_lo, v_lo], axis=0))
         + _bdot(m1[L:] * incl, jnp.concatenate([v_hi, u_hi], axis=0)))

    delta = _bdot_tn(jnp.concatenate([u, v], axis=0), bk)
    yield
    s_new = (s2 + delta * mk_ref[_MK_BD]) * w_last

    mean = _head_sum(y, lane_lo) * (1.0 / HEAD_A)
    yc = y - mean
    var = _head_sum(yc * yc, lane_lo) * (1.0 / HEAD_A)
    yn = yc * lax.rsqrt(var + GN_EPS) * lnx_w + lnx_b
    bonus = _head_sum(r * k2 * r_k, lane_lo)
    return (yn + bonus * v) * g, s_new


def _wkv_kernel(r_ref, k_ref, v_ref, wl_ref, a_ref, g_ref, kk_ref, ka_ref, rk_ref, lw_ref, lb_ref,
                s0_ref, y_ref, sout_ref, s_scr, mk_scr, *, n_pairs):
    b, pb, c = pl.program_id(0), pl.program_id(1), pl.program_id(2)

    @pl.when((b == 0) & (pb == 0) & (c == 0))
    def _():
        _wkv_fill_masks(mk_scr)

    @pl.when(c == 0)
    def _():
        s_scr[...] = s0_ref[0]

    lanes = [slice(p * LANES, (p + 1) * LANES) for p in range(n_pairs)]
    chains = [_wkv_pair_chunk(
        r_ref[0, :, sl], k_ref[0, :, sl], v_ref[0, :, sl], wl_ref[0, :, sl], a_ref[0, :, sl],
        g_ref[0, :, sl], kk_ref[:, sl], ka_ref[:, sl], rk_ref[:, sl], lw_ref[:, sl], lb_ref[:, sl],
        s_scr[p], mk_scr) for p, sl in enumerate(lanes)]
    for (y, s_new), p, sl in zip(_lockstep(chains), range(n_pairs), lanes):
        y_ref[0, :, sl] = y.astype(y_ref.dtype)
        s_scr[p] = s_new

    @pl.when(c == pl.num_programs(2) - 1)
    def _():
        sout_ref[0] = s_scr[...]


def _wkv_call(r, k, v, wlog, a, g, k_k, k_a, r_k, lnx_w, lnx_b, s2_0, *, pairs_per_step):
    B, T, D = r.shape
    L, P = WKV_CHUNK, pairs_per_step
    assert T % L == 0 and 2 * L == LANES and (D // LANES) % P == 0
    seq = pl.BlockSpec((1, L, P * LANES), lambda b, p, c: (b, c, p))
    par = pl.BlockSpec((1, P * LANES), lambda b, p, c: (0, p))
    st = pl.BlockSpec((1, P, LANES, LANES), lambda b, p, c: (b, p, 0, 0))
    row = lambda x: x.reshape(1, D)
    return pl.pallas_call(
        functools.partial(_wkv_kernel, n_pairs=P),
        grid=(B, D // LANES // P, T // L),
        in_specs=[seq] * 6 + [par] * 5 + [st],
        out_specs=[seq, st],
        out_shape=[jax.ShapeDtypeStruct((B, T, D), BF16),
                   jax.ShapeDtypeStruct((B, D // LANES, LANES, LANES), F32)],
        scratch_shapes=[pltpu.VMEM((P, LANES, LANES), F32), pltpu.VMEM((_N_MASKS, LANES, LANES), F32)],
        compiler_params=pltpu.CompilerParams(
            dimension_semantics=("arbitrary", "arbitrary", "arbitrary")),
        name="wkv7_chunk_scan",
    )(r, k, v, wlog, a, g, row(k_k), row(k_a), row(r_k), row(lnx_w), row(lnx_b), s2_0)


def _rmsnorm_kernel(x_ref, g_ref, o_ref):
    x = x_ref[...]
    y = x * lax.rsqrt(jnp.mean(x * x, axis=-1, keepdims=True) + RMS_EPS)
    o_ref[...] = (y * g_ref[...]).astype(o_ref.dtype)


def _rmsnorm_call(x, g, out_dtype=F32):
    M, D = x.shape
    bm = min(M, 512)
    return pl.pallas_call(
        _rmsnorm_kernel,
        grid=(M // bm,),
        in_specs=[pl.BlockSpec((bm, D), lambda m: (m, 0)), pl.BlockSpec((1, D), lambda m: (0, 0))],
        out_specs=pl.BlockSpec((bm, D), lambda m: (m, 0)),
        out_shape=jax.ShapeDtypeStruct((M, D), out_dtype),
        compiler_params=pltpu.CompilerParams(dimension_semantics=("parallel",)),
        name="rmsnorm",
    )(x, g.reshape(1, D))


def _mm_kernel(*refs, n_x, n_e, pre, post, nk):
    x_refs = refs[:n_x]
    w_ref = refs[n_x]
    e_refs = refs[n_x + 1:n_x + 1 + n_e]
    o_ref = refs[n_x + 1 + n_e]
    x = pre(*[r[...] for r in x_refs])
    part = jnp.dot(x.astype(BF16), w_ref[...].astype(BF16), preferred_element_type=F32)
    if nk == 1:
        o_ref[...] = post(part, *[e[...] for e in e_refs]).astype(o_ref.dtype)
        return
    acc_ref = refs[-1]
    k = pl.program_id(2)

    @pl.when(k == 0)
    def _():
        acc_ref[...] = part

    @pl.when(k > 0)
    def _():
        acc_ref[...] += part

    @pl.when(k == nk - 1)
    def _():
        o_ref[...] = post(acc_ref[...], *[e[...] for e in e_refs]).astype(o_ref.dtype)


def _mm_rows_kernel(*refs, n_x, n_e, pre, post):
    x_refs = refs[:n_x]
    w_ref = refs[n_x]
    e_refs = refs[n_x + 1:n_x + 1 + n_e]
    o_ref = refs[n_x + 1 + n_e]
    x_scr = refs[-1]

    @pl.when(pl.program_id(1) == 0)
    def _():
        x_scr[...] = pre(*[r[...] for r in x_refs]).astype(BF16)

    part = jnp.dot(x_scr[...], w_ref[...].astype(BF16), preferred_element_type=F32)
    o_ref[...] = post(part, *[e[...] for e in e_refs]).astype(o_ref.dtype)


def _identity(x):
    return x


def _matmul_rows_call(xs, w, extras=(), *, pre=_identity, post=_identity, out_dtype=F32,
                      bm=512, bn=512, name="matmul"):
    K, N = w.shape
    M = max(x.shape[0] for x in xs)
    bm = min(bm, M)
    bn = min(bn, N)
    assert M % bm == 0 and N % bn == 0
    x_specs = [pl.BlockSpec((bm, K), lambda m, n: (m, 0)) if x.shape[0] == M
               else pl.BlockSpec((x.shape[0], K), lambda m, n: (0, 0)) for x in xs]
    e_specs = [pl.BlockSpec((bm, bn), lambda m, n: (m, n)) if e.shape[0] == M
               else pl.BlockSpec((1, bn), lambda m, n: (0, n)) for e in extras]
    return pl.pallas_call(
        functools.partial(_mm_rows_kernel, n_x=len(xs), n_e=len(extras), pre=pre, post=post),
        grid=(M // bm, N // bn),
        in_specs=x_specs + [pl.BlockSpec((K, bn), lambda m, n: (0, n))] + e_specs,
        out_specs=pl.BlockSpec((bm, bn), lambda m, n: (m, n)),
        out_shape=jax.ShapeDtypeStruct((M, N), out_dtype),
        scratch_shapes=[pltpu.VMEM((bm, K), BF16)],
        compiler_params=pltpu.CompilerParams(
            dimension_semantics=("parallel", "arbitrary"), vmem_limit_bytes=VMEM_LIMIT),
        name=name,
    )(*xs, w, *extras)


def _matmul_call(xs, w, extras=(), *, pre=_identity, post=_identity, out_dtype=F32,
                 bm=512, bn=512, bk=None, layer=None, name="matmul"):
    K, N = w.shape[-2:]
    M = max(x.shape[0] for x in xs)
    bm = min(bm, M)
    bn = min(bn, N)
    bk = K if bk is None else min(bk, K)
    assert M % bm == 0 and N % bn == 0 and K % bk == 0
    nk = K // bk
    x_specs = [pl.BlockSpec((bm, bk), lambda n, m, k: (m, k)) if x.shape[0] == M
               else pl.BlockSpec((x.shape[0], bk), lambda n, m, k: (0, k)) for x in xs]
    e_specs = [pl.BlockSpec((bm, bn), lambda n, m, k: (m, n)) if e.shape[0] == M
               else pl.BlockSpec((1, bn), lambda n, m, k: (0, n)) for e in extras]
    if layer is None:
        w_spec = pl.BlockSpec((bk, bn), lambda n, m, k: (k, n))
    else:
        w_spec = pl.BlockSpec((None, bk, bn), lambda n, m, k: (layer, k, n))
    return pl.pallas_call(
        functools.partial(_mm_kernel, n_x=len(xs), n_e=len(extras), pre=pre, post=post, nk=nk),
        grid=(N // bn, M // bm, nk),
        in_specs=x_specs + [w_spec] + e_specs,
        out_specs=pl.BlockSpec((bm, bn), lambda n, m, k: (m, n)),
        out_shape=jax.ShapeDtypeStruct((M, N), out_dtype),
        scratch_shapes=[pltpu.VMEM((bm, bn), F32)] if nk > 1 else [],
        compiler_params=pltpu.CompilerParams(
            dimension_semantics=("parallel", "parallel", "arbitrary"),
            vmem_limit_bytes=VMEM_LIMIT),
        name=name,
    )(*xs, w, *extras)


def _lora_kernel(*refs, n_x, n_e, pre, mid, post):
    x_refs = refs[:n_x]
    w1_ref, w2_ref = refs[n_x], refs[n_x + 1]
    e_refs = refs[n_x + 2:n_x + 2 + n_e]
    o_ref = refs[n_x + 2 + n_e]
    x = pre(*[r[...] for r in x_refs])
    z = mid(_bdot(x, w1_ref[...]))
    o_ref[...] = post(_bdot(z, w2_ref[...]), *[e[...] for e in e_refs]).astype(o_ref.dtype)


def _lora_call(xs, w1, w2, extras=(), *, pre=_identity, mid=_identity, post=_identity, bm=512, name="lora"):
    K, R = w1.shape
    N = w2.shape[1]
    M = max(x.shape[0] for x in xs)
    bm = min(bm, M)
    x_specs = [pl.BlockSpec((bm, K), lambda m: (m, 0)) if x.shape[0] == M
               else pl.BlockSpec((x.shape[0], K), lambda m: (0, 0)) for x in xs]
    e_specs = [pl.BlockSpec((bm, N), lambda m: (m, 0)) if e.shape[0] == M
               else pl.BlockSpec((1, N), lambda m: (0, 0)) for e in extras]
    return pl.pallas_call(
        functools.partial(_lora_kernel, n_x=len(xs), n_e=len(extras), pre=pre, mid=mid, post=post),
        grid=(M // bm,),
        in_specs=x_specs + [pl.BlockSpec((K, R), lambda m: (0, 0)), pl.BlockSpec((R, N), lambda m: (0, 0))]
        + e_specs,
        out_specs=pl.BlockSpec((bm, N), lambda m: (m, 0)),
        out_shape=jax.ShapeDtypeStruct((M, N), F32),
        compiler_params=pltpu.CompilerParams(dimension_semantics=("parallel",),
                                             vmem_limit_bytes=VMEM_LIMIT),
        name=name,
    )(*xs, w1, w2, *extras)


def _swa_kernel(slopes_ref, q_ref, kp_ref, vp_ref, kc_ref, vc_ref, o_ref, lse_ref, *, tq, n_cur, dil, group):
    tp = ATT_STEPS
    lane_q = lax.broadcasted_iota(jnp.int32, (tq, LANES), 1) < HD_B
    iq_p = lax.broadcasted_iota(jnp.int32, (tq, tp), 0)
    jk_p = lax.broadcasted_iota(jnp.int32, (tq, tp), 1)
    steps_p = iq_p + tp - jk_p
    valid_p = steps_p <= ATT_STEPS
    iq_c = lax.broadcasted_iota(jnp.int32, (tq, tq), 0)
    jk_c = lax.broadcasted_iota(jnp.int32, (tq, tq), 1)
    steps_c = iq_c - jk_c
    valid_c = (steps_c >= 0) & (jk_c < n_cur)
    scale = HD_B ** -0.5

    def head(hp, h):
        sl = slice(hp * LANES, (hp + 1) * LANES)
        slope = slopes_ref[group * H_B + 2 * hp + h]
        q = q_ref[0, :, sl]
        qh = jnp.where(lane_q, q, 0.0) if h == 0 else jnp.where(lane_q, 0.0, q)
        sp = _bdot_nt(qh, kp_ref[0, :, sl])
        sc = _bdot_nt(qh, kc_ref[0, :, sl])
        yield
        sp = sp * scale - slope * (steps_p * dil).astype(F32)
        sc = sc * scale - slope * (steps_c * dil).astype(F32)
        sp = jnp.where(valid_p, sp, -jnp.inf)
        sc = jnp.where(valid_c, sc, -jnp.inf)
        m = jnp.maximum(jnp.max(sp, axis=-1, keepdims=True), jnp.max(sc, axis=-1, keepdims=True))
        pp = jnp.exp(sp - m)
        pc = jnp.exp(sc - m)
        l = jnp.sum(pp, axis=-1, keepdims=True) + jnp.sum(pc, axis=-1, keepdims=True)
        acc = _bdot(pp, vp_ref[0, :, sl]) + _bdot(pc, vc_ref[0, :, sl])
        yield
        return acc / l, jnp.broadcast_to(m + jnp.log(l), (tq, LANES))

    n_hp = H_B // 2
    res = _lockstep([head(hp, h) for hp in range(n_hp) for h in range(2)])
    for hp in range(n_hp):
        sl = slice(hp * LANES, (hp + 1) * LANES)
        (o0, l0), (o1, l1) = res[2 * hp], res[2 * hp + 1]
        o_ref[0, :, sl] = jnp.where(lane_q, o0, o1)
        lse_ref[0, :, sl] = jnp.where(lane_q, l0, l1)


def _swa_call(slopes, new, cache, *, n_streams, tq, n_cur, dil, group, name):
    B = new.shape[0]
    hw = H_B * HD_B
    in_specs = [
        pl.BlockSpec(memory_space=pltpu.SMEM),
        pl.BlockSpec((1, tq, hw), lambda b, c: (b, 0, 3 * c)),
        pl.BlockSpec((1, ATT_STEPS, hw), lambda b, c: (b, 0, 2 * c)),
        pl.BlockSpec((1, ATT_STEPS, hw), lambda b, c: (b, 0, 2 * c + 1)),
        pl.BlockSpec((1, tq, hw), lambda b, c: (b, 0, 3 * c + 1)),
        pl.BlockSpec((1, tq, hw), lambda b, c: (b, 0, 3 * c + 2)),
    ]
    out_spec = pl.BlockSpec((1, tq, hw), lambda b, c: (b, 0, c))
    out_sds = jax.ShapeDtypeStruct((B, tq, n_streams * hw), F32)
    return pl.pallas_call(
        functools.partial(_swa_kernel, tq=tq, n_cur=n_cur, dil=dil, group=group),
        grid=(B, n_streams),
        in_specs=in_specs,
        out_specs=[out_spec, out_spec],
        out_shape=[out_sds, out_sds],
        compiler_params=pltpu.CompilerParams(dimension_semantics=("parallel", "parallel")),
        name=name,
    )(slopes, new, cache, cache, new, new)


ATT_BLOCK = ATT_STEPS * max(d for _, d in GROUPS)
ATT_LANES = 4


def _attn_prompt_kernel(slopes_ref, *refs):
    ng = N_GROUPS
    q_refs, kc_refs, vc_refs = refs[0:ng], refs[ng:2 * ng], refs[2 * ng:3 * ng]
    kp_refs, vp_refs = refs[3 * ng:4 * ng], refs[4 * ng:5 * ng]
    o_ref = refs[5 * ng]
    m_scr, l_scr, acc_scr = refs[5 * ng + 1:]
    hp = pl.program_id(1)
    j = pl.program_id(2)
    S = ATT_STEPS
    lane_lo = lax.broadcasted_iota(jnp.int32, (S, LANES), 1) < HD_B
    iq = lax.broadcasted_iota(jnp.int32, (S, S), 0)
    jk = lax.broadcasted_iota(jnp.int32, (S, S), 1)
    steps_p = iq + S - jk
    steps_c = iq - jk
    mask_p = jnp.where(steps_p <= S, 0.0, -jnp.inf)
    mask_c = jnp.where(steps_c >= 0, 0.0, -jnp.inf)
    scale = HD_B ** -0.5
    no_prev = jnp.where(j > 0, 0.0, -jnp.inf)

    for g, (_, dil) in enumerate(GROUPS):
        biases = []
        for h in range(2):
            slope = slopes_ref[g * H_B + 2 * hp + h]
            biases.append((mask_p - slope * (steps_p * dil).astype(F32),
                           mask_c - slope * (steps_c * dil).astype(F32)))
        q_ref, kc_ref, vc_ref, kp_ref, vp_ref = q_refs[g], kc_refs[g], vc_refs[g], kp_refs[g], vp_refs[g]

        def attend(cur_start, prev_ref, prev_start, prev_bias, dil=dil, g=g, biases=biases, q_ref=q_ref,
                   kc_ref=kc_ref, vc_ref=vc_ref):
            cur = pl.ds(cur_start, S, stride=dil) if dil > 1 else pl.ds(cur_start, S)
            prv = pl.ds(prev_start, S, stride=dil) if dil > 1 else pl.ds(prev_start, S)
            q, kc, vc = q_ref[0, cur, :], kc_ref[0, cur, :], vc_ref[0, cur, :]
            kp, vp = prev_ref[0][0, prv, :], prev_ref[1][0, prv, :]
            qs = [jnp.where(lane_lo, q, 0.0), jnp.where(lane_lo, 0.0, q)]
            scores = [(_bdot_nt(qh, kp), _bdot_nt(qh, kc)) for qh in qs]
            yield
            ms, ls, probs = [], [], []
            for h in range(2):
                sp = scores[h][0] * scale + biases[h][0]
                if prev_bias is not None:
                    sp = sp + prev_bias
                sc = scores[h][1] * scale + biases[h][1]
                m = jnp.max(jnp.maximum(sp, sc), axis=-1, keepdims=True)
                pp = jnp.exp(sp - m)
                pc = jnp.exp(sc - m)
                ls.append(jnp.sum(pp + pc, axis=-1, keepdims=True))
                probs.append((pp, pc))
                ms.append(m)
            accs = [_bdot(pp, vp) + _bdot(pc, vc) for pp, pc in probs]
            yield
            m = jnp.where(lane_lo, ms[0], ms[1])
            l = jnp.where(lane_lo, ls[0], ls[1])
            acc = jnp.where(lane_lo, accs[0], accs[1])
            if g > 0:
                m_old = m_scr[cur, :]
                m_new = jnp.maximum(m_old, m)
                w_old, w_new = jnp.exp(m_old - m_new), jnp.exp(m - m_new)
                l = l_scr[cur, :] * w_old + l * w_new
                acc = acc_scr[cur, :] * w_old + acc * w_new
                m = m_new
            m_scr[cur, :] = m
            l_scr[cur, :] = l
            acc_scr[cur, :] = acc

        sd = S * dil
        n_sub = ATT_BLOCK // sd
        run = lambda items, attend=attend: _lockstep([attend(*it) for it in items])
        first = lambda c, kp_ref=kp_ref, vp_ref=vp_ref: (c, (kp_ref, vp_ref), c, no_prev)
        later = lambda c, i, sd=sd, kc_ref=kc_ref, vc_ref=vc_ref: (
            i * sd + c, (kc_ref, vc_ref), (i - 1) * sd + c, None)

        def loop(lo, hi, fn):
            def body(idx, carry):
                fn(idx)
                return carry
            lax.fori_loop(lo, hi, body, 0)

        W = ATT_LANES
        if dil == 1:
            hb = n_sub // W
            run([first(0)] + [later(0, k * hb) for k in range(1, W)])
            loop(1, hb, lambda i, hb=hb: run([later(0, i + k * hb) for k in range(W)]))
        else:
            cs = dil // W
            def streams(c, cs=cs, n_sub=n_sub):
                run([first(c + k * cs) for k in range(W)])
                if n_sub > 1:
                    loop(1, n_sub, lambda i: run([later(c + k * cs, i) for k in range(W)]))
            if cs == 1:
                streams(0)
            else:
                loop(0, cs, streams)

    o_ref[0] = (acc_scr[...] / l_scr[...]).astype(o_ref.dtype)


def _attn_prompt_call(qkv, slopes):
    B, T, W = qkv.shape
    hw = H_B * HD_B
    n_hp = H_B // 2
    TB = ATT_BLOCK
    assert T % TB == 0
    col = lambda g, which, hp: (g * 3 * hw + which * hw) // LANES + hp
    cur = lambda g, which: pl.BlockSpec((1, TB, LANES), lambda b, hp, j: (b, j, col(g, which, hp)))
    def prev(g, which):
        rows = ATT_STEPS * GROUPS[g][1]
        per = TB // rows
        return pl.BlockSpec((1, rows, LANES),
                            lambda b, hp, j: (b, jnp.maximum(j * per - 1, 0), col(g, which, hp)))
    gs = range(N_GROUPS)
    in_specs = ([pl.BlockSpec(memory_space=pltpu.SMEM)]
                + [cur(g, 0) for g in gs] + [cur(g, 1) for g in gs] + [cur(g, 2) for g in gs]
                + [prev(g, 1) for g in gs] + [prev(g, 2) for g in gs])
    return pl.pallas_call(
        _attn_prompt_kernel,
        grid=(B, n_hp, T // TB),
        in_specs=in_specs,
        out_specs=pl.BlockSpec((1, TB, LANES), lambda b, hp, j: (b, j, hp)),
        out_shape=jax.ShapeDtypeStruct((B, T, hw), BF16),
        scratch_shapes=[pltpu.VMEM((TB, LANES), F32)] * 3,
        compiler_params=pltpu.CompilerParams(
            dimension_semantics=("parallel", "parallel", "arbitrary"), vmem_limit_bytes=VMEM_LIMIT),
        name="attn_prompt",
    )(slopes, *([qkv] * (5 * N_GROUPS)))


def _kv_window_kernel(x_ref, o_ref):
    for hp in range(H_B // 2):
        sl = slice(hp * LANES, (hp + 1) * LANES)
        o_ref[0, 0, sl, :] = x_ref[0, :, sl].T


def _kv_window_call(qkv, g, window):
    B, T, _ = qkv.shape
    hw = H_B * HD_B
    wb = min(window, 512)
    first = (T - window) // wb
    out = pl.pallas_call(
        _kv_window_kernel,
        grid=(B, 2, window // wb),
        in_specs=[pl.BlockSpec((1, wb, hw), lambda b, kv, i: (b, first + i, 3 * g + 1 + kv))],
        out_specs=pl.BlockSpec((1, 1, hw, wb), lambda b, kv, i: (b, kv, 0, i)),
        out_shape=jax.ShapeDtypeStruct((B, 2, hw, window), F32),
        compiler_params=pltpu.CompilerParams(dimension_semantics=("parallel", "parallel", "parallel")),
        name=f"kv_window_g{g}",
    )(qkv)
    return out.reshape(B, 2, H_B, HD_B, window).transpose(0, 4, 1, 2, 3)


def _attn_decode_kernel(slopes_ref, *refs):
    ng = N_GROUPS
    q_refs, k_refs, v_refs = refs[0:ng], refs[ng:2 * ng], refs[2 * ng:3 * ng]
    kt_refs, vt_refs = refs[3 * ng:4 * ng], refs[4 * ng:5 * ng]
    o_ref = refs[5 * ng]
    hp = pl.program_id(1)
    S = q_refs[0].shape[1]
    lane_lo = lax.broadcasted_iota(jnp.int32, (S, LANES), 1) < HD_B
    scale = HD_B ** -0.5

    def chain(g, h):
        window, dil = GROUPS[g]
        slope = slopes_ref[g * H_B + 2 * hp + h]
        q = q_refs[g][0]
        qh = jnp.where(lane_lo, q, 0.0) if h == 0 else jnp.where(lane_lo, 0.0, q)
        s_old = _bdot(qh, kt_refs[g][0, 0])
        s_new = _bdot_nt(qh, k_refs[g][0])
        yield
        j_o = lax.broadcasted_iota(jnp.int32, (S, window), 0)
        w_o = lax.broadcasted_iota(jnp.int32, (S, window), 1)
        dist_o = window + j_o - w_o
        ok_o = ((dist_o & (dil - 1)) == 0) & (dist_o <= window)
        j_n = lax.broadcasted_iota(jnp.int32, (S, S), 0)
        dist_n = j_n - lax.broadcasted_iota(jnp.int32, (S, S), 1)
        ok_n = ((dist_n & (dil - 1)) == 0) & (dist_n >= 0)
        s_old = jnp.where(ok_o, s_old * scale - slope * dist_o.astype(F32), -jnp.inf)
        s_new = jnp.where(ok_n, s_new * scale - slope * dist_n.astype(F32), -jnp.inf)
        m = jnp.maximum(jnp.max(s_old, axis=-1, keepdims=True), jnp.max(s_new, axis=-1, keepdims=True))
        p_old = jnp.exp(s_old - m)
        p_new = jnp.exp(s_new - m)
        l = jnp.sum(p_old, axis=-1, keepdims=True) + jnp.sum(p_new, axis=-1, keepdims=True)
        acc = _bdot_nt(p_old, vt_refs[g][0, 0]) + _bdot(p_new, v_refs[g][0])
        yield
        return m, l, acc

    res = _lockstep([chain(g, h) for g in range(ng) for h in range(2)])
    m = l = acc = None
    for g in range(ng):
        (m0, l0, a0), (m1, l1, a1) = res[2 * g], res[2 * g + 1]
        mg = jnp.where(lane_lo, m0, m1)
        lg = jnp.where(lane_lo, l0, l1)
        ag = jnp.where(lane_lo, a0, a1)
        if g == 0:
            m, l, acc = mg, lg, ag
        else:
            m_new = jnp.maximum(m, mg)
            w_old, w_new = jnp.exp(m - m_new), jnp.exp(mg - m_new)
            l = l * w_old + lg * w_new
            acc = acc * w_old + ag * w_new
            m = m_new
    o_ref[0] = (acc / l).astype(o_ref.dtype)


def _attn_decode_call(qkv, caches, slopes):
    B, S, _ = qkv.shape
    hw = H_B * HD_B
    n_hp = H_B // 2
    assert all(w >= S and d & (d - 1) == 0 for w, d in GROUPS)
    cache_t = [c.transpose(0, 2, 3, 4, 1).reshape(B, 2, hw, c.shape[1]) for c in caches]
    col = lambda g, which, hp: (g * 3 * hw + which * hw) // LANES + hp
    new = lambda g, which: pl.BlockSpec((1, S, LANES), lambda b, hp: (b, 0, col(g, which, hp)))
    old = lambda g, kv: pl.BlockSpec((1, 1, LANES, GROUPS[g][0]), lambda b, hp: (b, kv, hp, 0))
    gs = range(N_GROUPS)
    in_specs = ([pl.BlockSpec(memory_space=pltpu.SMEM)]
                + [new(g, 0) for g in gs] + [new(g, 1) for g in gs] + [new(g, 2) for g in gs]
                + [old(g, 0) for g in gs] + [old(g, 1) for g in gs])
    return pl.pallas_call(
        _attn_decode_kernel,
        grid=(B, n_hp),
        in_specs=in_specs,
        out_specs=pl.BlockSpec((1, S, LANES), lambda b, hp: (b, 0, hp)),
        out_shape=jax.ShapeDtypeStruct((B, S, hw), BF16),
        compiler_params=pltpu.CompilerParams(dimension_semantics=("parallel", "parallel")),
        name="attn_decode",
    )(slopes, *([qkv] * (3 * N_GROUPS)), *cache_t, *cache_t)


def _merge_groups(o0, l0, o1, l1, o2, l2):
    m = jnp.maximum(jnp.maximum(l0, l1), l2)
    w0, w1, w2 = jnp.exp(l0 - m), jnp.exp(l1 - m), jnp.exp(l2 - m)
    return (w0 * o0 + w1 * o1 + w2 * o2) / (w0 + w1 + w2)


def _alibi_slopes():
    n = N_GROUPS * H_B
    return 2.0 ** (-8.0 * jnp.arange(1, n + 1, dtype=F32) / n)


def _pair_states(s):
    B, H = s.shape[:2]
    s = s.reshape(B, H // 2, 2, HEAD_A, HEAD_A)
    z = jnp.zeros_like(s[:, :, 0])
    top = jnp.concatenate([s[:, :, 0], z], axis=-1)
    bot = jnp.concatenate([z, s[:, :, 1]], axis=-1)
    return jnp.concatenate([top, bot], axis=-2)


def _unpair_states(s2):
    B, P = s2.shape[:2]
    return jnp.stack([s2[:, :, :HEAD_A, :HEAD_A], s2[:, :, HEAD_A:, HEAD_A:]],
                     axis=2).reshape(B, 2 * P, HEAD_A, HEAD_A)


def _norm_mix_kernel(x_ref, g_ref, mu_ref, shift_ref, *refs, tiles_per_seq):
    out_refs, hlast_ref, carry_scr = refs[:-2], refs[-2], refs[-1]
    @pl.when(pl.program_id(0) == 0)
    def _():
        carry_scr[...] = jnp.zeros_like(carry_scr)

    x = x_ref[0]
    h = x * lax.rsqrt(jnp.mean(x * x, axis=-1, keepdims=True) + RMS_EPS) * g_ref[...]
    rows = h.shape[0]
    first_tile = pl.program_id(0) % tiles_per_seq == 0
    row0 = jnp.where(first_tile, shift_ref[0], carry_scr[...])
    prev = pltpu.roll(h, 1, 0) if rows > 1 else h
    prev = jnp.where(lax.broadcasted_iota(jnp.int32, h.shape, 0) == 0, row0, prev)
    carry_scr[...] = h[rows - 1:rows, :]
    hlast_ref[0] = h[rows - 1:rows, :]
    diff = prev - h
    for i, o_ref in enumerate(out_refs):
        o_ref[0] = (h + diff * mu_ref[i:i + 1, :]).astype(o_ref.dtype)


def _norm_mix_call(x, g, mu, shift0, bt=512):
    B, T, D = x.shape
    bt = min(bt, T)
    assert T % bt == 0
    n_mix = mu.shape[0]
    tiles = T // bt
    seq = pl.BlockSpec((1, bt, D), lambda m: (m // tiles, m % tiles, 0))
    per_seq = pl.BlockSpec((1, 1, D), lambda m: (m // tiles, 0, 0))
    outs = pl.pallas_call(
        functools.partial(_norm_mix_kernel, tiles_per_seq=tiles),
        grid=(B * tiles,),
        in_specs=[seq, pl.BlockSpec((1, D), lambda m: (0, 0)), pl.BlockSpec((n_mix, D), lambda m: (0, 0)),
                  per_seq],
        out_specs=[seq] * n_mix + [per_seq],
        out_shape=[jax.ShapeDtypeStruct((B, T, D), BF16)] * n_mix + [jax.ShapeDtypeStruct((B, 1, D), F32)],
        scratch_shapes=[pltpu.VMEM((1, D), F32)],
        compiler_params=pltpu.CompilerParams(dimension_semantics=("arbitrary",),
                                             vmem_limit_bytes=VMEM_LIMIT),
        name="rwkv_norm_mix",
    )(x, g.reshape(1, D), mu, shift0.reshape(B, 1, D))
    return [o.reshape(B * T, D) for o in outs[:n_mix]], outs[n_mix].reshape(B, D)


def _rwkv_block(x, shift0, wkv0, norm_g, p):
    (mu, w0, w1, w2, a0, a1, a2, g1, g2, k_k, k_a, r_k, lnx_w, lnx_b, w_r, w_k, w_v, w_o) = p
    B, T, D = x.shape
    M = B * T
    x2 = x.reshape(M, D)
    (xr, xw, xk, xv, xa, xg), h_last = _norm_mix_call(x, norm_g, mu, shift0)
    big = dict(bm=1024, bn=1024)
    r = _matmul_call([xr], w_r, name="rwkv_r", **big)
    k = _matmul_call([xk], w_k, name="rwkv_k", **big)
    v = _matmul_call([xv], w_v, name="rwkv_v", **big)
    wlog = _lora_call([xw], w1, w2, [w0.reshape(1, D)], mid=jnp.tanh,
                      post=lambda z, b: -jax.nn.softplus(-(b + z)) - 0.5, name="rwkv_w")
    a = _lora_call([xa], a1, a2, [a0.reshape(1, D)],
                   post=lambda z, b: jax.nn.sigmoid(b + z), name="rwkv_a")
    g = _lora_call([xg], g1, g2, mid=jax.nn.sigmoid, name="rwkv_g")

    L = WKV_CHUNK
    Tp = -(-T // L) * L
    def seq(z, fill=0.0):
        z = z.reshape(B, T, D)
        if Tp != T:
            z = jnp.pad(z, ((0, 0), (0, Tp - T), (0, 0)), constant_values=fill)
        return z
    y, s2 = _wkv_call(seq(r), seq(k), seq(v), seq(wlog, -jnp.inf), seq(a), seq(g),
                      k_k, k_a, r_k.reshape(D), lnx_w, lnx_b, _pair_states(wkv0), pairs_per_step=D // LANES)
    y2 = y[:, :T].reshape(M, D)
    out = _matmul_call([y2], w_o, [x2], post=lambda acc, res: res + acc, name="rwkv_o", **big)
    return out.reshape(B, T, D), _unpair_states(s2), h_last


def _ffn_block(x2, norm_g, w_up, w_down, layer):
    hn = _rmsnorm_call(x2, norm_g, BF16)
    mid = _matmul_call([hn], w_up, post=lambda acc: jnp.square(jnp.maximum(acc, 0.0)),
                       out_dtype=BF16, bm=1024, bn=1024, layer=layer, name="ffn_up")
    return _matmul_call([mid], w_down, [x2], post=lambda acc, res: res + acc,
                        bm=1024, bn=1024, bk=2048, layer=layer, name="ffn_down")


def _attn_block(x, norm_g, w_qkv, w_o, slopes, caches=None):
    B, T, D = x.shape
    M = B * T
    x2 = x.reshape(M, D)
    hn = _rmsnorm_call(x2, norm_g, BF16)
    qkv = _matmul_call([hn], w_qkv, bm=1024, bn=1024, name="attn_qkv").reshape(B, T, -1)
    hw = H_B * HD_B
    if caches is None:
        merged = _attn_prompt_call(qkv, slopes)
        bufs = [_kv_window_call(qkv, g, min(w, T)) for g, (w, _) in enumerate(GROUPS)]
    else:
        merged = _attn_decode_call(qkv, caches, slopes)
        new_kv = lambda g: qkv[:, :, (3 * g + 1) * hw:(3 * g + 3) * hw].reshape(B, T, 2, H_B, HD_B)
        bufs = [jnp.concatenate([caches[g][:, T:], new_kv(g)], axis=1) for g in range(N_GROUPS)]
    out = _matmul_call([merged.reshape(M, hw)], w_o, [x2], post=lambda acc, res: res + acc,
                       bm=1024, name="attn_o")
    return out.reshape(B, T, D), bufs


def kernel(x_prompt, x_sample, state_wkv, state_shift, cache_kv_g1, cache_kv_g2, cache_kv_g3,
           norm_mix, norm_ffn, norm_final,
           rwkv_mu, rwkv_w0, rwkv_w1, rwkv_w2, rwkv_a0, rwkv_a1, rwkv_a2, rwkv_g1, rwkv_g2,
           rwkv_k_k, rwkv_k_a, rwkv_r_k, rwkv_lnx_w, rwkv_lnx_b, rwkv_w_r, rwkv_w_k, rwkv_w_v, rwkv_w_o,
           attn_w_qkv, attn_w_o, ffn_w_up, ffn_w_down):
    rwkv_params = (rwkv_mu, rwkv_w0, rwkv_w1, rwkv_w2, rwkv_a0, rwkv_a1, rwkv_a2, rwkv_g1, rwkv_g2,
                   rwkv_k_k, rwkv_k_a, rwkv_r_k, rwkv_lnx_w, rwkv_lnx_b,
                   rwkv_w_r, rwkv_w_k, rwkv_w_v, rwkv_w_o)
    slopes = _alibi_slopes()
    Bp, Tp, D = x_prompt.shape
    Bs, Ts, _ = x_sample.shape

    def ffn(x, i):
        B, T, _ = x.shape
        return _ffn_block(x.reshape(B * T, D), norm_ffn[i], ffn_w_up, ffn_w_down, i).reshape(B, T, D)

    xp, wkv_p, shift_p = _rwkv_block(x_prompt, jnp.zeros((Bp, D), F32),
                                     jnp.zeros((Bp, H_A, HEAD_A, HEAD_A), F32), norm_mix[0], rwkv_params)
    xs, wkv_s, shift_s = _rwkv_block(x_sample, state_shift, state_wkv, norm_mix[0], rwkv_params)
    xp, xs = ffn(xp, 0), ffn(xs, 0)
    xp, (kv1_p, kv2_p, kv3_p) = _attn_block(xp, norm_mix[1], attn_w_qkv, attn_w_o, slopes)
    xs, (kv1_s, kv2_s, kv3_s) = _attn_block(xs, norm_mix[1], attn_w_qkv, attn_w_o, slopes,
                                            (cache_kv_g1, cache_kv_g2, cache_kv_g3))
    xp, xs = ffn(xp, 1), ffn(xs, 1)
    y_prompt = _rmsnorm_call(xp.reshape(Bp * Tp, D), norm_final).reshape(Bp, Tp, D)
    y_sample = _rmsnorm_call(xs.reshape(Bs * Ts, D), norm_final).reshape(Bs, Ts, D)
    return (y_prompt, y_sample, wkv_p, shift_p, kv1_p, kv2_p, kv3_p,
            wkv_s, shift_s, kv1_s, kv2_s, kv3_s)
```

```python
import functools

import jax
import jax.numpy as jnp
from jax import lax
from jax.experimental import pallas as pl
from jax.experimental.pallas import tpu as pltpu

F32 = jnp.float32
BF16 = jnp.bfloat16

D_MODEL = 2048
HEAD_A = 64
H_A = D_MODEL // HEAD_A
GN_EPS = 64e-5
GROUPS = ((128, 1), (512, 4), (2048, 16))
N_GROUPS = len(GROUPS)
H_B = 16
HD_B = 64
ATT_STEPS = 128
RMS_EPS = 1e-6

LANES = 128
VMEM_LIMIT = 56 * 1024 * 1024


def _bdot(a, b):
    return jnp.dot(a.astype(BF16), b.astype(BF16), preferred_element_type=F32)


def _bdot_nt(a, b):
    return lax.dot_general(a.astype(BF16), b.astype(BF16), (((1,), (1,)), ((), ())),
                           preferred_element_type=F32)


def _bdot_tn(a, b):
    return lax.dot_general(a.astype(BF16), b.astype(BF16), (((0,), (0,)), ((), ())),
                           preferred_element_type=F32)


def _exact_dot(m, x):
    hi = x.astype(BF16)
    r1 = x - hi.astype(F32)
    mid = r1.astype(BF16)
    lo = (r1 - mid.astype(F32)).astype(BF16)
    n = x.shape[1]
    parts = jnp.dot(m.astype(BF16), jnp.concatenate([hi, mid, lo], axis=1), preferred_element_type=F32)
    return (parts[:, :n] + parts[:, n:2 * n]) + parts[:, 2 * n:]


def _lockstep(chains):
    results = [None] * len(chains)
    live = list(range(len(chains)))
    while live:
        for idx in list(live):
            try:
                next(chains[idx])
            except StopIteration as done:
                results[idx] = done.value
                live.remove(idx)
    return results


def _head_sum(x, lane_lo):
    s0 = jnp.sum(jnp.where(lane_lo, x, 0.0), axis=-1, keepdims=True)
    s1 = jnp.sum(jnp.where(lane_lo, 0.0, x), axis=-1, keepdims=True)
    return jnp.where(lane_lo, s0, s1)


WKV_CHUNK = HEAD_A
(_MK_EYE, _MK_LVL0, _MK_BD, _MK_NMASK, _MK_CAUSAL, _MK_TRI, _MK_LEVELS) = range(7)
_N_LEVELS = 5
_N_MASKS = _MK_LEVELS + _N_LEVELS


def _wkv_fill_masks(mk_ref):
    L = WKV_CHUNK
    r2 = lax.broadcasted_iota(jnp.int32, (LANES, LANES), 0)
    c2 = lax.broadcasted_iota(jnp.int32, (LANES, LANES), 1)
    f = lambda m: m.astype(F32)
    same_block = (r2 < L) == (c2 < L)
    strict = (c2 % L) < (r2 % L)
    incl = (c2 % L) <= (r2 % L)
    mk_ref[_MK_EYE] = f(r2 == c2)
    mk_ref[_MK_LVL0] = f((r2 // 2 == c2 // 2) & (c2 < r2))
    mk_ref[_MK_BD] = f(same_block)
    mk_ref[_MK_NMASK] = f(strict & same_block)
    mk_ref[_MK_CAUSAL] = f(strict | (incl & (r2 >= L)))
    mk_ref[_MK_TRI] = f(incl & (r2 < L) & (c2 < L))
    size = 2
    for lvl in range(_N_LEVELS):
        mk_ref[_MK_LEVELS + lvl] = f((r2 // (2 * size) == c2 // (2 * size))
                                     & ((r2 // size) % 2 == 1) & ((c2 // size) % 2 == 0) & same_block)
        size *= 2


def _wkv_pair_chunk(r, kraw, v, wl, a, g, k_k, k_a, r_k, lnx_w, lnx_b, s2, mk_ref):
    L = WKV_CHUNK
    lane_lo = lax.broadcasted_iota(jnp.int32, (L, LANES), 1) < HEAD_A
    lo2 = lax.broadcasted_iota(jnp.int32, (2 * L, LANES), 1) < HEAD_A

    logd = -jnp.exp(wl)
    kkraw = kraw * k_k
    kk = kkraw / jnp.maximum(jnp.sqrt(_head_sum(kkraw * kkraw, lane_lo)), 1e-12)
    k2 = kraw * (1.0 + (a - 1.0) * k_a)

    cum = _exact_dot(mk_ref[_MK_TRI][:L, :L], logd)
    w_incl = jnp.exp(cum)
    w_excl = jnp.exp(cum - logd)
    w_inv = jnp.exp(-cum)
    w_last = w_incl[L - 1:L, :]

    at = -kk * w_excl
    bt = (kk * a) * w_inv
    kt = k2 * w_inv
    rt = r * w_incl

    ar = jnp.concatenate([at, rt], axis=0)
    bk = jnp.concatenate([bt, kt], axis=0)
    kb = jnp.concatenate([kt, bt], axis=0)
    mm = _bdot_nt(jnp.concatenate([jnp.where(lo2, ar, 0.0), jnp.where(lo2, 0.0, ar)], axis=0),
                  jnp.concatenate([bk, kb], axis=0))
    m0 = mm[:2 * L, :2 * L]
    m1 = mm[2 * L:, 2 * L:]
    aprp = _bdot_nt(ar, s2)
    yield

    a0 = m0[:L]
    a1 = m1[:L]
    n2 = jnp.concatenate([a0, a1], axis=0) * mk_ref[_MK_NMASK]

    t2 = mk_ref[_MK_EYE] + n2 * mk_ref[_MK_LVL0]
    for lvl in range(_N_LEVELS):
        tc = _bdot(t2, n2 * mk_ref[_MK_LEVELS + lvl])
        yield
        t2 = t2 + _bdot(tc, t2)
        yield
    tcat = t2[:L] + t2[L:]

    causal = mk_ref[_MK_CAUSAL]
    strict, incl = causal[:L], causal[L:]
    v_lo = jnp.where(lane_lo, v, 0.0)
    v_hi = jnp.where(lane_lo, 0.0, v)
    ak_cat = jnp.where(lane_lo, a1, a0) * strict
    rhs = aprp[:L] + _bdot(ak_cat, jnp.concatenate([v_hi, v_lo], axis=0))
    yield
    u = _bdot(tcat, jnp.concatenate([jnp.where(lane_lo, rhs, 0.0), jnp.where(lane_lo, 0.0, rhs)], axis=0))
    yield
    u_lo = jnp.where(lane_lo, u, 0.0)
    u_hi = jnp.where(lane_lo, 0.0, u)

    y = aprp[L:] + _bdot(jnp.concatenate([m0[L:] * incl, m1[L:] * incl], axis=1),
                         jnp.concatenate([u_lo, v_lo, v_hi, u_hi], axis=0))

    delta = _bdot_tn(jnp.concatenate([u, v], axis=0), bk)
    yield
    s_new = (s2 + delta * mk_ref[_MK_BD]) * w_last

    mean = _head_sum(y, lane_lo) * (1.0 / HEAD_A)
    yc = y - mean
    var = _head_sum(yc * yc, lane_lo) * (1.0 / HEAD_A)
    yn = yc * lax.rsqrt(var + GN_EPS) * lnx_w + lnx_b
    bonus = _head_sum(r * k2 * r_k, lane_lo)
    return (yn + bonus * v) * g, s_new


def _wkv_kernel(r_ref, k_ref, v_ref, wl_ref, a_ref, g_ref, kk_ref, ka_ref, rk_ref, lw_ref, lb_ref,
                s0_ref, y_ref, sout_ref, s_scr, mk_scr, *, n_pairs):
    b, pb, c = pl.program_id(0), pl.program_id(1), pl.program_id(2)

    @pl.when((b == 0) & (pb == 0) & (c == 0))
    def _():
        _wkv_fill_masks(mk_scr)

    @pl.when(c == 0)
    def _():
        s_scr[...] = s0_ref[0]

    lanes = [slice(p * LANES, (p + 1) * LANES) for p in range(n_pairs)]
    chains = [_wkv_pair_chunk(
        r_ref[0, :, sl], k_ref[0, :, sl], v_ref[0, :, sl], wl_ref[0, :, sl], a_ref[0, :, sl],
        g_ref[0, :, sl], kk_ref[:, sl], ka_ref[:, sl], rk_ref[:, sl], lw_ref[:, sl], lb_ref[:, sl],
        s_scr[p], mk_scr) for p, sl in enumerate(lanes)]
    for (y, s_new), p, sl in zip(_lockstep(chains), range(n_pairs), lanes):
        y_ref[0, :, sl] = y.astype(y_ref.dtype)
        s_scr[p] = s_new

    @pl.when(c == pl.num_programs(2) - 1)
    def _():
        sout_ref[0] = s_scr[...]


def _wkv_call(r, k, v, wlog, a, g, k_k, k_a, r_k, lnx_w, lnx_b, s2_0, *, pairs_per_step):
    B, T, D = r.shape
    L, P = WKV_CHUNK, pairs_per_step
    assert T % L == 0 and 2 * L == LANES and (D // LANES) % P == 0
    seq = pl.BlockSpec((1, L, P * LANES), lambda b, p, c: (b, c, p))
    par = pl.BlockSpec((1, P * LANES), lambda b, p, c: (0, p))
    st = pl.BlockSpec((1, P, LANES, LANES), lambda b, p, c: (b, p, 0, 0))
    row = lambda x: x.reshape(1, D)
    return pl.pallas_call(
        functools.partial(_wkv_kernel, n_pairs=P),
        grid=(B, D // LANES // P, T // L),
        in_specs=[seq] * 6 + [par] * 5 + [st],
        out_specs=[seq, st],
        out_shape=[jax.ShapeDtypeStruct((B, T, D), BF16),
                   jax.ShapeDtypeStruct((B, D // LANES, LANES, LANES), F32)],
        scratch_shapes=[pltpu.VMEM((P, LANES, LANES), F32), pltpu.VMEM((_N_MASKS, LANES, LANES), F32)],
        compiler_params=pltpu.CompilerParams(
            dimension_semantics=("arbitrary", "arbitrary", "arbitrary")),
        name="wkv7_chunk_scan",
    )(r, k, v, wlog, a, g, row(k_k), row(k_a), row(r_k), row(lnx_w), row(lnx_b), s2_0)


def _rmsnorm(x, g):
    return x * lax.rsqrt(jnp.mean(x * x, axis=-1, keepdims=True) + RMS_EPS) * g


def _rmsnorm_kernel(x_ref, g_ref, o_ref):
    o_ref[...] = _rmsnorm(x_ref[...], g_ref[...]).astype(o_ref.dtype)


def _rmsnorm_call(x, g, out_dtype=F32):
    M, D = x.shape
    bm = min(M, 512)
    return pl.pallas_call(
        _rmsnorm_kernel,
        grid=(M // bm,),
        in_specs=[pl.BlockSpec((bm, D), lambda m: (m, 0)), pl.BlockSpec((1, D), lambda m: (0, 0))],
        out_specs=pl.BlockSpec((bm, D), lambda m: (m, 0)),
        out_shape=jax.ShapeDtypeStruct((M, D), out_dtype),
        compiler_params=pltpu.CompilerParams(dimension_semantics=("parallel",)),
        name="rmsnorm",
    )(x, g.reshape(1, D))


def _mm_kernel(*refs, n_x, n_e, pre, post, nk):
    x_refs = refs[:n_x]
    w_ref = refs[n_x]
    e_refs = refs[n_x + 1:n_x + 1 + n_e]
    o_ref = refs[n_x + 1 + n_e]
    x = pre(*[r[...] for r in x_refs])
    part = jnp.dot(x.astype(BF16), w_ref[...].astype(BF16), preferred_element_type=F32)
    if nk == 1:
        o_ref[...] = post(part, *[e[...] for e in e_refs]).astype(o_ref.dtype)
        return
    acc_ref = refs[-1]
    k = pl.program_id(2)

    @pl.when(k == 0)
    def _():
        acc_ref[...] = jnp.zeros_like(acc_ref)

    acc_ref[...] += part

    @pl.when(k == nk - 1)
    def _():
        o_ref[...] = post(acc_ref[...], *[e[...] for e in e_refs]).astype(o_ref.dtype)


def _mm_rows_kernel(*refs, n_x, n_e, pre, post):
    x_refs = refs[:n_x]
    w_ref = refs[n_x]
    e_refs = refs[n_x + 1:n_x + 1 + n_e]
    o_ref = refs[n_x + 1 + n_e]
    x_scr = refs[-1]

    @pl.when(pl.program_id(1) == 0)
    def _():
        x_scr[...] = pre(*[r[...] for r in x_refs]).astype(BF16)

    part = jnp.dot(x_scr[...], w_ref[...].astype(BF16), preferred_element_type=F32)
    o_ref[...] = post(part, *[e[...] for e in e_refs]).astype(o_ref.dtype)


def _identity(x):
    return x


def _matmul_rows_call(xs, w, extras=(), *, pre=_identity, post=_identity, out_dtype=F32,
                      bm=512, bn=512, layer=None, name="matmul"):
    K, N = w.shape[-2:]
    M = max(x.shape[0] for x in xs)
    bm = min(bm, M)
    bn = min(bn, N)
    assert M % bm == 0 and N % bn == 0
    x_specs = [pl.BlockSpec((bm, K), lambda m, n: (m, 0)) if x.shape[0] == M
               else pl.BlockSpec((x.shape[0], K), lambda m, n: (0, 0)) for x in xs]
    e_specs = [pl.BlockSpec((bm, bn), lambda m, n: (m, n)) if e.shape[0] == M
               else pl.BlockSpec((1, bn), lambda m, n: (0, n)) for e in extras]
    if layer is None:
        w_spec = pl.BlockSpec((K, bn), lambda m, n: (0, n))
    else:
        w_spec = pl.BlockSpec((None, K, bn), lambda m, n: (layer, 0, n))
    return pl.pallas_call(
        functools.partial(_mm_rows_kernel, n_x=len(xs), n_e=len(extras), pre=pre, post=post),
        grid=(M // bm, N // bn),
        in_specs=x_specs + [w_spec] + e_specs,
        out_specs=pl.BlockSpec((bm, bn), lambda m, n: (m, n)),
        out_shape=jax.ShapeDtypeStruct((M, N), out_dtype),
        scratch_shapes=[pltpu.VMEM((bm, K), BF16)],
        compiler_params=pltpu.CompilerParams(
            dimension_semantics=("parallel", "arbitrary"), vmem_limit_bytes=VMEM_LIMIT),
        name=name,
    )(*xs, w, *extras)


def _matmul_call(xs, w, extras=(), *, pre=_identity, post=_identity, out_dtype=F32,
                 bm=512, bn=512, bk=None, layer=None, name="matmul"):
    K, N = w.shape[-2:]
    M = max(x.shape[0] for x in xs)
    bm = min(bm, M)
    bn = min(bn, N)
    bk = K if bk is None else min(bk, K)
    assert M % bm == 0 and N % bn == 0 and K % bk == 0
    nk = K // bk
    x_specs = [pl.BlockSpec((bm, bk), lambda n, m, k: (m, k)) if x.shape[0] == M
               else pl.BlockSpec((x.shape[0], bk), lambda n, m, k: (0, k)) for x in xs]
    e_specs = [pl.BlockSpec((bm, bn), lambda n, m, k: (m, n)) if e.shape[0] == M
               else pl.BlockSpec((1, bn), lambda n, m, k: (0, n)) for e in extras]
    if layer is None:
        w_spec = pl.BlockSpec((bk, bn), lambda n, m, k: (k, n))
    else:
        w_spec = pl.BlockSpec((None, bk, bn), lambda n, m, k: (layer, k, n))
    return pl.pallas_call(
        functools.partial(_mm_kernel, n_x=len(xs), n_e=len(extras), pre=pre, post=post, nk=nk),
        grid=(N // bn, M // bm, nk),
        in_specs=x_specs + [w_spec] + e_specs,
        out_specs=pl.BlockSpec((bm, bn), lambda n, m, k: (m, n)),
        out_shape=jax.ShapeDtypeStruct((M, N), out_dtype),
        scratch_shapes=[pltpu.VMEM((bm, bn), F32)] if nk > 1 else [],
        compiler_params=pltpu.CompilerParams(
            dimension_semantics=("parallel", "parallel", "arbitrary"),
            vmem_limit_bytes=VMEM_LIMIT),
        name=name,
    )(*xs, w, *extras)


def _lora_kernel(*refs, n_x, n_e, pre, mid, post):
    x_refs = refs[:n_x]
    w1_ref, w2_ref = refs[n_x], refs[n_x + 1]
    e_refs = refs[n_x + 2:n_x + 2 + n_e]
    o_ref = refs[n_x + 2 + n_e]
    x = pre(*[r[...] for r in x_refs])
    z = mid(_bdot(x, w1_ref[...]))
    o_ref[...] = post(_bdot(z, w2_ref[...]), *[e[...] for e in e_refs]).astype(o_ref.dtype)


def _lora_call(xs, w1, w2, extras=(), *, pre=_identity, mid=_identity, post=_identity, bm=512, name="lora"):
    K, R = w1.shape
    N = w2.shape[1]
    M = max(x.shape[0] for x in xs)
    bm = min(bm, M)
    x_specs = [pl.BlockSpec((bm, K), lambda m: (m, 0)) if x.shape[0] == M
               else pl.BlockSpec((x.shape[0], K), lambda m: (0, 0)) for x in xs]
    e_specs = [pl.BlockSpec((bm, N), lambda m: (m, 0)) if e.shape[0] == M
               else pl.BlockSpec((1, N), lambda m: (0, 0)) for e in extras]
    return pl.pallas_call(
        functools.partial(_lora_kernel, n_x=len(xs), n_e=len(extras), pre=pre, mid=mid, post=post),
        grid=(M // bm,),
        in_specs=x_specs + [pl.BlockSpec((K, R), lambda m: (0, 0)), pl.BlockSpec((R, N), lambda m: (0, 0))]
        + e_specs,
        out_specs=pl.BlockSpec((bm, N), lambda m: (m, 0)),
        out_shape=jax.ShapeDtypeStruct((M, N), F32),
        compiler_params=pltpu.CompilerParams(dimension_semantics=("parallel",),
                                             vmem_limit_bytes=VMEM_LIMIT),
        name=name,
    )(*xs, w1, w2, *extras)


def _swa_kernel(slopes_ref, q_ref, kp_ref, vp_ref, kc_ref, vc_ref, o_ref, lse_ref, *, tq, n_cur, dil, group):
    tp = ATT_STEPS
    lane_q = lax.broadcasted_iota(jnp.int32, (tq, LANES), 1) < HD_B
    iq_p = lax.broadcasted_iota(jnp.int32, (tq, tp), 0)
    jk_p = lax.broadcasted_iota(jnp.int32, (tq, tp), 1)
    steps_p = iq_p + tp - jk_p
    valid_p = steps_p <= ATT_STEPS
    iq_c = lax.broadcasted_iota(jnp.int32, (tq, tq), 0)
    jk_c = lax.broadcasted_iota(jnp.int32, (tq, tq), 1)
    steps_c = iq_c - jk_c
    valid_c = (steps_c >= 0) & (jk_c < n_cur)
    scale = HD_B ** -0.5

    def head(hp, h):
        sl = slice(hp * LANES, (hp + 1) * LANES)
        slope = slopes_ref[group * H_B + 2 * hp + h]
        q = q_ref[0, :, sl]
        qh = jnp.where(lane_q, q, 0.0) if h == 0 else jnp.where(lane_q, 0.0, q)
        sp = _bdot_nt(qh, kp_ref[0, :, sl])
        sc = _bdot_nt(qh, kc_ref[0, :, sl])
        yield
        sp = sp * scale - slope * (steps_p * dil).astype(F32)
        sc = sc * scale - slope * (steps_c * dil).astype(F32)
        sp = jnp.where(valid_p, sp, -jnp.inf)
        sc = jnp.where(valid_c, sc, -jnp.inf)
        m = jnp.maximum(jnp.max(sp, axis=-1, keepdims=True), jnp.max(sc, axis=-1, keepdims=True))
        pp = jnp.exp(sp - m)
        pc = jnp.exp(sc - m)
        l = jnp.sum(pp, axis=-1, keepdims=True) + jnp.sum(pc, axis=-1, keepdims=True)
        acc = _bdot(pp, vp_ref[0, :, sl]) + _bdot(pc, vc_ref[0, :, sl])
        yield
        return acc / l, jnp.broadcast_to(m + jnp.log(l), (tq, LANES))

    n_hp = H_B // 2
    res = _lockstep([head(hp, h) for hp in range(n_hp) for h in range(2)])
    for hp in range(n_hp):
        sl = slice(hp * LANES, (hp + 1) * LANES)
        (o0, l0), (o1, l1) = res[2 * hp], res[2 * hp + 1]
        o_ref[0, :, sl] = jnp.where(lane_q, o0, o1)
        lse_ref[0, :, sl] = jnp.where(lane_q, l0, l1)


def _swa_call(slopes, new, cache, *, n_streams, tq, n_cur, dil, group, name):
    B = new.shape[0]
    hw = H_B * HD_B
    in_specs = [
        pl.BlockSpec(memory_space=pltpu.SMEM),
        pl.BlockSpec((1, tq, hw), lambda b, c: (b, 0, 3 * c)),
        pl.BlockSpec((1, ATT_STEPS, hw), lambda b, c: (b, 0, 2 * c)),
        pl.BlockSpec((1, ATT_STEPS, hw), lambda b, c: (b, 0, 2 * c + 1)),
        pl.BlockSpec((1, tq, hw), lambda b, c: (b, 0, 3 * c + 1)),
        pl.BlockSpec((1, tq, hw), lambda b, c: (b, 0, 3 * c + 2)),
    ]
    out_spec = pl.BlockSpec((1, tq, hw), lambda b, c: (b, 0, c))
    out_sds = jax.ShapeDtypeStruct((B, tq, n_streams * hw), F32)
    return pl.pallas_call(
        functools.partial(_swa_kernel, tq=tq, n_cur=n_cur, dil=dil, group=group),
        grid=(B, n_streams),
        in_specs=in_specs,
        out_specs=[out_spec, out_spec],
        out_shape=[out_sds, out_sds],
        compiler_params=pltpu.CompilerParams(dimension_semantics=("parallel", "parallel")),
        name=name,
    )(slopes, new, cache, cache, new, new)


ATT_BLOCK = ATT_STEPS * max(d for _, d in GROUPS)
ATT_LANES = 4


def _attn_prompt_kernel(slopes_ref, *refs):
    ng = N_GROUPS
    q_refs, kc_refs, vc_refs = refs[0:ng], refs[ng:2 * ng], refs[2 * ng:3 * ng]
    kp_refs, vp_refs = refs[3 * ng:4 * ng], refs[4 * ng:5 * ng]
    o_ref = refs[5 * ng]
    m_scr, l_scr, acc_scr = refs[5 * ng + 1:]
    hp = pl.program_id(1)
    j = pl.program_id(2)
    S = ATT_STEPS
    lane_lo = lax.broadcasted_iota(jnp.int32, (S, LANES), 1) < HD_B
    iq = lax.broadcasted_iota(jnp.int32, (S, S), 0)
    jk = lax.broadcasted_iota(jnp.int32, (S, S), 1)
    steps_p = iq + S - jk
    steps_c = iq - jk
    mask_p = jnp.where(steps_p <= S, 0.0, -jnp.inf)
    mask_c = jnp.where(steps_c >= 0, 0.0, -jnp.inf)
    scale = HD_B ** -0.5
    no_prev = jnp.where(j > 0, 0.0, -jnp.inf)

    for g, (_, dil) in enumerate(GROUPS):
        biases = []
        for h in range(2):
            slope = slopes_ref[g * H_B + 2 * hp + h]
            biases.append((mask_p - slope * (steps_p * dil).astype(F32),
                           mask_c - slope * (steps_c * dil).astype(F32)))
        q_ref, kc_ref, vc_ref, kp_ref, vp_ref = q_refs[g], kc_refs[g], vc_refs[g], kp_refs[g], vp_refs[g]

        def attend(cur_start, prev_ref, prev_start, prev_bias, dil=dil, g=g, biases=biases, q_ref=q_ref,
                   kc_ref=kc_ref, vc_ref=vc_ref):
            cur = pl.ds(cur_start, S, stride=dil) if dil > 1 else pl.ds(cur_start, S)
            prv = pl.ds(prev_start, S, stride=dil) if dil > 1 else pl.ds(prev_start, S)
            q, kc, vc = q_ref[0, cur, :], kc_ref[0, cur, :], vc_ref[0, cur, :]
            kp, vp = prev_ref[0][0, prv, :], prev_ref[1][0, prv, :]
            qs = [jnp.where(lane_lo, q, 0.0), jnp.where(lane_lo, 0.0, q)]
            scores = [(_bdot_nt(qh, kp), _bdot_nt(qh, kc)) for qh in qs]
            yield
            ms, ls, probs = [], [], []
            for h in range(2):
                sp = scores[h][0] * scale + biases[h][0]
                if prev_bias is not None:
                    sp = sp + prev_bias
                sc = scores[h][1] * scale + biases[h][1]
                m = jnp.max(jnp.maximum(sp, sc), axis=-1, keepdims=True)
                pp = jnp.exp(sp - m)
                pc = jnp.exp(sc - m)
                ls.append(jnp.sum(pp + pc, axis=-1, keepdims=True))
                probs.append((pp, pc))
                ms.append(m)
            accs = [_bdot(pp, vp) + _bdot(pc, vc) for pp, pc in probs]
            yield
            m = jnp.where(lane_lo, ms[0], ms[1])
            l = jnp.where(lane_lo, ls[0], ls[1])
            acc = jnp.where(lane_lo, accs[0], accs[1])
            if g > 0:
                m_old = m_scr[cur, :]
                m_new = jnp.maximum(m_old, m)
                w_old, w_new = jnp.exp(m_old - m_new), jnp.exp(m - m_new)
                l = l_scr[cur, :] * w_old + l * w_new
                acc = acc_scr[cur, :] * w_old + acc * w_new
                m = m_new
            m_scr[cur, :] = m
            l_scr[cur, :] = l
            acc_scr[cur, :] = acc

        sd = S * dil
        n_sub = ATT_BLOCK // sd
        run = lambda items, attend=attend: _lockstep([attend(*it) for it in items])
        first = lambda c, kp_ref=kp_ref, vp_ref=vp_ref: (c, (kp_ref, vp_ref), c, no_prev)
        later = lambda c, i, sd=sd, kc_ref=kc_ref, vc_ref=vc_ref: (
            i * sd + c, (kc_ref, vc_ref), (i - 1) * sd + c, None)

        def loop(lo, hi, fn):
            def body(idx, carry):
                fn(idx)
                return carry
            lax.fori_loop(lo, hi, body, 0)

        W = ATT_LANES
        if dil == 1:
            hb = n_sub // W
            run([first(0)] + [later(0, k * hb) for k in range(1, W)])
            loop(1, hb, lambda i, hb=hb: run([later(0, i + k * hb) for k in range(W)]))
        else:
            cs = dil // W
            def streams(c, cs=cs, n_sub=n_sub):
                run([first(c + k * cs) for k in range(W)])
                if n_sub > 1:
                    loop(1, n_sub, lambda i: run([later(c + k * cs, i) for k in range(W)]))
            if cs == 1:
                streams(0)
            else:
                loop(0, cs, streams)

    o_ref[0] = (acc_scr[...] / l_scr[...]).astype(o_ref.dtype)


def _attn_prompt_call(qkv, slopes):
    B, T, W = qkv.shape
    hw = H_B * HD_B
    n_hp = H_B // 2
    TB = ATT_BLOCK
    assert T % TB == 0
    col = lambda g, which, hp: (g * 3 * hw + which * hw) // LANES + hp
    cur = lambda g, which: pl.BlockSpec((1, TB, LANES), lambda b, hp, j: (b, j, col(g, which, hp)))
    def prev(g, which):
        rows = ATT_STEPS * GROUPS[g][1]
        per = TB // rows
        return pl.BlockSpec((1, rows, LANES),
                            lambda b, hp, j: (b, jnp.maximum(j * per - 1, 0), col(g, which, hp)))
    gs = range(N_GROUPS)
    in_specs = ([pl.BlockSpec(memory_space=pltpu.SMEM)]
                + [cur(g, 0) for g in gs] + [cur(g, 1) for g in gs] + [cur(g, 2) for g in gs]
                + [prev(g, 1) for g in gs] + [prev(g, 2) for g in gs])
    return pl.pallas_call(
        _attn_prompt_kernel,
        grid=(B, n_hp, T // TB),
        in_specs=in_specs,
        out_specs=pl.BlockSpec((1, TB, LANES), lambda b, hp, j: (b, j, hp)),
        out_shape=jax.ShapeDtypeStruct((B, T, hw), BF16),
        scratch_shapes=[pltpu.VMEM((TB, LANES), F32)] * 3,
        compiler_params=pltpu.CompilerParams(
            dimension_semantics=("parallel", "parallel", "arbitrary"), vmem_limit_bytes=VMEM_LIMIT),
        name="attn_prompt",
    )(slopes, *([qkv] * (5 * N_GROUPS)))


def _kv_window_kernel(x_ref, o_ref):
    for hp in range(H_B // 2):
        sl = slice(hp * LANES, (hp + 1) * LANES)
        o_ref[0, 0, sl, :] = x_ref[0, :, sl].T


def _kv_window_call(qkv, g, window):
    B, T, _ = qkv.shape
    hw = H_B * HD_B
    wb = min(window, 512)
    first = (T - window) // wb
    out = pl.pallas_call(
        _kv_window_kernel,
        grid=(B, 2, window // wb),
        in_specs=[pl.BlockSpec((1, wb, hw), lambda b, kv, i: (b, first + i, 3 * g + 1 + kv))],
        out_specs=pl.BlockSpec((1, 1, hw, wb), lambda b, kv, i: (b, kv, 0, i)),
        out_shape=jax.ShapeDtypeStruct((B, 2, hw, window), F32),
        compiler_params=pltpu.CompilerParams(dimension_semantics=("parallel", "parallel", "parallel")),
        name=f"kv_window_g{g}",
    )(qkv)
    return out.reshape(B, 2, H_B, HD_B, window).transpose(0, 4, 1, 2, 3)


KV_ROLL_ROWS = 256


def _kv_roll_kernel(c_ref, n_ref, o_ref):
    W = c_ref.shape[-1]
    S, R = n_ref.shape[1], n_ref.shape[2]
    rolled = pltpu.roll(c_ref[0, 0], W - S, 1)
    new_t = jnp.concatenate([jnp.zeros((LANES - S, R), F32), n_ref[0]], axis=0).T
    if W > LANES:
        o_ref[0, 0, :, :W - LANES] = rolled[:, :W - LANES]
    lane = lax.broadcasted_iota(jnp.int32, (R, LANES), 1)
    o_ref[0, 0, :, W - LANES:] = jnp.where(lane >= LANES - S, new_t, rolled[:, W - LANES:])


def _kv_roll_call(cache, qkv, g):
    B, window = cache.shape[:2]
    S = qkv.shape[1]
    hw = H_B * HD_B
    R = KV_ROLL_ROWS
    cache_t = cache.transpose(0, 2, 3, 4, 1).reshape(B, 2, hw, window)
    blk = pl.BlockSpec((1, 1, R, window), lambda b, kv, r: (b, kv, r, 0))
    out = pl.pallas_call(
        _kv_roll_kernel,
        grid=(B, 2, hw // R),
        in_specs=[blk, pl.BlockSpec((1, S, R), lambda b, kv, r: (b, 0, (3 * g + 1 + kv) * (hw // R) + r))],
        out_specs=blk,
        out_shape=jax.ShapeDtypeStruct((B, 2, hw, window), F32),
        compiler_params=pltpu.CompilerParams(dimension_semantics=("parallel", "parallel", "parallel")),
        name=f"kv_roll_g{g}",
    )(cache_t, qkv)
    return out.reshape(B, 2, H_B, HD_B, window).transpose(0, 4, 1, 2, 3)


def _attn_decode_kernel(slopes_ref, *refs):
    ng = N_GROUPS
    q_refs, k_refs, v_refs = refs[0:ng], refs[ng:2 * ng], refs[2 * ng:3 * ng]
    kt_refs, vt_refs = refs[3 * ng:4 * ng], refs[4 * ng:5 * ng]
    o_ref = refs[5 * ng]
    hp = pl.program_id(1)
    S = q_refs[0].shape[1]
    lane_lo = lax.broadcasted_iota(jnp.int32, (S, LANES), 1) < HD_B
    scale = HD_B ** -0.5

    def chain(g, h):
        window, dil = GROUPS[g]
        slope = slopes_ref[g * H_B + 2 * hp + h]
        q = q_refs[g][0]
        qh = jnp.where(lane_lo, q, 0.0) if h == 0 else jnp.where(lane_lo, 0.0, q)
        s_old = _bdot(qh, kt_refs[g][0, 0])
        s_new = _bdot_nt(qh, k_refs[g][0])
        yield
        j_o = lax.broadcasted_iota(jnp.int32, (S, window), 0)
        w_o = lax.broadcasted_iota(jnp.int32, (S, window), 1)
        dist_o = window + j_o - w_o
        ok_o = ((dist_o & (dil - 1)) == 0) & (dist_o <= window)
        j_n = lax.broadcasted_iota(jnp.int32, (S, S), 0)
        dist_n = j_n - lax.broadcasted_iota(jnp.int32, (S, S), 1)
        ok_n = ((dist_n & (dil - 1)) == 0) & (dist_n >= 0)
        s_old = jnp.where(ok_o, s_old * scale - slope * dist_o.astype(F32), -jnp.inf)
        s_new = jnp.where(ok_n, s_new * scale - slope * dist_n.astype(F32), -jnp.inf)
        m = jnp.maximum(jnp.max(s_old, axis=-1, keepdims=True), jnp.max(s_new, axis=-1, keepdims=True))
        p_old = jnp.exp(s_old - m)
        p_new = jnp.exp(s_new - m)
        l = jnp.sum(p_old, axis=-1, keepdims=True) + jnp.sum(p_new, axis=-1, keepdims=True)
        acc = _bdot_nt(p_old, vt_refs[g][0, 0]) + _bdot(p_new, v_refs[g][0])
        yield
        return m, l, acc

    res = _lockstep([chain(g, h) for g in range(ng) for h in range(2)])
    m = l = acc = None
    for g in range(ng):
        (m0, l0, a0), (m1, l1, a1) = res[2 * g], res[2 * g + 1]
        mg = jnp.where(lane_lo, m0, m1)
        lg = jnp.where(lane_lo, l0, l1)
        ag = jnp.where(lane_lo, a0, a1)
        if g == 0:
            m, l, acc = mg, lg, ag
        else:
            m_new = jnp.maximum(m, mg)
            w_old, w_new = jnp.exp(m - m_new), jnp.exp(mg - m_new)
            l = l * w_old + lg * w_new
            acc = acc * w_old + ag * w_new
            m = m_new
    o_ref[0] = (acc / l).astype(o_ref.dtype)


def _attn_decode_call(qkv, caches, slopes):
    B, S, _ = qkv.shape
    hw = H_B * HD_B
    n_hp = H_B // 2
    assert all(w >= S and d & (d - 1) == 0 for w, d in GROUPS)
    cache_t = [c.transpose(0, 2, 3, 4, 1).reshape(B, 2, hw, c.shape[1]) for c in caches]
    col = lambda g, which, hp: (g * 3 * hw + which * hw) // LANES + hp
    new = lambda g, which: pl.BlockSpec((1, S, LANES), lambda b, hp: (b, 0, col(g, which, hp)))
    old = lambda g, kv: pl.BlockSpec((1, 1, LANES, GROUPS[g][0]), lambda b, hp: (b, kv, hp, 0))
    gs = range(N_GROUPS)
    in_specs = ([pl.BlockSpec(memory_space=pltpu.SMEM)]
                + [new(g, 0) for g in gs] + [new(g, 1) for g in gs] + [new(g, 2) for g in gs]
                + [old(g, 0) for g in gs] + [old(g, 1) for g in gs])
    return pl.pallas_call(
        _attn_decode_kernel,
        grid=(B, n_hp),
        in_specs=in_specs,
        out_specs=pl.BlockSpec((1, S, LANES), lambda b, hp: (b, 0, hp)),
        out_shape=jax.ShapeDtypeStruct((B, S, hw), BF16),
        compiler_params=pltpu.CompilerParams(dimension_semantics=("parallel", "parallel")),
        name="attn_decode",
    )(slopes, *([qkv] * (3 * N_GROUPS)), *cache_t, *cache_t)


def _merge_groups(o0, l0, o1, l1, o2, l2):
    m = jnp.maximum(jnp.maximum(l0, l1), l2)
    w0, w1, w2 = jnp.exp(l0 - m), jnp.exp(l1 - m), jnp.exp(l2 - m)
    return (w0 * o0 + w1 * o1 + w2 * o2) / (w0 + w1 + w2)


def _alibi_slopes():
    n = N_GROUPS * H_B
    return 2.0 ** (-8.0 * jnp.arange(1, n + 1, dtype=F32) / n)


def _pair_states(s):
    B, H = s.shape[:2]
    s = s.reshape(B, H // 2, 2, HEAD_A, HEAD_A)
    z = jnp.zeros_like(s[:, :, 0])
    top = jnp.concatenate([s[:, :, 0], z], axis=-1)
    bot = jnp.concatenate([z, s[:, :, 1]], axis=-1)
    return jnp.concatenate([top, bot], axis=-2)


def _unpair_states(s2):
    B, P = s2.shape[:2]
    return jnp.stack([s2[:, :, :HEAD_A, :HEAD_A], s2[:, :, HEAD_A:, HEAD_A:]],
                     axis=2).reshape(B, 2 * P, HEAD_A, HEAD_A)


def _norm_mix_kernel(x_ref, g_ref, mu_ref, shift_ref, *refs, tiles_per_seq):
    out_refs, hlast_ref, carry_scr = refs[:-2], refs[-2], refs[-1]
    @pl.when(pl.program_id(0) == 0)
    def _():
        carry_scr[...] = jnp.zeros_like(carry_scr)

    x = x_ref[0]
    h = x * lax.rsqrt(jnp.mean(x * x, axis=-1, keepdims=True) + RMS_EPS) * g_ref[...]
    rows = h.shape[0]
    first_tile = pl.program_id(0) % tiles_per_seq == 0
    row0 = jnp.where(first_tile, shift_ref[0], carry_scr[...])
    prev = pltpu.roll(h, 1, 0) if rows > 1 else h
    prev = jnp.where(lax.broadcasted_iota(jnp.int32, h.shape, 0) == 0, row0, prev)
    carry_scr[...] = h[rows - 1:rows, :]
    hlast_ref[0] = h[rows - 1:rows, :]
    diff = prev - h
    for i, o_ref in enumerate(out_refs):
        o_ref[0] = (h + diff * mu_ref[i:i + 1, :]).astype(o_ref.dtype)


def _norm_mix_call(x, g, mu, shift0, bt=512):
    B, T, D = x.shape
    bt = min(bt, T)
    assert T % bt == 0
    n_mix = mu.shape[0]
    tiles = T // bt
    seq = pl.BlockSpec((1, bt, D), lambda m: (m // tiles, m % tiles, 0))
    per_seq = pl.BlockSpec((1, 1, D), lambda m: (m // tiles, 0, 0))
    outs = pl.pallas_call(
        functools.partial(_norm_mix_kernel, tiles_per_seq=tiles),
        grid=(B * tiles,),
        in_specs=[seq, pl.BlockSpec((1, D), lambda m: (0, 0)), pl.BlockSpec((n_mix, D), lambda m: (0, 0)),
                  per_seq],
        out_specs=[seq] * n_mix + [per_seq],
        out_shape=[jax.ShapeDtypeStruct((B, T, D), BF16)] * n_mix + [jax.ShapeDtypeStruct((B, 1, D), F32)],
        scratch_shapes=[pltpu.VMEM((1, D), F32)],
        compiler_params=pltpu.CompilerParams(dimension_semantics=("arbitrary",),
                                             vmem_limit_bytes=VMEM_LIMIT),
        name="rwkv_norm_mix",
    )(x, g.reshape(1, D), mu, shift0.reshape(B, 1, D))
    return [o.reshape(B * T, D) for o in outs[:n_mix]], outs[n_mix].reshape(B, D)


def _rwkv_block(x, shift0, wkv0, norm_g, p):
    (mu, w0, w1, w2, a0, a1, a2, g1, g2, k_k, k_a, r_k, lnx_w, lnx_b, w_r, w_k, w_v, w_o) = p
    B, T, D = x.shape
    M = B * T
    x2 = x.reshape(M, D)
    (xr, xw, xk, xv, xa, xg), h_last = _norm_mix_call(x, norm_g, mu, shift0)
    big = dict(bm=1024, bn=1024)
    r = _matmul_call([xr], w_r, name="rwkv_r", **big)
    k = _matmul_call([xk], w_k, name="rwkv_k", **big)
    v = _matmul_call([xv], w_v, name="rwkv_v", **big)
    wlog = _lora_call([xw], w1, w2, [w0.reshape(1, D)], mid=jnp.tanh,
                      post=lambda z, b: -jax.nn.softplus(-(b + z)) - 0.5, name="rwkv_w")
    a = _lora_call([xa], a1, a2, [a0.reshape(1, D)],
                   post=lambda z, b: jax.nn.sigmoid(b + z), name="rwkv_a")
    g = _lora_call([xg], g1, g2, mid=jax.nn.sigmoid, name="rwkv_g")

    L = WKV_CHUNK
    Tp = -(-T // L) * L
    def seq(z, fill=0.0):
        z = z.reshape(B, T, D)
        if Tp != T:
            z = jnp.pad(z, ((0, 0), (0, Tp - T), (0, 0)), constant_values=fill)
        return z
    y, s2 = _wkv_call(seq(r), seq(k), seq(v), seq(wlog, -jnp.inf), seq(a), seq(g),
                      k_k, k_a, r_k.reshape(D), lnx_w, lnx_b, _pair_states(wkv0), pairs_per_step=D // LANES)
    y2 = y[:, :T].reshape(M, D)
    out = _matmul_call([y2], w_o, [x2], post=lambda acc, res: res + acc, name="rwkv_o", **big)
    return out.reshape(B, T, D), _unpair_states(s2), h_last


def _ffn_block(x2, norm_g, w_up, w_down, layer):
    mid = _matmul_rows_call([x2, norm_g.reshape(1, -1)], w_up, pre=_rmsnorm,
                            post=lambda acc: jnp.square(jnp.maximum(acc, 0.0)),
                            out_dtype=BF16, bm=1024, bn=1024, layer=layer, name="ffn_up")
    return _matmul_call([mid], w_down, [x2], post=lambda acc, res: res + acc,
                        bm=1024, bn=1024, bk=2048, layer=layer, name="ffn_down")


def _attn_block(x, norm_g, w_qkv, w_o, slopes, caches=None):
    B, T, D = x.shape
    M = B * T
    x2 = x.reshape(M, D)
    qkv = _matmul_rows_call([x2, norm_g.reshape(1, -1)], w_qkv, pre=_rmsnorm,
                            bm=1024, bn=1024, name="attn_qkv").reshape(B, T, -1)
    hw = H_B * HD_B
    if caches is None:
        merged = _attn_prompt_call(qkv, slopes)
        bufs = [_kv_window_call(qkv, g, min(w, T)) for g, (w, _) in enumerate(GROUPS)]
    else:
        merged = _attn_decode_call(qkv, caches, slopes)
        bufs = [_kv_roll_call(caches[g], qkv, g) for g in range(N_GROUPS)]
    out = _matmul_call([merged.reshape(M, hw)], w_o, [x2], post=lambda acc, res: res + acc,
                       bm=1024, name="attn_o")
    return out.reshape(B, T, D), bufs


def kernel(x_prompt, x_sample, state_wkv, state_shift, cache_kv_g1, cache_kv_g2, cache_kv_g3,
           norm_mix, norm_ffn, norm_final,
           rwkv_mu, rwkv_w0, rwkv_w1, rwkv_w2, rwkv_a0, rwkv_a1, rwkv_a2, rwkv_g1, rwkv_g2,
           rwkv_k_k, rwkv_k_a, rwkv_r_k, rwkv_lnx_w, rwkv_lnx_b, rwkv_w_r, rwkv_w_k, rwkv_w_v, rwkv_w_o,
           attn_w_qkv, attn_w_o, ffn_w_up, ffn_w_down):
    rwkv_params = (rwkv_mu, rwkv_w0, rwkv_w1, rwkv_w2, rwkv_a0, rwkv_a1, rwkv_a2, rwkv_g1, rwkv_g2,
                   rwkv_k_k, rwkv_k_a, rwkv_r_k, rwkv_lnx_w, rwkv_lnx_b,
                   rwkv_w_r, rwkv_w_k, rwkv_w_v, rwkv_w_o)
    slopes = _alibi_slopes()
    Bp, Tp, D = x_prompt.shape
    Bs, Ts, _ = x_sample.shape

    def ffn(x, i):
        B, T, _ = x.shape
        return _ffn_block(x.reshape(B * T, D), norm_ffn[i], ffn_w_up, ffn_w_down, i).reshape(B, T, D)

    xp, wkv_p, shift_p = _rwkv_block(x_prompt, jnp.zeros((Bp, D), F32),
                                     jnp.zeros((Bp, H_A, HEAD_A, HEAD_A), F32), norm_mix[0], rwkv_params)
    xs, wkv_s, shift_s = _rwkv_block(x_sample, state_shift, state_wkv, norm_mix[0], rwkv_params)
    xp, xs = ffn(xp, 0), ffn(xs, 0)
    xp, (kv1_p, kv2_p, kv3_p) = _attn_block(xp, norm_mix[1], attn_w_qkv, attn_w_o, slopes)
    xs, (kv1_s, kv2_s, kv3_s) = _attn_block(xs, norm_mix[1], attn_w_qkv, attn_w_o, slopes,
                                            (cache_kv_g1, cache_kv_g2, cache_kv_g3))
    xp, xs = ffn(xp, 1), ffn(xs, 1)
    y_prompt = _rmsnorm_call(xp.reshape(Bp * Tp, D), norm_final).reshape(Bp, Tp, D)
    y_sample = _rmsnorm_call(xs.reshape(Bs * Ts, D), norm_final).reshape(Bs, Ts, D)
    return (y_prompt, y_sample, wkv_p, shift_p, kv1_p, kv2_p, kv3_p,
            wkv_s, shift_s, kv1_s, kv2_s, kv3_s)
```

```python
import functools

import jax
import jax.numpy as jnp
from jax import lax
from jax.experimental import pallas as pl
from jax.experimental.pallas import tpu as pltpu

F32 = jnp.float32
BF16 = jnp.bfloat16

D_MODEL = 2048
HEAD_A = 64
H_A = D_MODEL // HEAD_A
GN_EPS = 64e-5
GROUPS = ((128, 1), (512, 4), (2048, 16))
N_GROUPS = len(GROUPS)
H_B = 16
HD_B = 64
ATT_STEPS = 128
RMS_EPS = 1e-6

LANES = 128
VMEM_LIMIT = 56 * 1024 * 1024


def _bdot(a, b):
    return jnp.dot(a.astype(BF16), b.astype(BF16), preferred_element_type=F32)


def _bdot_nt(a, b):
    return lax.dot_general(a.astype(BF16), b.astype(BF16), (((1,), (1,)), ((), ())),
                           preferred_element_type=F32)


def _bdot_tn(a, b):
    return lax.dot_general(a.astype(BF16), b.astype(BF16), (((0,), (0,)), ((), ())),
                           preferred_element_type=F32)


def _exact_dot(m, x):
    hi = x.astype(BF16)
    r1 = x - hi.astype(F32)
    mid = r1.astype(BF16)
    lo = (r1 - mid.astype(F32)).astype(BF16)
    n = x.shape[1]
    parts = jnp.dot(m.astype(BF16), jnp.concatenate([hi, mid, lo], axis=1), preferred_element_type=F32)
    return (parts[:, :n] + parts[:, n:2 * n]) + parts[:, 2 * n:]


def _lockstep(chains):
    results = [None] * len(chains)
    live = list(range(len(chains)))
    while live:
        for idx in list(live):
            try:
                next(chains[idx])
            except StopIteration as done:
                results[idx] = done.value
                live.remove(idx)
    return results


def _head_sum(x, lane_lo):
    s0 = jnp.sum(jnp.where(lane_lo, x, 0.0), axis=-1, keepdims=True)
    s1 = jnp.sum(jnp.where(lane_lo, 0.0, x), axis=-1, keepdims=True)
    return jnp.where(lane_lo, s0, s1)


WKV_CHUNK = HEAD_A
(_MK_EYE, _MK_LVL0, _MK_BD, _MK_NMASK, _MK_CAUSAL, _MK_TRI, _MK_LEVELS) = range(7)
_N_LEVELS = 5
_N_MASKS = _MK_LEVELS + _N_LEVELS


def _wkv_fill_masks(mk_ref):
    L = WKV_CHUNK
    r2 = lax.broadcasted_iota(jnp.int32, (LANES, LANES), 0)
    c2 = lax.broadcasted_iota(jnp.int32, (LANES, LANES), 1)
    f = lambda m: m.astype(F32)
    same_block = (r2 < L) == (c2 < L)
    strict = (c2 % L) < (r2 % L)
    incl = (c2 % L) <= (r2 % L)
    mk_ref[_MK_EYE] = f(r2 == c2)
    mk_ref[_MK_LVL0] = f((r2 // 2 == c2 // 2) & (c2 < r2))
    mk_ref[_MK_BD] = f(same_block)
    mk_ref[_MK_NMASK] = f(strict & same_block)
    mk_ref[_MK_CAUSAL] = f(strict | (incl & (r2 >= L)))
    mk_ref[_MK_TRI] = f(incl & (r2 < L) & (c2 < L))
    size = 2
    for lvl in range(_N_LEVELS):
        mk_ref[_MK_LEVELS + lvl] = f((r2 // (2 * size) == c2 // (2 * size))
                                     & ((r2 // size) % 2 == 1) & ((c2 // size) % 2 == 0) & same_block)
        size *= 2


def _wkv_pair_chunk(r, kraw, v, wl, a, g, k_k, k_a, r_k, lnx_w, lnx_b, s2, mk_ref):
    L = WKV_CHUNK
    lane_lo = lax.broadcasted_iota(jnp.int32, (L, LANES), 1) < HEAD_A
    lo2 = lax.broadcasted_iota(jnp.int32, (2 * L, LANES), 1) < HEAD_A

    logd = -jnp.exp(wl)
    kkraw = kraw * k_k
    kk = kkraw / jnp.maximum(jnp.sqrt(_head_sum(kkraw * kkraw, lane_lo)), 1e-12)
    k2 = kraw * (1.0 + (a - 1.0) * k_a)

    cum = _exact_dot(mk_ref[_MK_TRI][:L, :L], logd)
    w_incl = jnp.exp(cum)
    w_excl = jnp.exp(cum - logd)
    w_inv = jnp.exp(-cum)
    w_last = w_incl[L - 1:L, :]

    at = -kk * w_excl
    bt = (kk * a) * w_inv
    kt = k2 * w_inv
    rt = r * w_incl

    ar = jnp.concatenate([at, rt], axis=0)
    bk = jnp.concatenate([bt, kt], axis=0)
    kb = jnp.concatenate([kt, bt], axis=0)
    mm = _bdot_nt(jnp.concatenate([jnp.where(lo2, ar, 0.0), jnp.where(lo2, 0.0, ar)], axis=0),
                  jnp.concatenate([bk, kb], axis=0))
    m0 = mm[:2 * L, :2 * L]
    m1 = mm[2 * L:, 2 * L:]
    aprp = _bdot_nt(ar, s2)
    yield

    a0 = m0[:L]
    a1 = m1[:L]
    n2 = jnp.concatenate([a0, a1], axis=0) * mk_ref[_MK_NMASK]
    causal = mk_ref[_MK_CAUSAL]
    strict, incl = causal[:L], causal[L:]
    v_lo = jnp.where(lane_lo, v, 0.0)
    v_hi = jnp.where(lane_lo, 0.0, v)
    ak_cat = jnp.where(lane_lo, a1, a0) * strict

    t2 = mk_ref[_MK_EYE] + n2 * mk_ref[_MK_LVL0]
    for lvl in range(_N_LEVELS):
        tc = _bdot(t2, n2 * mk_ref[_MK_LEVELS + lvl])
        yield
        t2 = t2 + _bdot(tc, t2)
        yield
    tcat = t2[:L] + t2[L:]
    rhs = aprp[:L] + _bdot(ak_cat, jnp.concatenate([v_hi, v_lo], axis=0))
    yield

    u = _bdot(tcat, jnp.concatenate([jnp.where(lane_lo, rhs, 0.0), jnp.where(lane_lo, 0.0, rhs)], axis=0))
    yield
    u_lo = jnp.where(lane_lo, u, 0.0)
    u_hi = jnp.where(lane_lo, 0.0, u)

    y = aprp[L:] + _bdot(jnp.concatenate([m0[L:] * incl, m1[L:] * incl], axis=1),
                         jnp.concatenate([u_lo, v_lo, v_hi, u_hi], axis=0))
    delta = _bdot_tn(jnp.concatenate([u, v], axis=0), bk)
    yield
    s_new = (s2 + delta * mk_ref[_MK_BD]) * w_last

    mean = _head_sum(y, lane_lo) * (1.0 / HEAD_A)
    yc = y - mean
    var = _head_sum(yc * yc, lane_lo) * (1.0 / HEAD_A)
    yn = yc * lax.rsqrt(var + GN_EPS) * lnx_w + lnx_b
    bonus = _head_sum(r * k2 * r_k, lane_lo)
    return (yn + bonus * v) * g, s_new


def _wkv_kernel(r_ref, k_ref, v_ref, wl_ref, a_ref, g_ref, kk_ref, ka_ref, rk_ref, lw_ref, lb_ref,
                s0_ref, y_ref, sout_ref, s_scr, mk_scr, *, n_seqs, n_pairs):
    b, pb, c = pl.program_id(0), pl.program_id(1), pl.program_id(2)

    @pl.when((b == 0) & (pb == 0) & (c == 0))
    def _():
        _wkv_fill_masks(mk_scr)

    @pl.when(c == 0)
    def _():
        s_scr[...] = s0_ref[...]

    items = [(q, p, slice(p * LANES, (p + 1) * LANES)) for q in range(n_seqs) for p in range(n_pairs)]
    chains = [_wkv_pair_chunk(
        r_ref[q, :, sl], k_ref[q, :, sl], v_ref[q, :, sl], wl_ref[q, :, sl], a_ref[q, :, sl],
        g_ref[q, :, sl], kk_ref[:, sl], ka_ref[:, sl], rk_ref[:, sl], lw_ref[:, sl], lb_ref[:, sl],
        s_scr[q, p], mk_scr) for q, p, sl in items]
    for (y, s_new), (q, p, sl) in zip(_lockstep(chains), items):
        y_ref[q, :, sl] = y.astype(y_ref.dtype)
        s_scr[q, p] = s_new

    @pl.when(c == pl.num_programs(2) - 1)
    def _():
        sout_ref[...] = s_scr[...]


def _wkv_call(r, k, v, wlog, a, g, k_k, k_a, r_k, lnx_w, lnx_b, s2_0, *, seqs_per_step, pairs_per_step):
    B, T, D = r.shape
    L, P, Q = WKV_CHUNK, pairs_per_step, seqs_per_step
    assert T % L == 0 and 2 * L == LANES and (D // LANES) % P == 0 and B % Q == 0
    seq = pl.BlockSpec((Q, L, P * LANES), lambda b, p, c: (b, c, p))
    par = pl.BlockSpec((1, P * LANES), lambda b, p, c: (0, p))
    st = pl.BlockSpec((Q, P, LANES, LANES), lambda b, p, c: (b, p, 0, 0))
    row = lambda x: x.reshape(1, D)
    return pl.pallas_call(
        functools.partial(_wkv_kernel, n_seqs=Q, n_pairs=P),
        grid=(B // Q, D // LANES // P, T // L),
        in_specs=[seq] * 6 + [par] * 5 + [st],
        out_specs=[seq, st],
        out_shape=[jax.ShapeDtypeStruct((B, T, D), BF16),
                   jax.ShapeDtypeStruct((B, D // LANES, LANES, LANES), F32)],
        scratch_shapes=[pltpu.VMEM((Q, P, LANES, LANES), F32), pltpu.VMEM((_N_MASKS, LANES, LANES), F32)],
        compiler_params=pltpu.CompilerParams(
            dimension_semantics=("arbitrary", "arbitrary", "arbitrary")),
        name="wkv7_chunk_scan",
    )(r, k, v, wlog, a, g, row(k_k), row(k_a), row(r_k), row(lnx_w), row(lnx_b), s2_0)


def _rmsnorm(x, g):
    return x * lax.rsqrt(jnp.mean(x * x, axis=-1, keepdims=True) + RMS_EPS) * g


def _rmsnorm_kernel(x_ref, g_ref, o_ref):
    o_ref[...] = _rmsnorm(x_ref[...], g_ref[...]).astype(o_ref.dtype)


def _rmsnorm_call(x, g, out_dtype=F32):
    M, D = x.shape
    bm = min(M, 512)
    return pl.pallas_call(
        _rmsnorm_kernel,
        grid=(M // bm,),
        in_specs=[pl.BlockSpec((bm, D), lambda m: (m, 0)), pl.BlockSpec((1, D), lambda m: (0, 0))],
        out_specs=pl.BlockSpec((bm, D), lambda m: (m, 0)),
        out_shape=jax.ShapeDtypeStruct((M, D), out_dtype),
        compiler_params=pltpu.CompilerParams(dimension_semantics=("parallel",)),
        name="rmsnorm",
    )(x, g.reshape(1, D))


MM_ROW_CHUNK = 256


def _mm_kernel(*refs, n_x, n_e, pre, post, nk):
    x_refs = refs[:n_x]
    w_ref = refs[n_x]
    e_refs = refs[n_x + 1:n_x + 1 + n_e]
    o_ref = refs[n_x + 1 + n_e]
    bm = o_ref.shape[0]
    acc_ref = refs[-1] if nk > 1 else None
    k = pl.program_id(2)
    if nk > 1:
        @pl.when(k == 0)
        def _():
            acc_ref[...] = jnp.zeros_like(acc_ref)

    w = w_ref[...].astype(BF16)
    chunk = min(bm, MM_ROW_CHUNK)
    for r0 in range(0, bm, chunk):
        rows = slice(r0, r0 + chunk)
        tile = lambda ref: ref[rows, :] if ref.shape[0] == bm else ref[...]
        x = pre(*[tile(r) for r in x_refs])
        part = jnp.dot(x.astype(BF16), w, preferred_element_type=F32)
        if nk == 1:
            o_ref[rows, :] = post(part, *[tile(e) for e in e_refs]).astype(o_ref.dtype)
        else:
            acc_ref[rows, :] += part

    if nk > 1:
        @pl.when(k == nk - 1)
        def _():
            o_ref[...] = post(acc_ref[...], *[e[...] for e in e_refs]).astype(o_ref.dtype)


def _mm_rows_kernel(*refs, n_x, n_e, pre, post):
    x_refs = refs[:n_x]
    w_ref = refs[n_x]
    e_refs = refs[n_x + 1:n_x + 1 + n_e]
    o_ref = refs[n_x + 1 + n_e]
    x_scr = refs[-1]

    @pl.when(pl.program_id(1) == 0)
    def _():
        x_scr[...] = pre(*[r[...] for r in x_refs]).astype(BF16)

    bm = o_ref.shape[0]
    w = w_ref[...].astype(BF16)
    chunk = min(bm, MM_ROW_CHUNK)
    for r0 in range(0, bm, chunk):
        rows = slice(r0, r0 + chunk)
        tile = lambda ref: ref[rows, :] if ref.shape[0] == bm else ref[...]
        part = jnp.dot(x_scr[rows, :], w, preferred_element_type=F32)
        o_ref[rows, :] = post(part, *[tile(e) for e in e_refs]).astype(o_ref.dtype)


def _identity(x):
    return x


def _matmul_rows_call(xs, w, extras=(), *, pre=_identity, post=_identity, out_dtype=F32,
                      bm=512, bn=512, layer=None, name="matmul"):
    K, N = w.shape[-2:]
    M = max(x.shape[0] for x in xs)
    bm = min(bm, M)
    bn = min(bn, N)
    assert M % bm == 0 and N % bn == 0
    x_specs = [pl.BlockSpec((bm, K), lambda m, n: (m, 0)) if x.shape[0] == M
               else pl.BlockSpec((x.shape[0], K), lambda m, n: (0, 0)) for x in xs]
    e_specs = [pl.BlockSpec((bm, bn), lambda m, n: (m, n)) if e.shape[0] == M
               else pl.BlockSpec((1, bn), lambda m, n: (0, n)) for e in extras]
    if layer is None:
        w_spec = pl.BlockSpec((K, bn), lambda m, n: (0, n))
    else:
        w_spec = pl.BlockSpec((None, K, bn), lambda m, n: (layer, 0, n))
    return pl.pallas_call(
        functools.partial(_mm_rows_kernel, n_x=len(xs), n_e=len(extras), pre=pre, post=post),
        grid=(M // bm, N // bn),
        in_specs=x_specs + [w_spec] + e_specs,
        out_specs=pl.BlockSpec((bm, bn), lambda m, n: (m, n)),
        out_shape=jax.ShapeDtypeStruct((M, N), out_dtype),
        scratch_shapes=[pltpu.VMEM((bm, K), BF16)],
        compiler_params=pltpu.CompilerParams(
            dimension_semantics=("parallel", "arbitrary"), vmem_limit_bytes=VMEM_LIMIT),
        name=name,
    )(*xs, w, *extras)


def _matmul_call(xs, w, extras=(), *, pre=_identity, post=_identity, out_dtype=F32,
                 bm=512, bn=512, bk=None, layer=None, name="matmul"):
    K, N = w.shape[-2:]
    M = max(x.shape[0] for x in xs)
    bm = min(bm, M)
    bn = min(bn, N)
    bk = K if bk is None else min(bk, K)
    assert M % bm == 0 and N % bn == 0 and K % bk == 0
    nk = K // bk
    x_specs = [pl.BlockSpec((bm, bk), lambda n, m, k: (m, k)) if x.shape[0] == M
               else pl.BlockSpec((x.shape[0], bk), lambda n, m, k: (0, k)) for x in xs]
    e_specs = [pl.BlockSpec((bm, bn), lambda n, m, k: (m, n)) if e.shape[0] == M
               else pl.BlockSpec((1, bn), lambda n, m, k: (0, n)) for e in extras]
    if layer is None:
        w_spec = pl.BlockSpec((bk, bn), lambda n, m, k: (k, n))
    else:
        w_spec = pl.BlockSpec((None, bk, bn), lambda n, m, k: (layer, k, n))
    return pl.pallas_call(
        functools.partial(_mm_kernel, n_x=len(xs), n_e=len(extras), pre=pre, post=post, nk=nk),
        grid=(N // bn, M // bm, nk),
        in_specs=x_specs + [w_spec] + e_specs,
        out_specs=pl.BlockSpec((bm, bn), lambda n, m, k: (m, n)),
        out_shape=jax.ShapeDtypeStruct((M, N), out_dtype),
        scratch_shapes=[pltpu.VMEM((bm, bn), F32)] if nk > 1 else [],
        compiler_params=pltpu.CompilerParams(
            dimension_semantics=("parallel", "parallel", "arbitrary"),
            vmem_limit_bytes=VMEM_LIMIT),
        name=name,
    )(*xs, w, *extras)


def _lora_kernel(*refs, n_x, n_e, pre, mid, post):
    x_refs = refs[:n_x]
    w1_ref, w2_ref = refs[n_x], refs[n_x + 1]
    e_refs = refs[n_x + 2:n_x + 2 + n_e]
    o_ref = refs[n_x + 2 + n_e]
    x = pre(*[r[...] for r in x_refs])
    z = mid(_bdot(x, w1_ref[...]))
    o_ref[...] = post(_bdot(z, w2_ref[...]), *[e[...] for e in e_refs]).astype(o_ref.dtype)


def _lora_call(xs, w1, w2, extras=(), *, pre=_identity, mid=_identity, post=_identity, bm=512, name="lora"):
    K, R = w1.shape
    N = w2.shape[1]
    M = max(x.shape[0] for x in xs)
    bm = min(bm, M)
    x_specs = [pl.BlockSpec((bm, K), lambda m: (m, 0)) if x.shape[0] == M
               else pl.BlockSpec((x.shape[0], K), lambda m: (0, 0)) for x in xs]
    e_specs = [pl.BlockSpec((bm, N), lambda m: (m, 0)) if e.shape[0] == M
               else pl.BlockSpec((1, N), lambda m: (0, 0)) for e in extras]
    return pl.pallas_call(
        functools.partial(_lora_kernel, n_x=len(xs), n_e=len(extras), pre=pre, mid=mid, post=post),
        grid=(M // bm,),
        in_specs=x_specs + [pl.BlockSpec((K, R), lambda m: (0, 0)), pl.BlockSpec((R, N), lambda m: (0, 0))]
        + e_specs,
        out_specs=pl.BlockSpec((bm, N), lambda m: (m, 0)),
        out_shape=jax.ShapeDtypeStruct((M, N), F32),
        compiler_params=pltpu.CompilerParams(dimension_semantics=("parallel",),
                                             vmem_limit_bytes=VMEM_LIMIT),
        name=name,
    )(*xs, w1, w2, *extras)


ATT_BLOCK = ATT_STEPS * max(d for _, d in GROUPS)
ATT_LANES = 4


def _attn_prompt_kernel(slopes_ref, *refs):
    ng = N_GROUPS
    q_refs, kc_refs, vc_refs = refs[0:ng], refs[ng:2 * ng], refs[2 * ng:3 * ng]
    kp_refs, vp_refs = refs[3 * ng:4 * ng], refs[4 * ng:5 * ng]
    o_ref = refs[5 * ng]
    m_scr, l_scr, acc_scr = refs[5 * ng + 1:]
    hp = pl.program_id(1)
    j = pl.program_id(2)
    S = ATT_STEPS
    lane_lo = lax.broadcasted_iota(jnp.int32, (S, LANES), 1) < HD_B
    iq = lax.broadcasted_iota(jnp.int32, (S, 2 * S), 0)
    jk = lax.broadcasted_iota(jnp.int32, (S, 2 * S), 1)
    steps = iq + S - jk
    mask = jnp.where((steps >= 0) & (steps <= S), 0.0, -jnp.inf)
    scale = HD_B ** -0.5
    no_prev = jnp.where(j > 0, 0.0, -jnp.inf)

    for g, (_, dil) in enumerate(GROUPS):
        biases = [mask - slopes_ref[g * H_B + 2 * hp + h] * (steps * dil).astype(F32) for h in range(2)]
        q_ref, kc_ref, vc_ref, kp_ref, vp_ref = q_refs[g], kc_refs[g], vc_refs[g], kp_refs[g], vp_refs[g]

        def attend(cur_start, prev_ref, prev_start, prev_bias, dil=dil, g=g, biases=biases, q_ref=q_ref,
                   kc_ref=kc_ref, vc_ref=vc_ref):
            cur = pl.ds(cur_start, S, stride=dil) if dil > 1 else pl.ds(cur_start, S)
            prv = pl.ds(prev_start, S, stride=dil) if dil > 1 else pl.ds(prev_start, S)
            q, kc, vc = q_ref[0, cur, :], kc_ref[0, cur, :], vc_ref[0, cur, :]
            kp, vp = prev_ref[0][0, prv, :], prev_ref[1][0, prv, :]
            qs = [jnp.where(lane_lo, q, 0.0), jnp.where(lane_lo, 0.0, q)]
            scores = [(_bdot_nt(qh, kp), _bdot_nt(qh, kc)) for qh in qs]
            yield
            ms, ls, probs = [], [], []
            for h in range(2):
                sp = scores[h][0] * scale + biases[h][:, :S]
                if prev_bias is not None:
                    sp = sp + prev_bias
                sc = scores[h][1] * scale + biases[h][:, S:]
                m = jnp.max(jnp.maximum(sp, sc), axis=-1, keepdims=True)
                pp = jnp.exp(sp - m)
                pc = jnp.exp(sc - m)
                ls.append(jnp.sum(pp + pc, axis=-1, keepdims=True))
                probs.append((pp, pc))
                ms.append(m)
            accs = [_bdot(pp, vp) + _bdot(pc, vc) for pp, pc in probs]
            yield
            m = jnp.where(lane_lo, ms[0], ms[1])
            l = jnp.where(lane_lo, ls[0], ls[1])
            acc = jnp.where(lane_lo, accs[0], accs[1])
            if g > 0:
                m_old = m_scr[cur, :]
                m_new = jnp.maximum(m_old, m)
                w_old, w_new = jnp.exp(m_old - m_new), jnp.exp(m - m_new)
                l = l_scr[cur, :] * w_old + l * w_new
                acc = acc_scr[cur, :] * w_old + acc * w_new
                m = m_new
            m_scr[cur, :] = m
            l_scr[cur, :] = l
            acc_scr[cur, :] = acc

        sd = S * dil
        n_sub = ATT_BLOCK // sd
        run = lambda items, attend=attend: _lockstep([attend(*it) for it in items])
        first = lambda c, kp_ref=kp_ref, vp_ref=vp_ref: (c, (kp_ref, vp_ref), c, no_prev)
        later = lambda c, i, sd=sd, kc_ref=kc_ref, vc_ref=vc_ref: (
            i * sd + c, (kc_ref, vc_ref), (i - 1) * sd + c, None)

        def loop(lo, hi, fn):
            def body(idx, carry):
                fn(idx)
                return carry
            lax.fori_loop(lo, hi, body, 0)

        W = ATT_LANES if dil == 1 else min(ATT_LANES, dil)
        if dil == 1:
            hb = n_sub // W
            run([first(0)] + [later(0, k * hb) for k in range(1, W)])
            loop(1, hb, lambda i, hb=hb: run([later(0, i + k * hb) for k in range(W)]))
        else:
            cs = dil // W
            def streams(c, cs=cs, n_sub=n_sub):
                run([first(c + k * cs) for k in range(W)])
                if n_sub > 1:
                    loop(1, n_sub, lambda i: run([later(c + k * cs, i) for k in range(W)]))
            if cs == 1:
                streams(0)
            else:
                loop(0, cs, streams)

    o_ref[0] = (acc_scr[...] / l_scr[...]).astype(o_ref.dtype)


def _attn_prompt_call(qkv, slopes):
    B, T, W = qkv.shape
    hw = H_B * HD_B
    n_hp = H_B // 2
    TB = ATT_BLOCK
    assert T % TB == 0
    col = lambda g, which, hp: (g * 3 * hw + which * hw) // LANES + hp
    cur = lambda g, which: pl.BlockSpec((1, TB, LANES), lambda b, hp, j: (b, j, col(g, which, hp)))
    def prev(g, which):
        rows = ATT_STEPS * GROUPS[g][1]
        per = TB // rows
        return pl.BlockSpec((1, rows, LANES),
                            lambda b, hp, j: (b, jnp.maximum(j * per - 1, 0), col(g, which, hp)))
    gs = range(N_GROUPS)
    in_specs = ([pl.BlockSpec(memory_space=pltpu.SMEM)]
                + [cur(g, 0) for g in gs] + [cur(g, 1) for g in gs] + [cur(g, 2) for g in gs]
                + [prev(g, 1) for g in gs] + [prev(g, 2) for g in gs])
    return pl.pallas_call(
        _attn_prompt_kernel,
        grid=(B, n_hp, T // TB),
        in_specs=in_specs,
        out_specs=pl.BlockSpec((1, TB, LANES), lambda b, hp, j: (b, j, hp)),
        out_shape=jax.ShapeDtypeStruct((B, T, hw), BF16),
        scratch_shapes=[pltpu.VMEM((TB, LANES), F32)] * 3,
        compiler_params=pltpu.CompilerParams(
            dimension_semantics=("parallel", "parallel", "arbitrary"), vmem_limit_bytes=VMEM_LIMIT),
        name="attn_prompt",
    )(slopes, *([qkv] * (5 * N_GROUPS)))


def _kv_window_kernel(x_ref, o_ref):
    for hp in range(H_B // 2):
        sl = slice(hp * LANES, (hp + 1) * LANES)
        o_ref[0, 0, sl, :] = x_ref[0, :, sl].T


def _kv_window_call(qkv, g, window):
    B, T, _ = qkv.shape
    hw = H_B * HD_B
    wb = min(window, 512)
    first = (T - window) // wb
    out = pl.pallas_call(
        _kv_window_kernel,
        grid=(B, 2, window // wb),
        in_specs=[pl.BlockSpec((1, wb, hw), lambda b, kv, i: (b, first + i, 3 * g + 1 + kv))],
        out_specs=pl.BlockSpec((1, 1, hw, wb), lambda b, kv, i: (b, kv, 0, i)),
        out_shape=jax.ShapeDtypeStruct((B, 2, hw, window), F32),
        compiler_params=pltpu.CompilerParams(dimension_semantics=("parallel", "parallel", "parallel")),
        name=f"kv_window_g{g}",
    )(qkv)
    return out.reshape(B, 2, H_B, HD_B, window).transpose(0, 4, 1, 2, 3)


KV_ROLL_ROWS = 256
KV_ROLL_BYTES = 2 * 1024 * 1024


def _kv_roll_kernel(c_ref, n_ref, o_ref):
    W = c_ref.shape[-1]
    S, R = n_ref.shape[1], n_ref.shape[2]
    rolled = pltpu.roll(c_ref[0, 0], W - S, 1)
    new_t = jnp.concatenate([jnp.zeros((LANES - S, R), F32), n_ref[0]], axis=0).T
    if W > LANES:
        o_ref[0, 0, :, :W - LANES] = rolled[:, :W - LANES]
    lane = lax.broadcasted_iota(jnp.int32, (R, LANES), 1)
    o_ref[0, 0, :, W - LANES:] = jnp.where(lane >= LANES - S, new_t, rolled[:, W - LANES:])


def _kv_roll_call(cache, qkv, g):
    B, window = cache.shape[:2]
    S = qkv.shape[1]
    hw = H_B * HD_B
    R = min(hw, max(KV_ROLL_ROWS, KV_ROLL_BYTES // (4 * window)))
    cache_t = cache.transpose(0, 2, 3, 4, 1).reshape(B, 2, hw, window)
    blk = pl.BlockSpec((1, 1, R, window), lambda b, kv, r: (b, kv, r, 0))
    out = pl.pallas_call(
        _kv_roll_kernel,
        grid=(B, 2, hw // R),
        in_specs=[blk, pl.BlockSpec((1, S, R), lambda b, kv, r: (b, 0, (3 * g + 1 + kv) * (hw // R) + r))],
        out_specs=blk,
        out_shape=jax.ShapeDtypeStruct((B, 2, hw, window), F32),
        compiler_params=pltpu.CompilerParams(dimension_semantics=("parallel", "parallel", "parallel")),
        name=f"kv_roll_g{g}",
    )(cache_t, qkv)
    return out.reshape(B, 2, H_B, HD_B, window).transpose(0, 4, 1, 2, 3)


def _attn_decode_kernel(slopes_ref, *refs):
    ng = N_GROUPS
    q_refs, k_refs, v_refs = refs[0:ng], refs[ng:2 * ng], refs[2 * ng:3 * ng]
    kt_refs, vt_refs = refs[3 * ng:4 * ng], refs[4 * ng:5 * ng]
    o_ref = refs[5 * ng]
    hp = pl.program_id(1)
    S = q_refs[0].shape[1]
    lane_lo = lax.broadcasted_iota(jnp.int32, (S, LANES), 1) < HD_B
    scale = HD_B ** -0.5

    def chain(g, h):
        window, dil = GROUPS[g]
        slope = slopes_ref[g * H_B + 2 * hp + h]
        q = q_refs[g][0]
        qh = jnp.where(lane_lo, q, 0.0) if h == 0 else jnp.where(lane_lo, 0.0, q)
        s_old = _bdot(qh, kt_refs[g][0, 0])
        s_new = _bdot_nt(qh, k_refs[g][0])
        yield
        j_o = lax.broadcasted_iota(jnp.int32, (S, window), 0)
        w_o = lax.broadcasted_iota(jnp.int32, (S, window), 1)
        dist_o = window + j_o - w_o
        ok_o = ((dist_o & (dil - 1)) == 0) & (dist_o <= window)
        j_n = lax.broadcasted_iota(jnp.int32, (S, S), 0)
        dist_n = j_n - lax.broadcasted_iota(jnp.int32, (S, S), 1)
        ok_n = ((dist_n & (dil - 1)) == 0) & (dist_n >= 0)
        s_old = jnp.where(ok_o, s_old * scale - slope * dist_o.astype(F32), -jnp.inf)
        s_new = jnp.where(ok_n, s_new * scale - slope * dist_n.astype(F32), -jnp.inf)
        m = jnp.maximum(jnp.max(s_old, axis=-1, keepdims=True), jnp.max(s_new, axis=-1, keepdims=True))
        p_old = jnp.exp(s_old - m)
        p_new = jnp.exp(s_new - m)
        l = jnp.sum(p_old, axis=-1, keepdims=True) + jnp.sum(p_new, axis=-1, keepdims=True)
        acc = _bdot_nt(p_old, vt_refs[g][0, 0]) + _bdot(p_new, v_refs[g][0])
        yield
        return m, l, acc

    res = _lockstep([chain(g, h) for g in range(ng) for h in range(2)])
    m = l = acc = None
    for g in range(ng):
        (m0, l0, a0), (m1, l1, a1) = res[2 * g], res[2 * g + 1]
        mg = jnp.where(lane_lo, m0, m1)
        lg = jnp.where(lane_lo, l0, l1)
        ag = jnp.where(lane_lo, a0, a1)
        if g == 0:
            m, l, acc = mg, lg, ag
        else:
            m_new = jnp.maximum(m, mg)
            w_old, w_new = jnp.exp(m - m_new), jnp.exp(mg - m_new)
            l = l * w_old + lg * w_new
            acc = acc * w_old + ag * w_new
            m = m_new
    o_ref[0] = (acc / l).astype(o_ref.dtype)


def _attn_decode_call(qkv, caches, slopes):
    B, S, _ = qkv.shape
    hw = H_B * HD_B
    n_hp = H_B // 2
    assert all(w >= S and d & (d - 1) == 0 for w, d in GROUPS)
    cache_t = [c.transpose(0, 2, 3, 4, 1).reshape(B, 2, hw, c.shape[1]) for c in caches]
    col = lambda g, which, hp: (g * 3 * hw + which * hw) // LANES + hp
    new = lambda g, which: pl.BlockSpec((1, S, LANES), lambda b, hp: (b, 0, col(g, which, hp)))
    old = lambda g, kv: pl.BlockSpec((1, 1, LANES, GROUPS[g][0]), lambda b, hp: (b, kv, hp, 0))
    gs = range(N_GROUPS)
    in_specs = ([pl.BlockSpec(memory_space=pltpu.SMEM)]
                + [new(g, 0) for g in gs] + [new(g, 1) for g in gs] + [new(g, 2) for g in gs]
                + [old(g, 0) for g in gs] + [old(g, 1) for g in gs])
    return pl.pallas_call(
        _attn_decode_kernel,
        grid=(B, n_hp),
        in_specs=in_specs,
        out_specs=pl.BlockSpec((1, S, LANES), lambda b, hp: (b, 0, hp)),
        out_shape=jax.ShapeDtypeStruct((B, S, hw), BF16),
        compiler_params=pltpu.CompilerParams(dimension_semantics=("parallel", "parallel")),
        name="attn_decode",
    )(slopes, *([qkv] * (3 * N_GROUPS)), *cache_t, *cache_t)


def _alibi_slopes():
    n = N_GROUPS * H_B
    return 2.0 ** (-8.0 * jnp.arange(1, n + 1, dtype=F32) / n)


def _pair_states(s):
    B, H = s.shape[:2]
    s = s.reshape(B, H // 2, 2, HEAD_A, HEAD_A)
    z = jnp.zeros_like(s[:, :, 0])
    top = jnp.concatenate([s[:, :, 0], z], axis=-1)
    bot = jnp.concatenate([z, s[:, :, 1]], axis=-1)
    return jnp.concatenate([top, bot], axis=-2)


def _unpair_states(s2):
    B, P = s2.shape[:2]
    return jnp.stack([s2[:, :, :HEAD_A, :HEAD_A], s2[:, :, HEAD_A:, HEAD_A:]],
                     axis=2).reshape(B, 2 * P, HEAD_A, HEAD_A)


def _norm_mix_kernel(x_ref, g_ref, mu_ref, shift_ref, *refs, tiles_per_seq):
    out_refs, hlast_ref, carry_scr = refs[:-2], refs[-2], refs[-1]
    @pl.when(pl.program_id(0) == 0)
    def _():
        carry_scr[...] = jnp.zeros_like(carry_scr)

    x = x_ref[0]
    h = x * lax.rsqrt(jnp.mean(x * x, axis=-1, keepdims=True) + RMS_EPS) * g_ref[...]
    rows = h.shape[0]
    first_tile = pl.program_id(0) % tiles_per_seq == 0
    row0 = jnp.where(first_tile, shift_ref[0], carry_scr[...])
    prev = pltpu.roll(h, 1, 0) if rows > 1 else h
    prev = jnp.where(lax.broadcasted_iota(jnp.int32, h.shape, 0) == 0, row0, prev)
    carry_scr[...] = h[rows - 1:rows, :]
    hlast_ref[0] = h[rows - 1:rows, :]
    diff = prev - h
    for i, o_ref in enumerate(out_refs):
        o_ref[0] = (h + diff * mu_ref[i:i + 1, :]).astype(o_ref.dtype)


def _norm_mix_call(x, g, mu, shift0, bt=512):
    B, T, D = x.shape
    bt = min(bt, T)
    assert T % bt == 0
    n_mix = mu.shape[0]
    tiles = T // bt
    seq = pl.BlockSpec((1, bt, D), lambda m: (m // tiles, m % tiles, 0))
    per_seq = pl.BlockSpec((1, 1, D), lambda m: (m // tiles, 0, 0))
    outs = pl.pallas_call(
        functools.partial(_norm_mix_kernel, tiles_per_seq=tiles),
        grid=(B * tiles,),
        in_specs=[seq, pl.BlockSpec((1, D), lambda m: (0, 0)), pl.BlockSpec((n_mix, D), lambda m: (0, 0)),
                  per_seq],
        out_specs=[seq] * n_mix + [per_seq],
        out_shape=[jax.ShapeDtypeStruct((B, T, D), BF16)] * n_mix + [jax.ShapeDtypeStruct((B, 1, D), F32)],
        scratch_shapes=[pltpu.VMEM((1, D), F32)],
        compiler_params=pltpu.CompilerParams(dimension_semantics=("arbitrary",),
                                             vmem_limit_bytes=VMEM_LIMIT),
        name="rwkv_norm_mix",
    )(x, g.reshape(1, D), mu, shift0.reshape(B, 1, D))
    return [o.reshape(B * T, D) for o in outs[:n_mix]], outs[n_mix].reshape(B, D)


def _rwkv_block(x, shift0, wkv0, norm_g, p):
    (mu, w0, w1, w2, a0, a1, a2, g1, g2, k_k, k_a, r_k, lnx_w, lnx_b, w_r, w_k, w_v, w_o) = p
    B, T, D = x.shape
    M = B * T
    x2 = x.reshape(M, D)
    (xr, xw, xk, xv, xa, xg), h_last = _norm_mix_call(x, norm_g, mu, shift0)
    big = dict(bm=1024, bn=1024)
    r = _matmul_call([xr], w_r, name="rwkv_r", **big)
    k = _matmul_call([xk], w_k, name="rwkv_k", **big)
    v = _matmul_call([xv], w_v, name="rwkv_v", **big)
    wlog = _lora_call([xw], w1, w2, [w0.reshape(1, D)], mid=jnp.tanh,
                      post=lambda z, b: -jax.nn.softplus(-(b + z)) - 0.5, name="rwkv_w")
    a = _lora_call([xa], a1, a2, [a0.reshape(1, D)],
                   post=lambda z, b: jax.nn.sigmoid(b + z), name="rwkv_a")
    g = _lora_call([xg], g1, g2, mid=jax.nn.sigmoid, name="rwkv_g")

    L = WKV_CHUNK
    Tp = -(-T // L) * L
    def seq(z, fill=0.0):
        z = z.reshape(B, T, D)
        if Tp != T:
            z = jnp.pad(z, ((0, 0), (0, Tp - T), (0, 0)), constant_values=fill)
        return z
    y, s2 = _wkv_call(seq(r), seq(k), seq(v), seq(wlog, -jnp.inf), seq(a), seq(g),
                      k_k, k_a, r_k.reshape(D), lnx_w, lnx_b, _pair_states(wkv0), seqs_per_step=1, pairs_per_step=D // LANES)
    y2 = y[:, :T].reshape(M, D)
    out = _matmul_call([y2], w_o, [x2], post=lambda acc, res: res + acc, name="rwkv_o", **big)
    return out.reshape(B, T, D), _unpair_states(s2), h_last


def _ffn_block(x2, norm_g, w_up, w_down, layer):
    mid = _matmul_rows_call([x2, norm_g.reshape(1, -1)], w_up, pre=_rmsnorm,
                            post=lambda acc: jnp.square(jnp.maximum(acc, 0.0)),
                            out_dtype=BF16, bm=1024, bn=1024, layer=layer, name="ffn_up")
    return _matmul_call([mid], w_down, [x2], post=lambda acc, res: res + acc,
                        bm=1024, bn=1024, bk=2048, layer=layer, name="ffn_down")


def _attn_block(x, norm_g, w_qkv, w_o, slopes, caches=None):
    B, T, D = x.shape
    M = B * T
    x2 = x.reshape(M, D)
    qkv = _matmul_rows_call([x2, norm_g.reshape(1, -1)], w_qkv, pre=_rmsnorm,
                            bm=1024, bn=1024, name="attn_qkv").reshape(B, T, -1)
    hw = H_B * HD_B
    if caches is None:
        merged = _attn_prompt_call(qkv, slopes)
        bufs = [_kv_window_call(qkv, g, min(w, T)) for g, (w, _) in enumerate(GROUPS)]
    else:
        merged = _attn_decode_call(qkv, caches, slopes)
        bufs = [_kv_roll_call(caches[g], qkv, g) for g in range(N_GROUPS)]
    out = _matmul_call([merged.reshape(M, hw)], w_o, [x2], post=lambda acc, res: res + acc,
                       bm=1024, name="attn_o")
    return out.reshape(B, T, D), bufs


def kernel(x_prompt, x_sample, state_wkv, state_shift, cache_kv_g1, cache_kv_g2, cache_kv_g3,
           norm_mix, norm_ffn, norm_final,
           rwkv_mu, rwkv_w0, rwkv_w1, rwkv_w2, rwkv_a0, rwkv_a1, rwkv_a2, rwkv_g1, rwkv_g2,
           rwkv_k_k, rwkv_k_a, rwkv_r_k, rwkv_lnx_w, rwkv_lnx_b, rwkv_w_r, rwkv_w_k, rwkv_w_v, rwkv_w_o,
           attn_w_qkv, attn_w_o, ffn_w_up, ffn_w_down):
    rwkv_params = (rwkv_mu, rwkv_w0, rwkv_w1, rwkv_w2, rwkv_a0, rwkv_a1, rwkv_a2, rwkv_g1, rwkv_g2,
                   rwkv_k_k, rwkv_k_a, rwkv_r_k, rwkv_lnx_w, rwkv_lnx_b,
                   rwkv_w_r, rwkv_w_k, rwkv_w_v, rwkv_w_o)
    slopes = _alibi_slopes()
    Bp, Tp, D = x_prompt.shape
    Bs, Ts, _ = x_sample.shape

    def ffn(x, i):
        B, T, _ = x.shape
        return _ffn_block(x.reshape(B * T, D), norm_ffn[i], ffn_w_up, ffn_w_down, i).reshape(B, T, D)

    xp, wkv_p, shift_p = _rwkv_block(x_prompt, jnp.zeros((Bp, D), F32),
                                     jnp.zeros((Bp, H_A, HEAD_A, HEAD_A), F32), norm_mix[0], rwkv_params)
    xs, wkv_s, shift_s = _rwkv_block(x_sample, state_shift, state_wkv, norm_mix[0], rwkv_params)
    xp, xs = ffn(xp, 0), ffn(xs, 0)
    xp, (kv1_p, kv2_p, kv3_p) = _attn_block(xp, norm_mix[1], attn_w_qkv, attn_w_o, slopes)
    xs, (kv1_s, kv2_s, kv3_s) = _attn_block(xs, norm_mix[1], attn_w_qkv, attn_w_o, slopes,
                                            (cache_kv_g1, cache_kv_g2, cache_kv_g3))
    xp, xs = ffn(xp, 1), ffn(xs, 1)
    y_prompt = _rmsnorm_call(xp.reshape(Bp * Tp, D), norm_final).reshape(Bp, Tp, D)
    y_sample = _rmsnorm_call(xs.reshape(Bs * Ts, D), norm_final).reshape(Bs, Ts, D)
    return (y_prompt, y_sample, wkv_p, shift_p, kv1_p, kv2_p, kv3_p,
            wkv_s, shift_s, kv1_s, kv2_s, kv3_s)
```

```python
import functools

import jax
import jax.numpy as jnp
from jax import lax
from jax.experimental import pallas as pl
from jax.experimental.pallas import tpu as pltpu

F32 = jnp.float32
BF16 = jnp.bfloat16

D_MODEL = 2048
HEAD_A = 64
H_A = D_MODEL // HEAD_A
GN_EPS = 64e-5
GROUPS = ((128, 1), (512, 4), (2048, 16))
N_GROUPS = len(GROUPS)
H_B = 16
HD_B = 64
ATT_STEPS = 128
RMS_EPS = 1e-6

LANES = 128
VMEM_LIMIT = 56 * 1024 * 1024


def _bdot(a, b):
    return jnp.dot(a.astype(BF16), b.astype(BF16), preferred_element_type=F32)


def _bdot_nt(a, b):
    return lax.dot_general(a.astype(BF16), b.astype(BF16), (((1,), (1,)), ((), ())),
                           preferred_element_type=F32)


def _bdot_tn(a, b):
    return lax.dot_general(a.astype(BF16), b.astype(BF16), (((0,), (0,)), ((), ())),
                           preferred_element_type=F32)


def _exact_dot(m, x):
    hi = x.astype(BF16)
    r1 = x - hi.astype(F32)
    mid = r1.astype(BF16)
    lo = (r1 - mid.astype(F32)).astype(BF16)
    n = x.shape[1]
    parts = jnp.dot(m.astype(BF16), jnp.concatenate([hi, mid, lo], axis=1), preferred_element_type=F32)
    return (parts[:, :n] + parts[:, n:2 * n]) + parts[:, 2 * n:]


def _lockstep(chains):
    results = [None] * len(chains)
    live = list(range(len(chains)))
    while live:
        for idx in list(live):
            try:
                next(chains[idx])
            except StopIteration as done:
                results[idx] = done.value
                live.remove(idx)
    return results


def _head_sum(x, lane_lo):
    s0 = jnp.sum(jnp.where(lane_lo, x, 0.0), axis=-1, keepdims=True)
    s1 = jnp.sum(jnp.where(lane_lo, 0.0, x), axis=-1, keepdims=True)
    return jnp.where(lane_lo, s0, s1)


WKV_CHUNK = HEAD_A
(_MK_EYE, _MK_LVL0, _MK_BD, _MK_NMASK, _MK_CAUSAL, _MK_TRI, _MK_LEVELS) = range(7)
_N_LEVELS = 5
_N_MASKS = _MK_LEVELS + _N_LEVELS


def _wkv_fill_masks(mk_ref):
    L = WKV_CHUNK
    r2 = lax.broadcasted_iota(jnp.int32, (LANES, LANES), 0)
    c2 = lax.broadcasted_iota(jnp.int32, (LANES, LANES), 1)
    f = lambda m: m.astype(F32)
    same_block = (r2 < L) == (c2 < L)
    strict = (c2 % L) < (r2 % L)
    incl = (c2 % L) <= (r2 % L)
    mk_ref[_MK_EYE] = f(r2 == c2)
    mk_ref[_MK_LVL0] = f((r2 // 2 == c2 // 2) & (c2 < r2))
    mk_ref[_MK_BD] = f(same_block)
    mk_ref[_MK_NMASK] = f(strict & same_block)
    mk_ref[_MK_CAUSAL] = f(strict | (incl & (r2 >= L)))
    mk_ref[_MK_TRI] = f(incl & (r2 < L) & (c2 < L))
    size = 2
    for lvl in range(_N_LEVELS):
        mk_ref[_MK_LEVELS + lvl] = f((r2 // (2 * size) == c2 // (2 * size))
                                     & ((r2 // size) % 2 == 1) & ((c2 // size) % 2 == 0) & same_block)
        size *= 2


def _wkv_pair_chunk(r, kraw, v, wl, a, g, k_k, k_a, r_k, lnx_w, lnx_b, s2, mk_ref):
    L = WKV_CHUNK
    lane_lo = lax.broadcasted_iota(jnp.int32, (L, LANES), 1) < HEAD_A
    lo2 = lax.broadcasted_iota(jnp.int32, (2 * L, LANES), 1) < HEAD_A

    logd = -jnp.exp(wl)
    kkraw = kraw * k_k
    kk = kkraw * lax.rsqrt(jnp.maximum(_head_sum(kkraw * kkraw, lane_lo), 1e-24))
    k2 = kraw * (1.0 + (a - 1.0) * k_a)

    cum = _exact_dot(mk_ref[_MK_TRI][:L, :L], logd)
    w_incl = jnp.exp(cum)
    w_excl = jnp.exp(cum - logd)
    w_inv = jnp.exp(-cum)
    w_last = w_incl[L - 1:L, :]

    at = -kk * w_excl
    bt = (kk * a) * w_inv
    kt = k2 * w_inv
    rt = r * w_incl

    ar = jnp.concatenate([at, rt], axis=0)
    bk = jnp.concatenate([bt, kt], axis=0)
    kb = jnp.concatenate([kt, bt], axis=0)
    mm = _bdot_nt(jnp.concatenate([jnp.where(lo2, ar, 0.0), jnp.where(lo2, 0.0, ar)], axis=0),
                  jnp.concatenate([bk, kb], axis=0))
    m0 = mm[:2 * L, :2 * L]
    m1 = mm[2 * L:, 2 * L:]
    aprp = _bdot_nt(ar, s2)
    yield

    a0 = m0[:L]
    a1 = m1[:L]
    n2 = jnp.concatenate([a0, a1], axis=0) * mk_ref[_MK_NMASK]
    causal = mk_ref[_MK_CAUSAL]
    strict, incl = causal[:L], causal[L:]
    v_lo = jnp.where(lane_lo, v, 0.0)
    v_hi = jnp.where(lane_lo, 0.0, v)
    ak_cat = jnp.where(lane_lo, a1, a0) * strict

    t2 = mk_ref[_MK_EYE] + n2 * mk_ref[_MK_LVL0]
    for lvl in range(_N_LEVELS):
        tc = _bdot(t2, n2 * mk_ref[_MK_LEVELS + lvl])
        yield
        t2 = t2 + _bdot(tc, t2)
        yield
    tcat = t2[:L] + t2[L:]
    rhs = aprp[:L] + _bdot(ak_cat, jnp.concatenate([v_hi, v_lo], axis=0))
    yield

    u = _bdot(tcat, jnp.concatenate([jnp.where(lane_lo, rhs, 0.0), jnp.where(lane_lo, 0.0, rhs)], axis=0))
    yield
    u_lo = jnp.where(lane_lo, u, 0.0)
    u_hi = jnp.where(lane_lo, 0.0, u)

    y = aprp[L:] + _bdot(jnp.concatenate([m0[L:] * incl, m1[L:] * incl], axis=1),
                         jnp.concatenate([u_lo, v_lo, v_hi, u_hi], axis=0))
    delta = _bdot_tn(jnp.concatenate([u, v], axis=0), bk)
    yield
    s_new = (s2 + delta * mk_ref[_MK_BD]) * w_last

    mean = _head_sum(y, lane_lo) * (1.0 / HEAD_A)
    yc = y - mean
    var = _head_sum(yc * yc, lane_lo) * (1.0 / HEAD_A)
    yn = yc * lax.rsqrt(var + GN_EPS) * lnx_w + lnx_b
    bonus = _head_sum(r * k2 * r_k, lane_lo)
    return (yn + bonus * v) * g, s_new


def _wkv_kernel(r_ref, k_ref, v_ref, wl_ref, a_ref, g_ref, kk_ref, ka_ref, rk_ref, lw_ref, lb_ref,
                s0_ref, y_ref, sout_ref, s_scr, mk_scr, *, n_seqs, n_pairs):
    b, pb, c = pl.program_id(0), pl.program_id(1), pl.program_id(2)

    @pl.when((b == 0) & (pb == 0) & (c == 0))
    def _():
        _wkv_fill_masks(mk_scr)

    @pl.when(c == 0)
    def _():
        s_scr[...] = s0_ref[...]

    items = [(q, p, slice(p * LANES, (p + 1) * LANES)) for q in range(n_seqs) for p in range(n_pairs)]
    chains = [_wkv_pair_chunk(
        r_ref[q, :, sl], k_ref[q, :, sl], v_ref[q, :, sl], wl_ref[q, :, sl], a_ref[q, :, sl],
        g_ref[q, :, sl], kk_ref[:, sl], ka_ref[:, sl], rk_ref[:, sl], lw_ref[:, sl], lb_ref[:, sl],
        s_scr[q, p], mk_scr) for q, p, sl in items]
    for (y, s_new), (q, p, sl) in zip(_lockstep(chains), items):
        y_ref[q, :, sl] = y.astype(y_ref.dtype)
        s_scr[q, p] = s_new

    @pl.when(c == pl.num_programs(2) - 1)
    def _():
        sout_ref[...] = s_scr[...]


def _wkv_call(r, k, v, wlog, a, g, k_k, k_a, r_k, lnx_w, lnx_b, s2_0, *, seqs_per_step, pairs_per_step):
    B, T, D = r.shape
    L, P, Q = WKV_CHUNK, pairs_per_step, seqs_per_step
    assert T % L == 0 and 2 * L == LANES and (D // LANES) % P == 0 and B % Q == 0
    seq = pl.BlockSpec((Q, L, P * LANES), lambda b, p, c: (b, c, p))
    par = pl.BlockSpec((1, P * LANES), lambda b, p, c: (0, p))
    st = pl.BlockSpec((Q, P, LANES, LANES), lambda b, p, c: (b, p, 0, 0))
    row = lambda x: x.reshape(1, D)
    return pl.pallas_call(
        functools.partial(_wkv_kernel, n_seqs=Q, n_pairs=P),
        grid=(B // Q, D // LANES // P, T // L),
        in_specs=[seq] * 6 + [par] * 5 + [st],
        out_specs=[seq, st],
        out_shape=[jax.ShapeDtypeStruct((B, T, D), BF16),
                   jax.ShapeDtypeStruct((B, D // LANES, LANES, LANES), F32)],
        scratch_shapes=[pltpu.VMEM((Q, P, LANES, LANES), F32), pltpu.VMEM((_N_MASKS, LANES, LANES), F32)],
        compiler_params=pltpu.CompilerParams(
            dimension_semantics=("arbitrary", "arbitrary", "arbitrary")),
        name="wkv7_chunk_scan",
    )(r, k, v, wlog, a, g, row(k_k), row(k_a), row(r_k), row(lnx_w), row(lnx_b), s2_0)


def _rmsnorm(x, g):
    return x * lax.rsqrt(jnp.mean(x * x, axis=-1, keepdims=True) + RMS_EPS) * g


def _rmsnorm_kernel(x_ref, g_ref, o_ref):
    o_ref[...] = _rmsnorm(x_ref[...], g_ref[...]).astype(o_ref.dtype)


def _rmsnorm_call(x, g, out_dtype=F32):
    M, D = x.shape
    bm = min(M, 512)
    return pl.pallas_call(
        _rmsnorm_kernel,
        grid=(M // bm,),
        in_specs=[pl.BlockSpec((bm, D), lambda m: (m, 0)), pl.BlockSpec((1, D), lambda m: (0, 0))],
        out_specs=pl.BlockSpec((bm, D), lambda m: (m, 0)),
        out_shape=jax.ShapeDtypeStruct((M, D), out_dtype),
        compiler_params=pltpu.CompilerParams(dimension_semantics=("parallel",)),
        name="rmsnorm",
    )(x, g.reshape(1, D))


MM_ROW_CHUNK = 256


def _mm_kernel(*refs, n_x, n_e, pre, post, nk):
    x_refs = refs[:n_x]
    w_ref = refs[n_x]
    e_refs = refs[n_x + 1:n_x + 1 + n_e]
    o_ref = refs[n_x + 1 + n_e]
    bm = o_ref.shape[0]
    acc_ref = refs[-1] if nk > 1 else None
    k = pl.program_id(2)
    if nk > 1:
        @pl.when(k == 0)
        def _():
            acc_ref[...] = jnp.zeros_like(acc_ref)

    w = w_ref[...].astype(BF16)
    chunk = min(bm, MM_ROW_CHUNK)
    for r0 in range(0, bm, chunk):
        rows = slice(r0, r0 + chunk)
        tile = lambda ref: ref[rows, :] if ref.shape[0] == bm else ref[...]
        x = pre(*[tile(r) for r in x_refs])
        part = jnp.dot(x.astype(BF16), w, preferred_element_type=F32)
        if nk == 1:
            o_ref[rows, :] = post(part, *[tile(e) for e in e_refs]).astype(o_ref.dtype)
        else:
            acc_ref[rows, :] += part

    if nk > 1:
        @pl.when(k == nk - 1)
        def _():
            o_ref[...] = post(acc_ref[...], *[e[...] for e in e_refs]).astype(o_ref.dtype)


def _mm_rows_kernel(*refs, n_x, n_e, pre, post):
    x_refs = refs[:n_x]
    w_ref = refs[n_x]
    e_refs = refs[n_x + 1:n_x + 1 + n_e]
    o_ref = refs[n_x + 1 + n_e]
    x_scr = refs[-1]

    @pl.when(pl.program_id(1) == 0)
    def _():
        x_scr[...] = pre(*[r[...] for r in x_refs]).astype(BF16)

    bm = o_ref.shape[0]
    w = w_ref[...].astype(BF16)
    chunk = min(bm, MM_ROW_CHUNK)
    for r0 in range(0, bm, chunk):
        rows = slice(r0, r0 + chunk)
        tile = lambda ref: ref[rows, :] if ref.shape[0] == bm else ref[...]
        part = jnp.dot(x_scr[rows, :], w, preferred_element_type=F32)
        o_ref[rows, :] = post(part, *[tile(e) for e in e_refs]).astype(o_ref.dtype)


def _identity(x):
    return x


def _matmul_rows_call(xs, w, extras=(), *, pre=_identity, post=_identity, out_dtype=F32,
                      bm=512, bn=512, layer=None, name="matmul"):
    K, N = w.shape[-2:]
    M = max(x.shape[0] for x in xs)
    bm = min(bm, M)
    bn = min(bn, N)
    assert M % bm == 0 and N % bn == 0
    x_specs = [pl.BlockSpec((bm, K), lambda m, n: (m, 0)) if x.shape[0] == M
               else pl.BlockSpec((x.shape[0], K), lambda m, n: (0, 0)) for x in xs]
    e_specs = [pl.BlockSpec((bm, bn), lambda m, n: (m, n)) if e.shape[0] == M
               else pl.BlockSpec((1, bn), lambda m, n: (0, n)) for e in extras]
    if layer is None:
        w_spec = pl.BlockSpec((K, bn), lambda m, n: (0, n))
    else:
        w_spec = pl.BlockSpec((None, K, bn), lambda m, n: (layer, 0, n))
    return pl.pallas_call(
        functools.partial(_mm_rows_kernel, n_x=len(xs), n_e=len(extras), pre=pre, post=post),
        grid=(M // bm, N // bn),
        in_specs=x_specs + [w_spec] + e_specs,
        out_specs=pl.BlockSpec((bm, bn), lambda m, n: (m, n)),
        out_shape=jax.ShapeDtypeStruct((M, N), out_dtype),
        scratch_shapes=[pltpu.VMEM((bm, K), BF16)],
        compiler_params=pltpu.CompilerParams(
            dimension_semantics=("parallel", "arbitrary"), vmem_limit_bytes=VMEM_LIMIT),
        name=name,
    )(*xs, w, *extras)


def _matmul_call(xs, w, extras=(), *, pre=_identity, post=_identity, out_dtype=F32,
                 bm=512, bn=512, bk=None, layer=None, name="matmul"):
    K, N = w.shape[-2:]
    M = max(x.shape[0] for x in xs)
    bm = min(bm, M)
    bn = min(bn, N)
    bk = K if bk is None else min(bk, K)
    assert M % bm == 0 and N % bn == 0 and K % bk == 0
    nk = K // bk
    x_specs = [pl.BlockSpec((bm, bk), lambda n, m, k: (m, k)) if x.shape[0] == M
               else pl.BlockSpec((x.shape[0], bk), lambda n, m, k: (0, k)) for x in xs]
    e_specs = [pl.BlockSpec((bm, bn), lambda n, m, k: (m, n)) if e.shape[0] == M
               else pl.BlockSpec((1, bn), lambda n, m, k: (0, n)) for e in extras]
    if layer is None:
        w_spec = pl.BlockSpec((bk, bn), lambda n, m, k: (k, n))
    else:
        w_spec = pl.BlockSpec((None, bk, bn), lambda n, m, k: (layer, k, n))
    return pl.pallas_call(
        functools.partial(_mm_kernel, n_x=len(xs), n_e=len(extras), pre=pre, post=post, nk=nk),
        grid=(N // bn, M // bm, nk),
        in_specs=x_specs + [w_spec] + e_specs,
        out_specs=pl.BlockSpec((bm, bn), lambda n, m, k: (m, n)),
        out_shape=jax.ShapeDtypeStruct((M, N), out_dtype),
        scratch_shapes=[pltpu.VMEM((bm, bn), F32)] if nk > 1 else [],
        compiler_params=pltpu.CompilerParams(
            dimension_semantics=("parallel", "parallel", "arbitrary"),
            vmem_limit_bytes=VMEM_LIMIT),
        name=name,
    )(*xs, w, *extras)


def _lora_kernel(*refs, n_x, n_e, pre, mid, post):
    x_refs = refs[:n_x]
    w1_ref, w2_ref = refs[n_x], refs[n_x + 1]
    e_refs = refs[n_x + 2:n_x + 2 + n_e]
    o_ref = refs[n_x + 2 + n_e]
    x = pre(*[r[...] for r in x_refs])
    z = mid(_bdot(x, w1_ref[...]))
    o_ref[...] = post(_bdot(z, w2_ref[...]), *[e[...] for e in e_refs]).astype(o_ref.dtype)


def _lora_call(xs, w1, w2, extras=(), *, pre=_identity, mid=_identity, post=_identity, bm=512, name="lora"):
    K, R = w1.shape
    N = w2.shape[1]
    M = max(x.shape[0] for x in xs)
    bm = min(bm, M)
    x_specs = [pl.BlockSpec((bm, K), lambda m: (m, 0)) if x.shape[0] == M
               else pl.BlockSpec((x.shape[0], K), lambda m: (0, 0)) for x in xs]
    e_specs = [pl.BlockSpec((bm, N), lambda m: (m, 0)) if e.shape[0] == M
               else pl.BlockSpec((1, N), lambda m: (0, 0)) for e in extras]
    return pl.pallas_call(
        functools.partial(_lora_kernel, n_x=len(xs), n_e=len(extras), pre=pre, mid=mid, post=post),
        grid=(M // bm,),
        in_specs=x_specs + [pl.BlockSpec((K, R), lambda m: (0, 0)), pl.BlockSpec((R, N), lambda m: (0, 0))]
        + e_specs,
        out_specs=pl.BlockSpec((bm, N), lambda m: (m, 0)),
        out_shape=jax.ShapeDtypeStruct((M, N), F32),
        compiler_params=pltpu.CompilerParams(dimension_semantics=("parallel",),
                                             vmem_limit_bytes=VMEM_LIMIT),
        name=name,
    )(*xs, w1, w2, *extras)


ATT_BLOCK = ATT_STEPS * max(d for _, d in GROUPS)
ATT_LANES = 4


def _attn_prompt_kernel(slopes_ref, *refs):
    ng = N_GROUPS
    q_refs, kc_refs, vc_refs = refs[0:ng], refs[ng:2 * ng], refs[2 * ng:3 * ng]
    kp_refs, vp_refs = refs[3 * ng:4 * ng], refs[4 * ng:5 * ng]
    o_ref = refs[5 * ng]
    m_scr, l_scr, acc_scr = refs[5 * ng + 1:]
    hp = pl.program_id(1)
    j = pl.program_id(2)
    S = ATT_STEPS
    lane_lo = lax.broadcasted_iota(jnp.int32, (S, LANES), 1) < HD_B
    iq = lax.broadcasted_iota(jnp.int32, (S, 2 * S), 0)
    jk = lax.broadcasted_iota(jnp.int32, (S, 2 * S), 1)
    steps = iq + S - jk
    mask = jnp.where((steps >= 0) & (steps <= S), 0.0, -jnp.inf)
    scale = HD_B ** -0.5
    no_prev = jnp.where(j > 0, 0.0, -jnp.inf)

    order = sorted(range(ng), key=lambda g: -GROUPS[g][1])
    for g in order:
        dil = GROUPS[g][1]
        biases = [mask - slopes_ref[g * H_B + 2 * hp + h] * (steps * dil).astype(F32) for h in range(2)]
        q_ref, kc_ref, vc_ref, kp_ref, vp_ref = q_refs[g], kc_refs[g], vc_refs[g], kp_refs[g], vp_refs[g]

        def attend(cur_start, prev_ref, prev_start, prev_bias, dil=dil, g=g, biases=biases, q_ref=q_ref,
                   kc_ref=kc_ref, vc_ref=vc_ref):
            cur = pl.ds(cur_start, S, stride=dil) if dil > 1 else pl.ds(cur_start, S)
            prv = pl.ds(prev_start, S, stride=dil) if dil > 1 else pl.ds(prev_start, S)
            q, kc, vc = q_ref[0, cur, :], kc_ref[0, cur, :], vc_ref[0, cur, :]
            kp, vp = prev_ref[0][0, prv, :], prev_ref[1][0, prv, :]
            qs = [jnp.where(lane_lo, q, 0.0), jnp.where(lane_lo, 0.0, q)]
            scores = [(_bdot_nt(qh, kp), _bdot_nt(qh, kc)) for qh in qs]
            yield
            ms, ls, probs = [], [], []
            for h in range(2):
                sp = scores[h][0] * scale + biases[h][:, :S]
                if prev_bias is not None:
                    sp = sp + prev_bias
                sc = scores[h][1] * scale + biases[h][:, S:]
                m = jnp.max(jnp.maximum(sp, sc), axis=-1, keepdims=True)
                pp = jnp.exp(sp - m)
                pc = jnp.exp(sc - m)
                ls.append(jnp.sum(pp + pc, axis=-1, keepdims=True))
                probs.append((pp, pc))
                ms.append(m)
            accs = [_bdot(pp, vp) + _bdot(pc, vc) for pp, pc in probs]
            yield
            m = jnp.where(lane_lo, ms[0], ms[1])
            l = jnp.where(lane_lo, ls[0], ls[1])
            acc = jnp.where(lane_lo, accs[0], accs[1])
            if g != order[0]:
                m_old = m_scr[cur, :]
                m_new = jnp.maximum(m_old, m)
                w_old, w_new = jnp.exp(m_old - m_new), jnp.exp(m - m_new)
                l = l_scr[cur, :] * w_old + l * w_new
                acc = acc_scr[cur, :] * w_old + acc * w_new
                m = m_new
            m_scr[cur, :] = m
            l_scr[cur, :] = l
            acc_scr[cur, :] = acc

        sd = S * dil
        n_sub = ATT_BLOCK // sd
        run = lambda items, attend=attend: _lockstep([attend(*it) for it in items])
        first = lambda c, kp_ref=kp_ref, vp_ref=vp_ref: (c, (kp_ref, vp_ref), c, no_prev)
        later = lambda c, i, sd=sd, kc_ref=kc_ref, vc_ref=vc_ref: (
            i * sd + c, (kc_ref, vc_ref), (i - 1) * sd + c, None)

        def loop(lo, hi, fn):
            def body(idx, carry):
                fn(idx)
                return carry
            lax.fori_loop(lo, hi, body, 0)

        W = ATT_LANES if dil == 1 else min(ATT_LANES, dil)
        if dil == 1:
            hb = n_sub // W
            run([first(0)] + [later(0, k * hb) for k in range(1, W)])
            loop(1, hb, lambda i, hb=hb: run([later(0, i + k * hb) for k in range(W)]))
        else:
            cs = dil // W
            def streams(c, cs=cs, n_sub=n_sub):
                run([first(c + k * cs) for k in range(W)])
                if n_sub > 1:
                    loop(1, n_sub, lambda i: run([later(c + k * cs, i) for k in range(W)]))
            if cs == 1:
                streams(0)
            else:
                loop(0, cs, streams)

    o_ref[0] = (acc_scr[...] / l_scr[...]).astype(o_ref.dtype)


def _attn_prompt_call(qkv, slopes):
    B, T, W = qkv.shape
    hw = H_B * HD_B
    n_hp = H_B // 2
    TB = ATT_BLOCK
    assert T % TB == 0
    col = lambda g, which, hp: (g * 3 * hw + which * hw) // LANES + hp
    cur = lambda g, which: pl.BlockSpec((1, TB, LANES), lambda b, hp, j: (b, j, col(g, which, hp)))
    def prev(g, which):
        rows = ATT_STEPS * GROUPS[g][1]
        per = TB // rows
        return pl.BlockSpec((1, rows, LANES),
                            lambda b, hp, j: (b, jnp.maximum(j * per - 1, 0), col(g, which, hp)))
    gs = range(N_GROUPS)
    in_specs = ([pl.BlockSpec(memory_space=pltpu.SMEM)]
                + [cur(g, 0) for g in gs] + [cur(g, 1) for g in gs] + [cur(g, 2) for g in gs]
                + [prev(g, 1) for g in gs] + [prev(g, 2) for g in gs])
    return pl.pallas_call(
        _attn_prompt_kernel,
        grid=(B, n_hp, T // TB),
        in_specs=in_specs,
        out_specs=pl.BlockSpec((1, TB, LANES), lambda b, hp, j: (b, j, hp)),
        out_shape=jax.ShapeDtypeStruct((B, T, hw), BF16),
        scratch_shapes=[pltpu.VMEM((TB, LANES), F32)] * 3,
        compiler_params=pltpu.CompilerParams(
            dimension_semantics=("parallel", "parallel", "arbitrary"), vmem_limit_bytes=VMEM_LIMIT),
        name="attn_prompt",
    )(slopes, *([qkv] * (5 * N_GROUPS)))


def _kv_window_kernel(x_ref, o_ref):
    for hp in range(H_B // 2):
        sl = slice(hp * LANES, (hp + 1) * LANES)
        o_ref[0, 0, sl, :] = x_ref[0, :, sl].T


def _kv_window_call(qkv, g, window):
    B, T, _ = qkv.shape
    hw = H_B * HD_B
    wb = min(window, 512)
    first = (T - window) // wb
    out = pl.pallas_call(
        _kv_window_kernel,
        grid=(B, 2, window // wb),
        in_specs=[pl.BlockSpec((1, wb, hw), lambda b, kv, i: (b, first + i, 3 * g + 1 + kv))],
        out_specs=pl.BlockSpec((1, 1, hw, wb), lambda b, kv, i: (b, kv, 0, i)),
        out_shape=jax.ShapeDtypeStruct((B, 2, hw, window), F32),
        compiler_params=pltpu.CompilerParams(dimension_semantics=("parallel", "parallel", "parallel")),
        name=f"kv_window_g{g}",
    )(qkv)
    return out.reshape(B, 2, H_B, HD_B, window).transpose(0, 4, 1, 2, 3)


def _roll_append(old, new, o_ref, kv):
    R, W = old.shape
    S = new.shape[0]
    rolled = pltpu.roll(old, W - S, 1)
    new_t = jnp.concatenate([jnp.zeros((LANES - S, R), F32), new], axis=0).T
    if W > LANES:
        o_ref[0, kv, :, :W - LANES] = rolled[:, :W - LANES]
    lane = lax.broadcasted_iota(jnp.int32, (R, LANES), 1)
    o_ref[0, kv, :, W - LANES:] = jnp.where(lane >= LANES - S, new_t, rolled[:, W - LANES:])


def _attn_decode_kernel(slopes_ref, *refs):
    ng = N_GROUPS
    q_refs, k_refs, v_refs = refs[0:ng], refs[ng:2 * ng], refs[2 * ng:3 * ng]
    kt_refs, vt_refs = refs[3 * ng:4 * ng], refs[4 * ng:5 * ng]
    o_ref = refs[5 * ng]
    roll_refs = refs[5 * ng + 1:6 * ng + 1]
    hp = pl.program_id(1)
    S = q_refs[0].shape[1]
    lane_lo = lax.broadcasted_iota(jnp.int32, (S, LANES), 1) < HD_B
    scale = HD_B ** -0.5

    for kv, (new_refs, old_refs) in enumerate(((k_refs, kt_refs), (v_refs, vt_refs))):
        for g in range(ng):
            _roll_append(old_refs[g][0, 0], new_refs[g][0], roll_refs[g], kv)

    def chain(g, h):
        window, dil = GROUPS[g]
        slope = slopes_ref[g * H_B + 2 * hp + h]
        q = q_refs[g][0]
        qh = jnp.where(lane_lo, q, 0.0) if h == 0 else jnp.where(lane_lo, 0.0, q)
        s_old = _bdot(qh, kt_refs[g][0, 0])
        s_new = _bdot_nt(qh, k_refs[g][0])
        yield
        j_o = lax.broadcasted_iota(jnp.int32, (S, window), 0)
        w_o = lax.broadcasted_iota(jnp.int32, (S, window), 1)
        dist_o = window + j_o - w_o
        ok_o = ((dist_o & (dil - 1)) == 0) & (dist_o <= window)
        j_n = lax.broadcasted_iota(jnp.int32, (S, S), 0)
        dist_n = j_n - lax.broadcasted_iota(jnp.int32, (S, S), 1)
        ok_n = ((dist_n & (dil - 1)) == 0) & (dist_n >= 0)
        s_old = jnp.where(ok_o, s_old * scale - slope * dist_o.astype(F32), -jnp.inf)
        s_new = jnp.where(ok_n, s_new * scale - slope * dist_n.astype(F32), -jnp.inf)
        m = jnp.maximum(jnp.max(s_old, axis=-1, keepdims=True), jnp.max(s_new, axis=-1, keepdims=True))
        p_old = jnp.exp(s_old - m)
        p_new = jnp.exp(s_new - m)
        l = jnp.sum(p_old, axis=-1, keepdims=True) + jnp.sum(p_new, axis=-1, keepdims=True)
        acc = _bdot_nt(p_old, vt_refs[g][0, 0]) + _bdot(p_new, v_refs[g][0])
        yield
        return m, l, acc

    res = _lockstep([chain(g, h) for g in range(ng) for h in range(2)])
    m = l = acc = None
    for g in range(ng):
        (m0, l0, a0), (m1, l1, a1) = res[2 * g], res[2 * g + 1]
        mg = jnp.where(lane_lo, m0, m1)
        lg = jnp.where(lane_lo, l0, l1)
        ag = jnp.where(lane_lo, a0, a1)
        if g == 0:
            m, l, acc = mg, lg, ag
        else:
            m_new = jnp.maximum(m, mg)
            w_old, w_new = jnp.exp(m - m_new), jnp.exp(mg - m_new)
            l = l * w_old + lg * w_new
            acc = acc * w_old + ag * w_new
            m = m_new
    o_ref[0] = (acc / l).astype(o_ref.dtype)


def _attn_decode_call(qkv, caches, slopes):
    B, S, _ = qkv.shape
    hw = H_B * HD_B
    n_hp = H_B // 2
    assert all(w >= S and d & (d - 1) == 0 for w, d in GROUPS)
    cache_t = [c.transpose(0, 2, 3, 4, 1).reshape(B, 2, hw, c.shape[1]) for c in caches]
    col = lambda g, which, hp: (g * 3 * hw + which * hw) // LANES + hp
    new = lambda g, which: pl.BlockSpec((1, S, LANES), lambda b, hp: (b, 0, col(g, which, hp)))
    old = lambda g, kv: pl.BlockSpec((1, 1, LANES, GROUPS[g][0]), lambda b, hp: (b, kv, hp, 0))
    gs = range(N_GROUPS)
    in_specs = ([pl.BlockSpec(memory_space=pltpu.SMEM)]
                + [new(g, 0) for g in gs] + [new(g, 1) for g in gs] + [new(g, 2) for g in gs]
                + [old(g, 0) for g in gs] + [old(g, 1) for g in gs])
    outs = pl.pallas_call(
        _attn_decode_kernel,
        grid=(B, n_hp),
        in_specs=in_specs,
        out_specs=([pl.BlockSpec((1, S, LANES), lambda b, hp: (b, 0, hp))]
                   + [pl.BlockSpec((1, 2, LANES, w), lambda b, hp: (b, 0, hp, 0)) for w, _ in GROUPS]),
        out_shape=([jax.ShapeDtypeStruct((B, S, hw), BF16)]
                   + [jax.ShapeDtypeStruct((B, 2, hw, w), F32) for w, _ in GROUPS]),
        compiler_params=pltpu.CompilerParams(dimension_semantics=("parallel", "parallel")),
        name="attn_decode",
    )(slopes, *([qkv] * (3 * N_GROUPS)), *cache_t, *cache_t)
    rolled = [r.reshape(B, 2, H_B, HD_B, r.shape[-1]).transpose(0, 4, 1, 2, 3) for r in outs[1:]]
    return outs[0], rolled


def _alibi_slopes():
    n = N_GROUPS * H_B
    return 2.0 ** (-8.0 * jnp.arange(1, n + 1, dtype=F32) / n)


def _pair_states(s):
    B, H = s.shape[:2]
    s = s.reshape(B, H // 2, 2, HEAD_A, HEAD_A)
    z = jnp.zeros_like(s[:, :, 0])
    top = jnp.concatenate([s[:, :, 0], z], axis=-1)
    bot = jnp.concatenate([z, s[:, :, 1]], axis=-1)
    return jnp.concatenate([top, bot], axis=-2)


def _unpair_states(s2):
    B, P = s2.shape[:2]
    return jnp.stack([s2[:, :, :HEAD_A, :HEAD_A], s2[:, :, HEAD_A:, HEAD_A:]],
                     axis=2).reshape(B, 2 * P, HEAD_A, HEAD_A)


def _norm_mix_kernel(x_ref, g_ref, mu_ref, shift_ref, *refs, tiles_per_seq):
    out_refs, hlast_ref, carry_scr = refs[:-2], refs[-2], refs[-1]
    @pl.when(pl.program_id(0) == 0)
    def _():
        carry_scr[...] = jnp.zeros_like(carry_scr)

    x = x_ref[0]
    h = x * lax.rsqrt(jnp.mean(x * x, axis=-1, keepdims=True) + RMS_EPS) * g_ref[...]
    rows = h.shape[0]
    first_tile = pl.program_id(0) % tiles_per_seq == 0
    row0 = jnp.where(first_tile, shift_ref[0], carry_scr[...])
    prev = pltpu.roll(h, 1, 0) if rows > 1 else h
    prev = jnp.where(lax.broadcasted_iota(jnp.int32, h.shape, 0) == 0, row0, prev)
    carry_scr[...] = h[rows - 1:rows, :]
    hlast_ref[0] = h[rows - 1:rows, :]
    diff = prev - h
    for i, o_ref in enumerate(out_refs):
        o_ref[0] = (h + diff * mu_ref[i:i + 1, :]).astype(o_ref.dtype)


def _norm_mix_call(x, g, mu, shift0, bt=512):
    B, T, D = x.shape
    bt = min(bt, T)
    assert T % bt == 0
    n_mix = mu.shape[0]
    tiles = T // bt
    seq = pl.BlockSpec((1, bt, D), lambda m: (m // tiles, m % tiles, 0))
    per_seq = pl.BlockSpec((1, 1, D), lambda m: (m // tiles, 0, 0))
    outs = pl.pallas_call(
        functools.partial(_norm_mix_kernel, tiles_per_seq=tiles),
        grid=(B * tiles,),
        in_specs=[seq, pl.BlockSpec((1, D), lambda m: (0, 0)), pl.BlockSpec((n_mix, D), lambda m: (0, 0)),
                  per_seq],
        out_specs=[seq] * n_mix + [per_seq],
        out_shape=[jax.ShapeDtypeStruct((B, T, D), BF16)] * n_mix + [jax.ShapeDtypeStruct((B, 1, D), F32)],
        scratch_shapes=[pltpu.VMEM((1, D), F32)],
        compiler_params=pltpu.CompilerParams(dimension_semantics=("arbitrary",),
                                             vmem_limit_bytes=VMEM_LIMIT),
        name="rwkv_norm_mix",
    )(x, g.reshape(1, D), mu, shift0.reshape(B, 1, D))
    return [o.reshape(B * T, D) for o in outs[:n_mix]], outs[n_mix].reshape(B, D)


def _rwkv_block(x, shift0, wkv0, norm_g, p):
    (mu, w0, w1, w2, a0, a1, a2, g1, g2, k_k, k_a, r_k, lnx_w, lnx_b, w_r, w_k, w_v, w_o) = p
    B, T, D = x.shape
    M = B * T
    x2 = x.reshape(M, D)
    (xr, xw, xk, xv, xa, xg), h_last = _norm_mix_call(x, norm_g, mu, shift0)
    big = dict(bm=1024, bn=1024)
    r = _matmul_call([xr], w_r, name="rwkv_r", **big)
    k = _matmul_call([xk], w_k, name="rwkv_k", **big)
    v = _matmul_call([xv], w_v, name="rwkv_v", **big)
    wlog = _lora_call([xw], w1, w2, [w0.reshape(1, D)], mid=jnp.tanh,
                      post=lambda z, b: -jax.nn.softplus(-(b + z)) - 0.5, name="rwkv_w")
    a = _lora_call([xa], a1, a2, [a0.reshape(1, D)],
                   post=lambda z, b: jax.nn.sigmoid(b + z), name="rwkv_a")
    g = _lora_call([xg], g1, g2, mid=jax.nn.sigmoid, name="rwkv_g")

    L = WKV_CHUNK
    Tp = -(-T // L) * L
    def seq(z, fill=0.0):
        z = z.reshape(B, T, D)
        if Tp != T:
            z = jnp.pad(z, ((0, 0), (0, Tp - T), (0, 0)), constant_values=fill)
        return z
    y, s2 = _wkv_call(seq(r), seq(k), seq(v), seq(wlog, -jnp.inf), seq(a), seq(g),
                      k_k, k_a, r_k.reshape(D), lnx_w, lnx_b, _pair_states(wkv0), seqs_per_step=1, pairs_per_step=D // LANES)
    y2 = y[:, :T].reshape(M, D)
    out = _matmul_call([y2], w_o, [x2], post=lambda acc, res: res + acc, name="rwkv_o", **big)
    return out.reshape(B, T, D), _unpair_states(s2), h_last


def _ffn_block(x2, norm_g, w_up, w_down, layer):
    mid = _matmul_rows_call([x2, norm_g.reshape(1, -1)], w_up, pre=_rmsnorm,
                            post=lambda acc: jnp.square(jnp.maximum(acc, 0.0)),
                            out_dtype=BF16, bm=1024, bn=1024, layer=layer, name="ffn_up")
    return _matmul_call([mid], w_down, [x2], post=lambda acc, res: res + acc,
                        bm=1024, bn=1024, bk=2048, layer=layer, name="ffn_down")


def _attn_block(x, norm_g, w_qkv, w_o, slopes, caches=None):
    B, T, D = x.shape
    M = B * T
    x2 = x.reshape(M, D)
    qkv = _matmul_rows_call([x2, norm_g.reshape(1, -1)], w_qkv, pre=_rmsnorm,
                            bm=1024, bn=1024, name="attn_qkv").reshape(B, T, -1)
    hw = H_B * HD_B
    if caches is None:
        merged = _attn_prompt_call(qkv, slopes)
        bufs = [_kv_window_call(qkv, g, min(w, T)) for g, (w, _) in enumerate(GROUPS)]
    else:
        merged, bufs = _attn_decode_call(qkv, caches, slopes)
    out = _matmul_call([merged.reshape(M, hw)], w_o, [x2], post=lambda acc, res: res + acc,
                       bm=1024, name="attn_o")
    return out.reshape(B, T, D), bufs


def kernel(x_prompt, x_sample, state_wkv, state_shift, cache_kv_g1, cache_kv_g2, cache_kv_g3,
           norm_mix, norm_ffn, norm_final,
           rwkv_mu, rwkv_w0, rwkv_w1, rwkv_w2, rwkv_a0, rwkv_a1, rwkv_a2, rwkv_g1, rwkv_g2,
           rwkv_k_k, rwkv_k_a, rwkv_r_k, rwkv_lnx_w, rwkv_lnx_b, rwkv_w_r, rwkv_w_k, rwkv_w_v, rwkv_w_o,
           attn_w_qkv, attn_w_o, ffn_w_up, ffn_w_down):
    rwkv_params = (rwkv_mu, rwkv_w0, rwkv_w1, rwkv_w2, rwkv_a0, rwkv_a1, rwkv_a2, rwkv_g1, rwkv_g2,
                   rwkv_k_k, rwkv_k_a, rwkv_r_k, rwkv_lnx_w, rwkv_lnx_b,
                   rwkv_w_r, rwkv_w_k, rwkv_w_v, rwkv_w_o)
    slopes = _alibi_slopes()
    Bp, Tp, D = x_prompt.shape
    Bs, Ts, _ = x_sample.shape

    def ffn(x, i):
        B, T, _ = x.shape
        return _ffn_block(x.reshape(B * T, D), norm_ffn[i], ffn_w_up, ffn_w_down, i).reshape(B, T, D)

    xp, wkv_p, shift_p = _rwkv_block(x_prompt, jnp.zeros((Bp, D), F32),
                                     jnp.zeros((Bp, H_A, HEAD_A, HEAD_A), F32), norm_mix[0], rwkv_params)
    xs, wkv_s, shift_s = _rwkv_block(x_sample, state_shift, state_wkv, norm_mix[0], rwkv_params)
    xp, xs = ffn(xp, 0), ffn(xs, 0)
    xp, (kv1_p, kv2_p, kv3_p) = _attn_block(xp, norm_mix[1], attn_w_qkv, attn_w_o, slopes)
    xs, (kv1_s, kv2_s, kv3_s) = _attn_block(xs, norm_mix[1], attn_w_qkv, attn_w_o, slopes,
                                            (cache_kv_g1, cache_kv_g2, cache_kv_g3))
    xp, xs = ffn(xp, 1), ffn(xs, 1)
    y_prompt = _rmsnorm_call(xp.reshape(Bp * Tp, D), norm_final).reshape(Bp, Tp, D)
    y_sample = _rmsnorm_call(xs.reshape(Bs * Ts, D), norm_final).reshape(Bs, Ts, D)
    return (y_prompt, y_sample, wkv_p, shift_p, kv1_p, kv2_p, kv3_p,
            wkv_s, shift_s, kv1_s, kv2_s, kv3_s)
```

```python
import functools

import jax
import jax.numpy as jnp
from jax import lax
from jax.experimental import pallas as pl
from jax.experimental.pallas import tpu as pltpu

F32 = jnp.float32
BF16 = jnp.bfloat16

D_MODEL = 2048
HEAD_A = 64
H_A = D_MODEL // HEAD_A
GN_EPS = 64e-5
GROUPS = ((128, 1), (512, 4), (2048, 16))
N_GROUPS = len(GROUPS)
H_B = 16
HD_B = 64
ATT_STEPS = 128
RMS_EPS = 1e-6

LANES = 128
VMEM_LIMIT = 56 * 1024 * 1024


def _bdot(a, b):
    return jnp.dot(a.astype(BF16), b.astype(BF16), preferred_element_type=F32)


def _bdot_nt(a, b):
    return lax.dot_general(a.astype(BF16), b.astype(BF16), (((1,), (1,)), ((), ())),
                           preferred_element_type=F32)


def _bdot_tn(a, b):
    return lax.dot_general(a.astype(BF16), b.astype(BF16), (((0,), (0,)), ((), ())),
                           preferred_element_type=F32)


def _exact_dot(m, x):
    hi = x.astype(BF16)
    r1 = x - hi.astype(F32)
    mid = r1.astype(BF16)
    lo = (r1 - mid.astype(F32)).astype(BF16)
    n = x.shape[1]
    parts = jnp.dot(m.astype(BF16), jnp.concatenate([hi, mid, lo], axis=1), preferred_element_type=F32)
    return (parts[:, :n] + parts[:, n:2 * n]) + parts[:, 2 * n:]


def _lockstep(chains):
    results = [None] * len(chains)
    live = list(range(len(chains)))
    while live:
        for idx in list(live):
            try:
                next(chains[idx])
            except StopIteration as done:
                results[idx] = done.value
                live.remove(idx)
    return results


def _head_sum(x, lane_lo):
    s0 = jnp.sum(jnp.where(lane_lo, x, 0.0), axis=-1, keepdims=True)
    s1 = jnp.sum(jnp.where(lane_lo, 0.0, x), axis=-1, keepdims=True)
    return jnp.where(lane_lo, s0, s1)


WKV_CHUNK = HEAD_A
(_MK_EYE, _MK_LVL0, _MK_BD, _MK_NMASK, _MK_CAUSAL, _MK_TRI, _MK_LEVELS) = range(7)
_N_LEVELS = 5
_N_MASKS = _MK_LEVELS + _N_LEVELS


def _wkv_fill_masks(mk_ref):
    L = WKV_CHUNK
    r2 = lax.broadcasted_iota(jnp.int32, (LANES, LANES), 0)
    c2 = lax.broadcasted_iota(jnp.int32, (LANES, LANES), 1)
    f = lambda m: m.astype(F32)
    same_block = (r2 < L) == (c2 < L)
    strict = (c2 % L) < (r2 % L)
    incl = (c2 % L) <= (r2 % L)
    mk_ref[_MK_EYE] = f(r2 == c2)
    mk_ref[_MK_LVL0] = f((r2 // 2 == c2 // 2) & (c2 < r2))
    mk_ref[_MK_BD] = f(same_block)
    mk_ref[_MK_NMASK] = f(strict & same_block)
    mk_ref[_MK_CAUSAL] = f(strict | (incl & (r2 >= L)))
    mk_ref[_MK_TRI] = f(incl & (r2 < L) & (c2 < L))
    size = 2
    for lvl in range(_N_LEVELS):
        mk_ref[_MK_LEVELS + lvl] = f((r2 // (2 * size) == c2 // (2 * size))
                                     & ((r2 // size) % 2 == 1) & ((c2 // size) % 2 == 0) & same_block)
        size *= 2


def _wkv_pair_chunk(r, kraw, v, wl, a, g, k_k, k_a, r_k, lnx_w, lnx_b, s2, mk_ref):
    L = WKV_CHUNK
    lane_lo = lax.broadcasted_iota(jnp.int32, (L, LANES), 1) < HEAD_A
    lo2 = lax.broadcasted_iota(jnp.int32, (2 * L, LANES), 1) < HEAD_A

    logd = -jnp.exp(wl)
    kkraw = kraw * k_k
    kk = kkraw * lax.rsqrt(jnp.maximum(_head_sum(kkraw * kkraw, lane_lo), 1e-24))
    k2 = kraw * (1.0 + (a - 1.0) * k_a)

    cum = _exact_dot(mk_ref[_MK_TRI][:L, :L], logd)
    w_incl = jnp.exp(cum)
    w_excl = jnp.exp(cum - logd)
    w_inv = jnp.exp(-cum)
    w_last = w_incl[L - 1:L, :]

    at = -kk * w_excl
    bt = (kk * a) * w_inv
    kt = k2 * w_inv
    rt = r * w_incl

    ar = jnp.concatenate([at, rt], axis=0)
    bk = jnp.concatenate([bt, kt], axis=0)
    kb = jnp.concatenate([kt, bt], axis=0)
    mm = _bdot_nt(jnp.concatenate([jnp.where(lo2, ar, 0.0), jnp.where(lo2, 0.0, ar)], axis=0),
                  jnp.concatenate([bk, kb], axis=0))
    m0 = mm[:2 * L, :2 * L]
    m1 = mm[2 * L:, 2 * L:]
    aprp = _bdot_nt(ar, s2)
    yield

    a0 = m0[:L]
    a1 = m1[:L]
    n2 = jnp.concatenate([a0, a1], axis=0) * mk_ref[_MK_NMASK]
    causal = mk_ref[_MK_CAUSAL]
    strict, incl = causal[:L], causal[L:]
    v_lo = jnp.where(lane_lo, v, 0.0)
    v_hi = jnp.where(lane_lo, 0.0, v)
    ak_cat = jnp.where(lane_lo, a1, a0) * strict

    t2 = mk_ref[_MK_EYE] + n2 * mk_ref[_MK_LVL0]
    for lvl in range(_N_LEVELS):
        tc = _bdot(t2, n2 * mk_ref[_MK_LEVELS + lvl])
        yield
        t2 = t2 + _bdot(tc, t2)
        yield
    tcat = t2[:L] + t2[L:]
    rhs = aprp[:L] + _bdot(ak_cat, jnp.concatenate([v_hi, v_lo], axis=0))
    yield

    u = _bdot(tcat, jnp.concatenate([jnp.where(lane_lo, rhs, 0.0), jnp.where(lane_lo, 0.0, rhs)], axis=0))
    yield
    u_lo = jnp.where(lane_lo, u, 0.0)
    u_hi = jnp.where(lane_lo, 0.0, u)

    y = aprp[L:] + _bdot(jnp.concatenate([m0[L:] * incl, m1[L:] * incl], axis=1),
                         jnp.concatenate([u_lo, v_lo, v_hi, u_hi], axis=0))
    delta = _bdot_tn(jnp.concatenate([u, v], axis=0), bk)
    yield
    s_new = (s2 + delta * mk_ref[_MK_BD]) * w_last

    mean = _head_sum(y, lane_lo) * (1.0 / HEAD_A)
    yc = y - mean
    var = _head_sum(yc * yc, lane_lo) * (1.0 / HEAD_A)
    yn = yc * lax.rsqrt(var + GN_EPS) * lnx_w + lnx_b
    bonus = _head_sum(r * k2 * r_k, lane_lo)
    return (yn + bonus * v) * g, s_new


def _wkv_kernel(r_ref, k_ref, v_ref, wl_ref, a_ref, g_ref, kk_ref, ka_ref, rk_ref, lw_ref, lb_ref,
                s0_ref, y_ref, sout_ref, s_scr, mk_scr, *, n_seqs, n_pairs):
    b, pb, c = pl.program_id(0), pl.program_id(1), pl.program_id(2)

    @pl.when((b == 0) & (pb == 0) & (c == 0))
    def _():
        _wkv_fill_masks(mk_scr)

    @pl.when(c == 0)
    def _():
        s_scr[...] = s0_ref[...]

    items = [(q, p, slice(p * LANES, (p + 1) * LANES)) for q in range(n_seqs) for p in range(n_pairs)]
    chains = [_wkv_pair_chunk(
        r_ref[q, :, sl], k_ref[q, :, sl], v_ref[q, :, sl], wl_ref[q, :, sl], a_ref[q, :, sl],
        g_ref[q, :, sl], kk_ref[:, sl], ka_ref[:, sl], rk_ref[:, sl], lw_ref[:, sl], lb_ref[:, sl],
        s_scr[q, p], mk_scr) for q, p, sl in items]
    for (y, s_new), (q, p, sl) in zip(_lockstep(chains), items):
        y_ref[q, :, sl] = y.astype(y_ref.dtype)
        s_scr[q, p] = s_new

    @pl.when(c == pl.num_programs(2) - 1)
    def _():
        sout_ref[...] = s_scr[...]


def _wkv_call(r, k, v, wlog, a, g, k_k, k_a, r_k, lnx_w, lnx_b, s2_0, *, seqs_per_step, pairs_per_step):
    B, T, D = r.shape
    L, P, Q = WKV_CHUNK, pairs_per_step, seqs_per_step
    assert T % L == 0 and 2 * L == LANES and (D // LANES) % P == 0 and B % Q == 0
    seq = pl.BlockSpec((Q, L, P * LANES), lambda b, p, c: (b, c, p))
    par = pl.BlockSpec((1, P * LANES), lambda b, p, c: (0, p))
    st = pl.BlockSpec((Q, P, LANES, LANES), lambda b, p, c: (b, p, 0, 0))
    row = lambda x: x.reshape(1, D)
    return pl.pallas_call(
        functools.partial(_wkv_kernel, n_seqs=Q, n_pairs=P),
        grid=(B // Q, D // LANES // P, T // L),
        in_specs=[seq] * 6 + [par] * 5 + [st],
        out_specs=[seq, st],
        out_shape=[jax.ShapeDtypeStruct((B, T, D), BF16),
                   jax.ShapeDtypeStruct((B, D // LANES, LANES, LANES), F32)],
        scratch_shapes=[pltpu.VMEM((Q, P, LANES, LANES), F32), pltpu.VMEM((_N_MASKS, LANES, LANES), F32)],
        compiler_params=pltpu.CompilerParams(
            dimension_semantics=("arbitrary", "arbitrary", "arbitrary")),
        name="wkv7_chunk_scan",
    )(r, k, v, wlog, a, g, row(k_k), row(k_a), row(r_k), row(lnx_w), row(lnx_b), s2_0)


def _rmsnorm(x, g):
    return x * lax.rsqrt(jnp.mean(x * x, axis=-1, keepdims=True) + RMS_EPS) * g


def _rmsnorm_kernel(x_ref, g_ref, o_ref):
    o_ref[...] = _rmsnorm(x_ref[...], g_ref[...]).astype(o_ref.dtype)


def _rmsnorm_call(x, g, out_dtype=F32):
    M, D = x.shape
    bm = min(M, 512)
    return pl.pallas_call(
        _rmsnorm_kernel,
        grid=(M // bm,),
        in_specs=[pl.BlockSpec((bm, D), lambda m: (m, 0)), pl.BlockSpec((1, D), lambda m: (0, 0))],
        out_specs=pl.BlockSpec((bm, D), lambda m: (m, 0)),
        out_shape=jax.ShapeDtypeStruct((M, D), out_dtype),
        compiler_params=pltpu.CompilerParams(dimension_semantics=("parallel",)),
        name="rmsnorm",
    )(x, g.reshape(1, D))


MM_ROW_CHUNK = 256


def _mm_kernel(*refs, n_x, n_e, pre, post, nk):
    x_refs = refs[:n_x]
    w_ref = refs[n_x]
    e_refs = refs[n_x + 1:n_x + 1 + n_e]
    o_ref = refs[n_x + 1 + n_e]
    bm = o_ref.shape[0]
    acc_ref = refs[-1] if nk > 1 else None
    k = pl.program_id(2)
    if nk > 1:
        @pl.when(k == 0)
        def _():
            acc_ref[...] = jnp.zeros_like(acc_ref)

    w = w_ref[...].astype(BF16)
    chunk = min(bm, MM_ROW_CHUNK)
    for r0 in range(0, bm, chunk):
        rows = slice(r0, r0 + chunk)
        tile = lambda ref: ref[rows, :] if ref.shape[0] == bm else ref[...]
        x = pre(*[tile(r) for r in x_refs])
        part = jnp.dot(x.astype(BF16), w, preferred_element_type=F32)
        if nk == 1:
            o_ref[rows, :] = post(part, *[tile(e) for e in e_refs]).astype(o_ref.dtype)
        else:
            acc_ref[rows, :] += part

    if nk > 1:
        @pl.when(k == nk - 1)
        def _():
            o_ref[...] = post(acc_ref[...], *[e[...] for e in e_refs]).astype(o_ref.dtype)


def _mm_rows_kernel(*refs, n_x, n_e, pre, post):
    x_refs = refs[:n_x]
    w_ref = refs[n_x]
    e_refs = refs[n_x + 1:n_x + 1 + n_e]
    o_ref = refs[n_x + 1 + n_e]
    x_scr = refs[-1]

    @pl.when(pl.program_id(1) == 0)
    def _():
        x_scr[...] = pre(*[r[...] for r in x_refs]).astype(BF16)

    bm = o_ref.shape[0]
    w = w_ref[...].astype(BF16)
    chunk = min(bm, MM_ROW_CHUNK)
    for r0 in range(0, bm, chunk):
        rows = slice(r0, r0 + chunk)
        tile = lambda ref: ref[rows, :] if ref.shape[0] == bm else ref[...]
        part = jnp.dot(x_scr[rows, :], w, preferred_element_type=F32)
        o_ref[rows, :] = post(part, *[tile(e) for e in e_refs]).astype(o_ref.dtype)


def _identity(x):
    return x


def _matmul_rows_call(xs, w, extras=(), *, pre=_identity, post=_identity, out_dtype=F32,
                      bm=512, bn=512, layer=None, name="matmul"):
    K, N = w.shape[-2:]
    M = max(x.shape[0] for x in xs)
    bm = min(bm, M)
    bn = min(bn, N)
    assert M % bm == 0 and N % bn == 0
    x_specs = [pl.BlockSpec((bm, K), lambda m, n: (m, 0), pipeline_mode=pl.Buffered(1)) if x.shape[0] == M
               else pl.BlockSpec((x.shape[0], K), lambda m, n: (0, 0)) for x in xs]
    e_specs = [pl.BlockSpec((bm, bn), lambda m, n: (m, n)) if e.shape[0] == M
               else pl.BlockSpec((1, bn), lambda m, n: (0, n)) for e in extras]
    if layer is None:
        w_spec = pl.BlockSpec((K, bn), lambda m, n: (0, n))
    else:
        w_spec = pl.BlockSpec((None, K, bn), lambda m, n: (layer, 0, n))
    return pl.pallas_call(
        functools.partial(_mm_rows_kernel, n_x=len(xs), n_e=len(extras), pre=pre, post=post),
        grid=(M // bm, N // bn),
        in_specs=x_specs + [w_spec] + e_specs,
        out_specs=pl.BlockSpec((bm, bn), lambda m, n: (m, n)),
        out_shape=jax.ShapeDtypeStruct((M, N), out_dtype),
        scratch_shapes=[pltpu.VMEM((bm, K), BF16)],
        compiler_params=pltpu.CompilerParams(
            dimension_semantics=("parallel", "arbitrary"), vmem_limit_bytes=VMEM_LIMIT),
        name=name,
    )(*xs, w, *extras)


def _matmul_call(xs, w, extras=(), *, pre=_identity, post=_identity, out_dtype=F32,
                 bm=512, bn=512, bk=None, layer=None, name="matmul"):
    K, N = w.shape[-2:]
    M = max(x.shape[0] for x in xs)
    bm = min(bm, M)
    bn = min(bn, N)
    bk = K if bk is None else min(bk, K)
    assert M % bm == 0 and N % bn == 0 and K % bk == 0
    nk = K // bk
    x_specs = [pl.BlockSpec((bm, bk), lambda n, m, k: (m, k)) if x.shape[0] == M
               else pl.BlockSpec((x.shape[0], bk), lambda n, m, k: (0, k)) for x in xs]
    e_specs = [pl.BlockSpec((bm, bn), lambda n, m, k: (m, n)) if e.shape[0] == M
               else pl.BlockSpec((1, bn), lambda n, m, k: (0, n)) for e in extras]
    if layer is None:
        w_spec = pl.BlockSpec((bk, bn), lambda n, m, k: (k, n))
    else:
        w_spec = pl.BlockSpec((None, bk, bn), lambda n, m, k: (layer, k, n))
    return pl.pallas_call(
        functools.partial(_mm_kernel, n_x=len(xs), n_e=len(extras), pre=pre, post=post, nk=nk),
        grid=(N // bn, M // bm, nk),
        in_specs=x_specs + [w_spec] + e_specs,
        out_specs=pl.BlockSpec((bm, bn), lambda n, m, k: (m, n)),
        out_shape=jax.ShapeDtypeStruct((M, N), out_dtype),
        scratch_shapes=[pltpu.VMEM((bm, bn), F32)] if nk > 1 else [],
        compiler_params=pltpu.CompilerParams(
            dimension_semantics=("parallel", "parallel", "arbitrary"),
            vmem_limit_bytes=VMEM_LIMIT),
        name=name,
    )(*xs, w, *extras)


def _lora_kernel(*refs, n_x, n_e, pre, mid, post):
    x_refs = refs[:n_x]
    w1_ref, w2_ref = refs[n_x], refs[n_x + 1]
    e_refs = refs[n_x + 2:n_x + 2 + n_e]
    o_ref = refs[n_x + 2 + n_e]
    x = pre(*[r[...] for r in x_refs])
    z = mid(_bdot(x, w1_ref[...]))
    o_ref[...] = post(_bdot(z, w2_ref[...]), *[e[...] for e in e_refs]).astype(o_ref.dtype)


def _lora_call(xs, w1, w2, extras=(), *, pre=_identity, mid=_identity, post=_identity, bm=512, name="lora"):
    K, R = w1.shape
    N = w2.shape[1]
    M = max(x.shape[0] for x in xs)
    bm = min(bm, M)
    x_specs = [pl.BlockSpec((bm, K), lambda m: (m, 0)) if x.shape[0] == M
               else pl.BlockSpec((x.shape[0], K), lambda m: (0, 0)) for x in xs]
    e_specs = [pl.BlockSpec((bm, N), lambda m: (m, 0)) if e.shape[0] == M
               else pl.BlockSpec((1, N), lambda m: (0, 0)) for e in extras]
    return pl.pallas_call(
        functools.partial(_lora_kernel, n_x=len(xs), n_e=len(extras), pre=pre, mid=mid, post=post),
        grid=(M // bm,),
        in_specs=x_specs + [pl.BlockSpec((K, R), lambda m: (0, 0)), pl.BlockSpec((R, N), lambda m: (0, 0))]
        + e_specs,
        out_specs=pl.BlockSpec((bm, N), lambda m: (m, 0)),
        out_shape=jax.ShapeDtypeStruct((M, N), F32),
        compiler_params=pltpu.CompilerParams(dimension_semantics=("parallel",),
                                             vmem_limit_bytes=VMEM_LIMIT),
        name=name,
    )(*xs, w1, w2, *extras)


ATT_BLOCK = ATT_STEPS * max(d for _, d in GROUPS)
ATT_LANES = 4


def _attn_prompt_kernel(slopes_ref, *refs):
    ng = N_GROUPS
    q_refs, kc_refs, vc_refs = refs[0:ng], refs[ng:2 * ng], refs[2 * ng:3 * ng]
    kp_refs, vp_refs = refs[3 * ng:4 * ng], refs[4 * ng:5 * ng]
    o_ref = refs[5 * ng]
    m_scr, l_scr, acc_scr = refs[5 * ng + 1:]
    hp = pl.program_id(1)
    j = pl.program_id(2)
    S = ATT_STEPS
    lane_lo = lax.broadcasted_iota(jnp.int32, (S, LANES), 1) < HD_B
    iq = lax.broadcasted_iota(jnp.int32, (S, 2 * S), 0)
    jk = lax.broadcasted_iota(jnp.int32, (S, 2 * S), 1)
    steps = iq + S - jk
    mask = jnp.where((steps >= 0) & (steps <= S), 0.0, -jnp.inf)
    scale = HD_B ** -0.5
    no_prev = jnp.where(j > 0, 0.0, -jnp.inf)

    order = sorted(range(ng), key=lambda g: -GROUPS[g][1])
    for g in order:
        dil = GROUPS[g][1]
        biases = [mask - slopes_ref[g * H_B + 2 * hp + h] * (steps * dil).astype(F32) for h in range(2)]
        q_ref, kc_ref, vc_ref, kp_ref, vp_ref = q_refs[g], kc_refs[g], vc_refs[g], kp_refs[g], vp_refs[g]

        def attend(cur_start, prev_ref, prev_start, prev_bias, dil=dil, g=g, biases=biases, q_ref=q_ref,
                   kc_ref=kc_ref, vc_ref=vc_ref):
            cur = pl.ds(cur_start, S, stride=dil) if dil > 1 else pl.ds(cur_start, S)
            prv = pl.ds(prev_start, S, stride=dil) if dil > 1 else pl.ds(prev_start, S)
            q, kc, vc = q_ref[0, cur, :], kc_ref[0, cur, :], vc_ref[0, cur, :]
            kp, vp = prev_ref[0][0, prv, :], prev_ref[1][0, prv, :]
            qs = [jnp.where(lane_lo, q, 0.0), jnp.where(lane_lo, 0.0, q)]
            scores = [(_bdot_nt(qh, kp), _bdot_nt(qh, kc)) for qh in qs]
            yield
            ms, ls, probs = [], [], []
            for h in range(2):
                sp = scores[h][0] * scale + biases[h][:, :S]
                if prev_bias is not None:
                    sp = sp + prev_bias
                sc = scores[h][1] * scale + biases[h][:, S:]
                m = jnp.max(jnp.maximum(sp, sc), axis=-1, keepdims=True)
                pp = jnp.exp(sp - m)
                pc = jnp.exp(sc - m)
                ls.append(jnp.sum(pp + pc, axis=-1, keepdims=True))
                probs.append((pp, pc))
                ms.append(m)
            accs = [_bdot(pp, vp) + _bdot(pc, vc) for pp, pc in probs]
            yield
            m = jnp.where(lane_lo, ms[0], ms[1])
            l = jnp.where(lane_lo, ls[0], ls[1])
            acc = jnp.where(lane_lo, accs[0], accs[1])
            if g != order[0]:
                m_old = m_scr[cur, :]
                m_new = jnp.maximum(m_old, m)
                w_old, w_new = jnp.exp(m_old - m_new), jnp.exp(m - m_new)
                l = l_scr[cur, :] * w_old + l * w_new
                acc = acc_scr[cur, :] * w_old + acc * w_new
                m = m_new
            m_scr[cur, :] = m
            l_scr[cur, :] = l
            acc_scr[cur, :] = acc

        sd = S * dil
        n_sub = ATT_BLOCK // sd
        run = lambda items, attend=attend: _lockstep([attend(*it) for it in items])
        first = lambda c, kp_ref=kp_ref, vp_ref=vp_ref: (c, (kp_ref, vp_ref), c, no_prev)
        later = lambda c, i, sd=sd, kc_ref=kc_ref, vc_ref=vc_ref: (
            i * sd + c, (kc_ref, vc_ref), (i - 1) * sd + c, None)

        def loop(lo, hi, fn):
            def body(idx, carry):
                fn(idx)
                return carry
            lax.fori_loop(lo, hi, body, 0)

        W = ATT_LANES if dil == 1 else min(ATT_LANES, dil)
        if dil == 1:
            hb = n_sub // W
            run([first(0)] + [later(0, k * hb) for k in range(1, W)])
            loop(1, hb, lambda i, hb=hb: run([later(0, i + k * hb) for k in range(W)]))
        else:
            cs = dil // W
            def streams(c, cs=cs, n_sub=n_sub):
                run([first(c + k * cs) for k in range(W)])
                if n_sub > 1:
                    loop(1, n_sub, lambda i: run([later(c + k * cs, i) for k in range(W)]))
            if cs == 1:
                streams(0)
            else:
                loop(0, cs, streams)

    o_ref[0] = (acc_scr[...] / l_scr[...]).astype(o_ref.dtype)


def _attn_prompt_call(qkv, slopes):
    B, T, W = qkv.shape
    hw = H_B * HD_B
    n_hp = H_B // 2
    TB = ATT_BLOCK
    assert T % TB == 0
    col = lambda g, which, hp: (g * 3 * hw + which * hw) // LANES + hp
    cur = lambda g, which: pl.BlockSpec((1, TB, LANES), lambda b, hp, j: (b, j, col(g, which, hp)))
    def prev(g, which):
        rows = ATT_STEPS * GROUPS[g][1]
        per = TB // rows
        return pl.BlockSpec((1, rows, LANES),
                            lambda b, hp, j: (b, jnp.maximum(j * per - 1, 0), col(g, which, hp)))
    gs = range(N_GROUPS)
    in_specs = ([pl.BlockSpec(memory_space=pltpu.SMEM)]
                + [cur(g, 0) for g in gs] + [cur(g, 1) for g in gs] + [cur(g, 2) for g in gs]
                + [prev(g, 1) for g in gs] + [prev(g, 2) for g in gs])
    return pl.pallas_call(
        _attn_prompt_kernel,
        grid=(B, n_hp, T // TB),
        in_specs=in_specs,
        out_specs=pl.BlockSpec((1, TB, LANES), lambda b, hp, j: (b, j, hp)),
        out_shape=jax.ShapeDtypeStruct((B, T, hw), BF16),
        scratch_shapes=[pltpu.VMEM((TB, LANES), F32)] * 3,
        compiler_params=pltpu.CompilerParams(
            dimension_semantics=("parallel", "parallel", "arbitrary"), vmem_limit_bytes=VMEM_LIMIT),
        name="attn_prompt",
    )(slopes, *([qkv] * (5 * N_GROUPS)))


def _kv_window_kernel(x_ref, o_ref):
    for hp in range(H_B // 2):
        sl = slice(hp * LANES, (hp + 1) * LANES)
        o_ref[0, 0, sl, :] = x_ref[0, :, sl].T


def _kv_window_call(qkv, g, window):
    B, T, _ = qkv.shape
    hw = H_B * HD_B
    wb = min(window, 512)
    first = (T - window) // wb
    out = pl.pallas_call(
        _kv_window_kernel,
        grid=(B, 2, window // wb),
        in_specs=[pl.BlockSpec((1, wb, hw), lambda b, kv, i: (b, first + i, 3 * g + 1 + kv))],
        out_specs=pl.BlockSpec((1, 1, hw, wb), lambda b, kv, i: (b, kv, 0, i)),
        out_shape=jax.ShapeDtypeStruct((B, 2, hw, window), F32),
        compiler_params=pltpu.CompilerParams(dimension_semantics=("parallel", "parallel", "parallel")),
        name=f"kv_window_g{g}",
    )(qkv)
    return out.reshape(B, 2, H_B, HD_B, window).transpose(0, 4, 1, 2, 3)


def _roll_append(old, new, o_ref, kv):
    R, W = old.shape
    S = new.shape[0]
    rolled = pltpu.roll(old, W - S, 1)
    new_t = jnp.concatenate([jnp.zeros((LANES - S, R), F32), new], axis=0).T
    if W > LANES:
        o_ref[0, kv, :, :W - LANES] = rolled[:, :W - LANES]
    lane = lax.broadcasted_iota(jnp.int32, (R, LANES), 1)
    o_ref[0, kv, :, W - LANES:] = jnp.where(lane >= LANES - S, new_t, rolled[:, W - LANES:])


def _attn_decode_kernel(slopes_ref, *refs):
    ng = N_GROUPS
    q_refs, k_refs, v_refs = refs[0:ng], refs[ng:2 * ng], refs[2 * ng:3 * ng]
    kt_refs, vt_refs = refs[3 * ng:4 * ng], refs[4 * ng:5 * ng]
    o_ref = refs[5 * ng]
    roll_refs = refs[5 * ng + 1:6 * ng + 1]
    hp = pl.program_id(1)
    S = q_refs[0].shape[1]
    lane_lo = lax.broadcasted_iota(jnp.int32, (S, LANES), 1) < HD_B
    scale = HD_B ** -0.5

    for kv, (new_refs, old_refs) in enumerate(((k_refs, kt_refs), (v_refs, vt_refs))):
        for g in range(ng):
            _roll_append(old_refs[g][0, 0], new_refs[g][0], roll_refs[g], kv)

    def chain(g, h):
        window, dil = GROUPS[g]
        slope = slopes_ref[g * H_B + 2 * hp + h]
        q = q_refs[g][0]
        qh = jnp.where(lane_lo, q, 0.0) if h == 0 else jnp.where(lane_lo, 0.0, q)
        s_old = _bdot(qh, kt_refs[g][0, 0])
        s_new = _bdot_nt(qh, k_refs[g][0])
        yield
        j_o = lax.broadcasted_iota(jnp.int32, (S, window), 0)
        w_o = lax.broadcasted_iota(jnp.int32, (S, window), 1)
        dist_o = window + j_o - w_o
        ok_o = ((dist_o & (dil - 1)) == 0) & (dist_o <= window)
        j_n = lax.broadcasted_iota(jnp.int32, (S, S), 0)
        dist_n = j_n - lax.broadcasted_iota(jnp.int32, (S, S), 1)
        ok_n = ((dist_n & (dil - 1)) == 0) & (dist_n >= 0)
        s_old = jnp.where(ok_o, s_old * scale - slope * dist_o.astype(F32), -jnp.inf)
        s_new = jnp.where(ok_n, s_new * scale - slope * dist_n.astype(F32), -jnp.inf)
        m = jnp.maximum(jnp.max(s_old, axis=-1, keepdims=True), jnp.max(s_new, axis=-1, keepdims=True))
        p_old = jnp.exp(s_old - m)
        p_new = jnp.exp(s_new - m)
        l = jnp.sum(p_old, axis=-1, keepdims=True) + jnp.sum(p_new, axis=-1, keepdims=True)
        acc = _bdot_nt(p_old, vt_refs[g][0, 0]) + _bdot(p_new, v_refs[g][0])
        yield
        return m, l, acc

    res = _lockstep([chain(g, h) for g in range(ng) for h in range(2)])
    m = l = acc = None
    for g in range(ng):
        (m0, l0, a0), (m1, l1, a1) = res[2 * g], res[2 * g + 1]
        mg = jnp.where(lane_lo, m0, m1)
        lg = jnp.where(lane_lo, l0, l1)
        ag = jnp.where(lane_lo, a0, a1)
        if g == 0:
            m, l, acc = mg, lg, ag
        else:
            m_new = jnp.maximum(m, mg)
            w_old, w_new = jnp.exp(m - m_new), jnp.exp(mg - m_new)
            l = l * w_old + lg * w_new
            acc = acc * w_old + ag * w_new
            m = m_new
    o_ref[0] = (acc / l).astype(o_ref.dtype)


def _attn_decode_call(qkv, caches, slopes):
    B, S, _ = qkv.shape
    hw = H_B * HD_B
    n_hp = H_B // 2
    assert all(w >= S and d & (d - 1) == 0 for w, d in GROUPS)
    cache_t = [c.transpose(0, 2, 3, 4, 1).reshape(B, 2, hw, c.shape[1]) for c in caches]
    col = lambda g, which, hp: (g * 3 * hw + which * hw) // LANES + hp
    new = lambda g, which: pl.BlockSpec((1, S, LANES), lambda b, hp: (b, 0, col(g, which, hp)))
    old = lambda g, kv: pl.BlockSpec((1, 1, LANES, GROUPS[g][0]), lambda b, hp: (b, kv, hp, 0))
    gs = range(N_GROUPS)
    in_specs = ([pl.BlockSpec(memory_space=pltpu.SMEM)]
                + [new(g, 0) for g in gs] + [new(g, 1) for g in gs] + [new(g, 2) for g in gs]
                + [old(g, 0) for g in gs] + [old(g, 1) for g in gs])
    outs = pl.pallas_call(
        _attn_decode_kernel,
        grid=(B, n_hp),
        in_specs=in_specs,
        out_specs=([pl.BlockSpec((1, S, LANES), lambda b, hp: (b, 0, hp))]
                   + [pl.BlockSpec((1, 2, LANES, w), lambda b, hp: (b, 0, hp, 0)) for w, _ in GROUPS]),
        out_shape=([jax.ShapeDtypeStruct((B, S, hw), BF16)]
                   + [jax.ShapeDtypeStruct((B, 2, hw, w), F32) for w, _ in GROUPS]),
        compiler_params=pltpu.CompilerParams(dimension_semantics=("parallel", "parallel")),
        name="attn_decode",
    )(slopes, *([qkv] * (3 * N_GROUPS)), *cache_t, *cache_t)
    rolled = [r.reshape(B, 2, H_B, HD_B, r.shape[-1]).transpose(0, 4, 1, 2, 3) for r in outs[1:]]
    return outs[0], rolled


def _alibi_slopes():
    n = N_GROUPS * H_B
    return 2.0 ** (-8.0 * jnp.arange(1, n + 1, dtype=F32) / n)


def _pair_states(s):
    B, H = s.shape[:2]
    s = s.reshape(B, H // 2, 2, HEAD_A, HEAD_A)
    z = jnp.zeros_like(s[:, :, 0])
    top = jnp.concatenate([s[:, :, 0], z], axis=-1)
    bot = jnp.concatenate([z, s[:, :, 1]], axis=-1)
    return jnp.concatenate([top, bot], axis=-2)


def _unpair_states(s2):
    B, P = s2.shape[:2]
    return jnp.stack([s2[:, :, :HEAD_A, :HEAD_A], s2[:, :, HEAD_A:, HEAD_A:]],
                     axis=2).reshape(B, 2 * P, HEAD_A, HEAD_A)


def _softplus(x):
    return jnp.maximum(x, 0.0) + jnp.log(1.0 + jnp.exp(-jnp.abs(x)))


def _norm_mix_kernel(x_ref, g_ref, mu_ref, shift_ref, *refs, tiles_per_seq):
    out_refs, hlast_ref, carry_scr = refs[:-2], refs[-2], refs[-1]
    @pl.when(pl.program_id(0) == 0)
    def _():
        carry_scr[...] = jnp.zeros_like(carry_scr)

    x = x_ref[0]
    h = x * lax.rsqrt(jnp.mean(x * x, axis=-1, keepdims=True) + RMS_EPS) * g_ref[...]
    rows = h.shape[0]
    first_tile = pl.program_id(0) % tiles_per_seq == 0
    row0 = jnp.where(first_tile, shift_ref[0], carry_scr[...])
    prev = pltpu.roll(h, 1, 0) if rows > 1 else h
    prev = jnp.where(lax.broadcasted_iota(jnp.int32, h.shape, 0) == 0, row0, prev)
    carry_scr[...] = h[rows - 1:rows, :]
    hlast_ref[0] = h[rows - 1:rows, :]
    diff = prev - h
    for i, o_ref in enumerate(out_refs):
        o_ref[0] = (h + diff * mu_ref[i:i + 1, :]).astype(o_ref.dtype)


def _norm_mix_call(x, g, mu, shift0, bt=512):
    B, T, D = x.shape
    bt = min(bt, T)
    assert T % bt == 0
    n_mix = mu.shape[0]
    tiles = T // bt
    seq = pl.BlockSpec((1, bt, D), lambda m: (m // tiles, m % tiles, 0))
    per_seq = pl.BlockSpec((1, 1, D), lambda m: (m // tiles, 0, 0))
    outs = pl.pallas_call(
        functools.partial(_norm_mix_kernel, tiles_per_seq=tiles),
        grid=(B * tiles,),
        in_specs=[seq, pl.BlockSpec((1, D), lambda m: (0, 0)), pl.BlockSpec((n_mix, D), lambda m: (0, 0)),
                  per_seq],
        out_specs=[seq] * n_mix + [per_seq],
        out_shape=[jax.ShapeDtypeStruct((B, T, D), BF16)] * n_mix + [jax.ShapeDtypeStruct((B, 1, D), F32)],
        scratch_shapes=[pltpu.VMEM((1, D), F32)],
        compiler_params=pltpu.CompilerParams(dimension_semantics=("arbitrary",),
                                             vmem_limit_bytes=VMEM_LIMIT),
        name="rwkv_norm_mix",
    )(x, g.reshape(1, D), mu, shift0.reshape(B, 1, D))
    return [o.reshape(B * T, D) for o in outs[:n_mix]], outs[n_mix].reshape(B, D)


def _rwkv_block(x, shift0, wkv0, norm_g, p):
    (mu, w0, w1, w2, a0, a1, a2, g1, g2, k_k, k_a, r_k, lnx_w, lnx_b, w_r, w_k, w_v, w_o) = p
    B, T, D = x.shape
    M = B * T
    x2 = x.reshape(M, D)
    (xr, xw, xk, xv, xa, xg), h_last = _norm_mix_call(x, norm_g, mu, shift0)
    big = dict(bm=1024, bn=1024)
    r = _matmul_call([xr], w_r, name="rwkv_r", **big)
    k = _matmul_call([xk], w_k, name="rwkv_k", **big)
    v = _matmul_call([xv], w_v, name="rwkv_v", **big)
    wlog = _lora_call([xw], w1, w2, [w0.reshape(1, D)], mid=jnp.tanh,
                      post=lambda z, b: -_softplus(-(b + z)) - 0.5, name="rwkv_w")
    a = _lora_call([xa], a1, a2, [a0.reshape(1, D)],
                   post=lambda z, b: jax.nn.sigmoid(b + z), name="rwkv_a")
    g = _lora_call([xg], g1, g2, mid=jax.nn.sigmoid, name="rwkv_g")

    L = WKV_CHUNK
    Tp = -(-T // L) * L
    def seq(z, fill=0.0):
        z = z.reshape(B, T, D)
        if Tp != T:
            z = jnp.pad(z, ((0, 0), (0, Tp - T), (0, 0)), constant_values=fill)
        return z
    y, s2 = _wkv_call(seq(r), seq(k), seq(v), seq(wlog, -jnp.inf), seq(a), seq(g),
                      k_k, k_a, r_k.reshape(D), lnx_w, lnx_b, _pair_states(wkv0), seqs_per_step=1, pairs_per_step=D // LANES)
    y2 = y[:, :T].reshape(M, D)
    out = _matmul_call([y2], w_o, [x2], post=lambda acc, res: res + acc, name="rwkv_o", **big)
    return out.reshape(B, T, D), _unpair_states(s2), h_last


def _ffn_block(x2, norm_g, w_up, w_down, layer):
    mid = _matmul_rows_call([x2, norm_g.reshape(1, -1)], w_up, pre=_rmsnorm,
                            post=lambda acc: jnp.square(jnp.maximum(acc, 0.0)),
                            out_dtype=BF16, bm=2048, bn=512, layer=layer, name="ffn_up")
    return _matmul_call([mid], w_down, [x2], post=lambda acc, res: res + acc,
                        bm=1024, bn=1024, bk=2048, layer=layer, name="ffn_down")


def _attn_block(x, norm_g, w_qkv, w_o, slopes, caches=None):
    B, T, D = x.shape
    M = B * T
    x2 = x.reshape(M, D)
    qkv = _matmul_rows_call([x2, norm_g.reshape(1, -1)], w_qkv, pre=_rmsnorm,
                            bm=2048, bn=512, name="attn_qkv").reshape(B, T, -1)
    hw = H_B * HD_B
    if caches is None:
        merged = _attn_prompt_call(qkv, slopes)
        bufs = [_kv_window_call(qkv, g, min(w, T)) for g, (w, _) in enumerate(GROUPS)]
    else:
        merged, bufs = _attn_decode_call(qkv, caches, slopes)
    out = _matmul_call([merged.reshape(M, hw)], w_o, [x2], post=lambda acc, res: res + acc,
                       bm=1024, name="attn_o")
    return out.reshape(B, T, D), bufs


def kernel(x_prompt, x_sample, state_wkv, state_shift, cache_kv_g1, cache_kv_g2, cache_kv_g3,
           norm_mix, norm_ffn, norm_final,
           rwkv_mu, rwkv_w0, rwkv_w1, rwkv_w2, rwkv_a0, rwkv_a1, rwkv_a2, rwkv_g1, rwkv_g2,
           rwkv_k_k, rwkv_k_a, rwkv_r_k, rwkv_lnx_w, rwkv_lnx_b, rwkv_w_r, rwkv_w_k, rwkv_w_v, rwkv_w_o,
           attn_w_qkv, attn_w_o, ffn_w_up, ffn_w_down):
    rwkv_params = (rwkv_mu, rwkv_w0, rwkv_w1, rwkv_w2, rwkv_a0, rwkv_a1, rwkv_a2, rwkv_g1, rwkv_g2,
                   rwkv_k_k, rwkv_k_a, rwkv_r_k, rwkv_lnx_w, rwkv_lnx_b,
                   rwkv_w_r, rwkv_w_k, rwkv_w_v, rwkv_w_o)
    slopes = _alibi_slopes()
    Bp, Tp, D = x_prompt.shape
    Bs, Ts, _ = x_sample.shape

    def ffn(x, i):
        B, T, _ = x.shape
        return _ffn_block(x.reshape(B * T, D), norm_ffn[i], ffn_w_up, ffn_w_down, i).reshape(B, T, D)

    xp, wkv_p, shift_p = _rwkv_block(x_prompt, jnp.zeros((Bp, D), F32),
                                     jnp.zeros((Bp, H_A, HEAD_A, HEAD_A), F32), norm_mix[0], rwkv_params)
    xs, wkv_s, shift_s = _rwkv_block(x_sample, state_shift, state_wkv, norm_mix[0], rwkv_params)
    xp, xs = ffn(xp, 0), ffn(xs, 0)
    xp, (kv1_p, kv2_p, kv3_p) = _attn_block(xp, norm_mix[1], attn_w_qkv, attn_w_o, slopes)
    xs, (kv1_s, kv2_s, kv3_s) = _attn_block(xs, norm_mix[1], attn_w_qkv, attn_w_o, slopes,
                                            (cache_kv_g1, cache_kv_g2, cache_kv_g3))
    xp, xs = ffn(xp, 1), ffn(xs, 1)
    y_prompt = _rmsnorm_call(xp.reshape(Bp * Tp, D), norm_final).reshape(Bp, Tp, D)
    y_sample = _rmsnorm_call(xs.reshape(Bs * Ts, D), norm_final).reshape(Bs, Ts, D)
    return (y_prompt, y_sample, wkv_p, shift_p, kv1_p, kv2_p, kv3_p,
            wkv_s, shift_s, kv1_s, kv2_s, kv3_s)
```

```python
import functools

import jax
import jax.numpy as jnp
from jax import lax
from jax.experimental import pallas as pl
from jax.experimental.pallas import tpu as pltpu

F32 = jnp.float32
BF16 = jnp.bfloat16

D_MODEL = 2048
HEAD_A = 64
H_A = D_MODEL // HEAD_A
GN_EPS = 64e-5
GROUPS = ((128, 1), (512, 4), (2048, 16))
N_GROUPS = len(GROUPS)
H_B = 16
HD_B = 64
ATT_STEPS = 128
RMS_EPS = 1e-6

LANES = 128
VMEM_LIMIT = 56 * 1024 * 1024


def _bdot(a, b):
    return jnp.dot(a.astype(BF16), b.astype(BF16), preferred_element_type=F32)


def _bdot_nt(a, b):
    return lax.dot_general(a.astype(BF16), b.astype(BF16), (((1,), (1,)), ((), ())),
                           preferred_element_type=F32)


def _bdot_tn(a, b):
    return lax.dot_general(a.astype(BF16), b.astype(BF16), (((0,), (0,)), ((), ())),
                           preferred_element_type=F32)


def _exact_dot(m, x):
    hi = x.astype(BF16)
    r1 = x - hi.astype(F32)
    mid = r1.astype(BF16)
    lo = (r1 - mid.astype(F32)).astype(BF16)
    n = x.shape[1]
    parts = jnp.dot(m.astype(BF16), jnp.concatenate([hi, mid, lo], axis=1), preferred_element_type=F32)
    return (parts[:, :n] + parts[:, n:2 * n]) + parts[:, 2 * n:]


def _lockstep(chains):
    results = [None] * len(chains)
    live = list(range(len(chains)))
    while live:
        for idx in list(live):
            try:
                next(chains[idx])
            except StopIteration as done:
                results[idx] = done.value
                live.remove(idx)
    return results


def _head_sum(x, lane_lo):
    s0 = jnp.sum(jnp.where(lane_lo, x, 0.0), axis=-1, keepdims=True)
    s1 = jnp.sum(jnp.where(lane_lo, 0.0, x), axis=-1, keepdims=True)
    return jnp.where(lane_lo, s0, s1)


WKV_CHUNK = HEAD_A
(_MK_EYE, _MK_LVL0, _MK_BD, _MK_NMASK, _MK_CAUSAL, _MK_TRI, _MK_LEVELS) = range(7)
_N_LEVELS = 5
_N_MASKS = _MK_LEVELS + _N_LEVELS


def _wkv_fill_masks(mk_ref):
    L = WKV_CHUNK
    r2 = lax.broadcasted_iota(jnp.int32, (LANES, LANES), 0)
    c2 = lax.broadcasted_iota(jnp.int32, (LANES, LANES), 1)
    f = lambda m: m.astype(F32)
    same_block = (r2 < L) == (c2 < L)
    strict = (c2 % L) < (r2 % L)
    incl = (c2 % L) <= (r2 % L)
    mk_ref[_MK_EYE] = f(r2 == c2)
    mk_ref[_MK_LVL0] = f((r2 // 2 == c2 // 2) & (c2 < r2))
    mk_ref[_MK_BD] = f(same_block)
    mk_ref[_MK_NMASK] = f(strict & same_block)
    mk_ref[_MK_CAUSAL] = f(strict | (incl & (r2 >= L)))
    mk_ref[_MK_TRI] = f(incl & (r2 < L) & (c2 < L))
    size = 2
    for lvl in range(_N_LEVELS):
        mk_ref[_MK_LEVELS + lvl] = f((r2 // (2 * size) == c2 // (2 * size))
                                     & ((r2 // size) % 2 == 1) & ((c2 // size) % 2 == 0) & same_block)
        size *= 2


def _wkv_pair_chunk(r, kraw, v, wl, a, g, k_k, k_a, r_k, lnx_w, lnx_b, s2, mk_ref):
    L = WKV_CHUNK
    lane_lo = lax.broadcasted_iota(jnp.int32, (L, LANES), 1) < HEAD_A
    lo2 = lax.broadcasted_iota(jnp.int32, (2 * L, LANES), 1) < HEAD_A

    logd = -jnp.exp(wl)
    kkraw = kraw * k_k
    kk = kkraw * lax.rsqrt(jnp.maximum(_head_sum(kkraw * kkraw, lane_lo), 1e-24))
    k2 = kraw * (1.0 + (a - 1.0) * k_a)

    cum = _exact_dot(mk_ref[_MK_TRI][:L, :L], logd)
    w_incl = jnp.exp(cum)
    w_excl = jnp.exp(cum - logd)
    w_inv = jnp.exp(-cum)
    w_last = w_incl[L - 1:L, :]

    at = -kk * w_excl
    bt = (kk * a) * w_inv
    kt = k2 * w_inv
    rt = r * w_incl

    ar = jnp.concatenate([at, rt], axis=0)
    bk = jnp.concatenate([bt, kt], axis=0)
    kb = jnp.concatenate([kt, bt], axis=0)
    mm = _bdot_nt(jnp.concatenate([jnp.where(lo2, ar, 0.0), jnp.where(lo2, 0.0, ar)], axis=0),
                  jnp.concatenate([bk, kb], axis=0))
    m0 = mm[:2 * L, :2 * L]
    m1 = mm[2 * L:, 2 * L:]
    aprp = _bdot_nt(ar, s2)
    yield

    a0 = m0[:L]
    a1 = m1[:L]
    n2 = jnp.concatenate([a0, a1], axis=0) * mk_ref[_MK_NMASK]
    causal = mk_ref[_MK_CAUSAL]
    strict, incl = causal[:L], causal[L:]
    v_lo = jnp.where(lane_lo, v, 0.0)
    v_hi = jnp.where(lane_lo, 0.0, v)
    ak_cat = jnp.where(lane_lo, a1, a0) * strict

    t2 = mk_ref[_MK_EYE] + n2 * mk_ref[_MK_LVL0]
    for lvl in range(_N_LEVELS):
        tc = _bdot(t2, n2 * mk_ref[_MK_LEVELS + lvl])
        yield
        t2 = t2 + _bdot(tc, t2)
        yield
    tcat = t2[:L] + t2[L:]
    rhs = aprp[:L] + _bdot(ak_cat, jnp.concatenate([v_hi, v_lo], axis=0))
    yield

    u = _bdot(tcat, jnp.concatenate([jnp.where(lane_lo, rhs, 0.0), jnp.where(lane_lo, 0.0, rhs)], axis=0))
    yield
    u_lo = jnp.where(lane_lo, u, 0.0)
    u_hi = jnp.where(lane_lo, 0.0, u)

    y = aprp[L:] + _bdot(jnp.concatenate([m0[L:] * incl, m1[L:] * incl], axis=1),
                         jnp.concatenate([u_lo, v_lo, v_hi, u_hi], axis=0))
    delta = _bdot_tn(jnp.concatenate([u, v], axis=0), bk)
    yield
    s_new = (s2 + delta * mk_ref[_MK_BD]) * w_last

    mean = _head_sum(y, lane_lo) * (1.0 / HEAD_A)
    yc = y - mean
    var = _head_sum(yc * yc, lane_lo) * (1.0 / HEAD_A)
    yn = yc * lax.rsqrt(var + GN_EPS) * lnx_w + lnx_b
    bonus = _head_sum(r * k2 * r_k, lane_lo)
    return (yn + bonus * v) * g, s_new


def _wkv_kernel(r_ref, k_ref, v_ref, wl_ref, a_ref, g_ref, kk_ref, ka_ref, rk_ref, lw_ref, lb_ref,
                s0_ref, y_ref, sout_ref, s_scr, mk_scr):
    b, c = pl.program_id(0), pl.program_id(1)

    @pl.when((b == 0) & (c == 0))
    def _():
        _wkv_fill_masks(mk_scr)

    @pl.when(c == 0)
    def _():
        s_scr[...] = s0_ref[0]

    n_pairs = s_scr.shape[0]
    lanes = [slice(p * LANES, (p + 1) * LANES) for p in range(n_pairs)]
    chains = [_wkv_pair_chunk(
        r_ref[0, :, sl], k_ref[0, :, sl], v_ref[0, :, sl], wl_ref[0, :, sl], a_ref[0, :, sl],
        g_ref[0, :, sl], kk_ref[:, sl], ka_ref[:, sl], rk_ref[:, sl], lw_ref[:, sl], lb_ref[:, sl],
        s_scr[p], mk_scr) for p, sl in enumerate(lanes)]
    for (y, s_new), p, sl in zip(_lockstep(chains), range(n_pairs), lanes):
        y_ref[0, :, sl] = y.astype(y_ref.dtype)
        s_scr[p] = s_new
        sout_ref[0, p] = s_new


def _wkv_call(r, k, v, wlog, a, g, k_k, k_a, r_k, lnx_w, lnx_b, s2_0):
    B, T, D = r.shape
    L, P = WKV_CHUNK, D // LANES
    assert T % L == 0 and 2 * L == LANES and D % LANES == 0
    seq = pl.BlockSpec((1, L, D), lambda b, c: (b, c, 0))
    par = pl.BlockSpec((1, D), lambda b, c: (0, 0))
    st = pl.BlockSpec((1, P, LANES, LANES), lambda b, c: (b, 0, 0, 0))
    row = lambda x: x.reshape(1, D)
    return pl.pallas_call(
        _wkv_kernel,
        grid=(B, T // L),
        in_specs=[seq] * 6 + [par] * 5 + [st],
        out_specs=[seq, st],
        out_shape=[jax.ShapeDtypeStruct((B, T, D), BF16), jax.ShapeDtypeStruct((B, P, LANES, LANES), F32)],
        scratch_shapes=[pltpu.VMEM((P, LANES, LANES), F32), pltpu.VMEM((_N_MASKS, LANES, LANES), F32)],
        compiler_params=pltpu.CompilerParams(dimension_semantics=("arbitrary", "arbitrary")),
        name="wkv7_chunk_scan",
    )(r, k, v, wlog, a, g, row(k_k), row(k_a), row(r_k), row(lnx_w), row(lnx_b), s2_0)


def _rmsnorm(x, g):
    return x * lax.rsqrt(jnp.mean(x * x, axis=-1, keepdims=True) + RMS_EPS) * g


def _rmsnorm_kernel(x_ref, g_ref, o_ref):
    o_ref[...] = _rmsnorm(x_ref[...], g_ref[...]).astype(o_ref.dtype)


def _rmsnorm_call(x, g, out_dtype=F32):
    M, D = x.shape
    bm = min(M, 512)
    return pl.pallas_call(
        _rmsnorm_kernel,
        grid=(M // bm,),
        in_specs=[pl.BlockSpec((bm, D), lambda m: (m, 0)), pl.BlockSpec((1, D), lambda m: (0, 0))],
        out_specs=pl.BlockSpec((bm, D), lambda m: (m, 0)),
        out_shape=jax.ShapeDtypeStruct((M, D), out_dtype),
        compiler_params=pltpu.CompilerParams(dimension_semantics=("parallel",)),
        name="rmsnorm",
    )(x, g.reshape(1, D))


MM_ROW_CHUNK = 256


def _mm_kernel(x_ref, w_ref, *refs, n_e, post, nk):
    e_refs = refs[:n_e]
    o_ref = refs[n_e]
    bm = o_ref.shape[0]
    acc_ref = refs[-1] if nk > 1 else None
    k = pl.program_id(2)
    if nk > 1:
        @pl.when(k == 0)
        def _():
            acc_ref[...] = jnp.zeros_like(acc_ref)

    w = w_ref[...].astype(BF16)
    chunk = min(bm, MM_ROW_CHUNK)
    for r0 in range(0, bm, chunk):
        rows = slice(r0, r0 + chunk)
        tile = lambda ref: ref[rows, :] if ref.shape[0] == bm else ref[...]
        part = jnp.dot(x_ref[rows, :].astype(BF16), w, preferred_element_type=F32)
        if nk == 1:
            o_ref[rows, :] = post(part, *[tile(e) for e in e_refs]).astype(o_ref.dtype)
        else:
            acc_ref[rows, :] += part

    if nk > 1:
        @pl.when(k == nk - 1)
        def _():
            o_ref[...] = post(acc_ref[...], *[e[...] for e in e_refs]).astype(o_ref.dtype)


def _mm_norm_kernel(x_ref, g_ref, w_ref, o_ref, x_scr, *, post):
    @pl.when(pl.program_id(1) == 0)
    def _():
        x_scr[...] = _rmsnorm(x_ref[...], g_ref[...]).astype(BF16)

    bm = o_ref.shape[0]
    w = w_ref[...].astype(BF16)
    chunk = min(bm, MM_ROW_CHUNK)
    for r0 in range(0, bm, chunk):
        rows = slice(r0, r0 + chunk)
        part = jnp.dot(x_scr[rows, :], w, preferred_element_type=F32)
        o_ref[rows, :] = post(part).astype(o_ref.dtype)


def _identity(x):
    return x


def _add_residual(acc, res):
    return res + acc


def _weight_spec(block, index, layer):
    if layer is None:
        return pl.BlockSpec(block, index)
    return pl.BlockSpec((None,) + block, lambda *ids: (layer,) + index(*ids))


def _matmul_norm_call(x, g, w, *, post=_identity, out_dtype=F32, bm=2048, bn=512, layer=None, name="matmul"):
    K, N = w.shape[-2:]
    M = x.shape[0]
    bm = min(bm, M)
    bn = min(bn, N)
    assert M % bm == 0 and N % bn == 0
    return pl.pallas_call(
        functools.partial(_mm_norm_kernel, post=post),
        grid=(M // bm, N // bn),
        in_specs=[pl.BlockSpec((bm, K), lambda m, n: (m, 0), pipeline_mode=pl.Buffered(1)),
                  pl.BlockSpec((1, K), lambda m, n: (0, 0)),
                  _weight_spec((K, bn), lambda m, n: (0, n), layer)],
        out_specs=pl.BlockSpec((bm, bn), lambda m, n: (m, n)),
        out_shape=jax.ShapeDtypeStruct((M, N), out_dtype),
        scratch_shapes=[pltpu.VMEM((bm, K), BF16)],
        compiler_params=pltpu.CompilerParams(
            dimension_semantics=("parallel", "arbitrary"), vmem_limit_bytes=VMEM_LIMIT),
        name=name,
    )(x, g.reshape(1, K), w)


def _matmul_call(x, w, extras=(), *, post=_identity, out_dtype=F32,
                 bm=1024, bn=1024, bk=None, layer=None, name="matmul"):
    K, N = w.shape[-2:]
    M = x.shape[0]
    bm = min(bm, M)
    bn = min(bn, N)
    bk = K if bk is None else min(bk, K)
    assert M % bm == 0 and N % bn == 0 and K % bk == 0
    nk = K // bk
    e_specs = [pl.BlockSpec((bm, bn), lambda n, m, k: (m, n)) if e.shape[0] == M
               else pl.BlockSpec((1, bn), lambda n, m, k: (0, n)) for e in extras]
    return pl.pallas_call(
        functools.partial(_mm_kernel, n_e=len(extras), post=post, nk=nk),
        grid=(N // bn, M // bm, nk),
        in_specs=[pl.BlockSpec((bm, bk), lambda n, m, k: (m, k)),
                  _weight_spec((bk, bn), lambda n, m, k: (k, n), layer)] + e_specs,
        out_specs=pl.BlockSpec((bm, bn), lambda n, m, k: (m, n)),
        out_shape=jax.ShapeDtypeStruct((M, N), out_dtype),
        scratch_shapes=[pltpu.VMEM((bm, bn), F32)] if nk > 1 else [],
        compiler_params=pltpu.CompilerParams(
            dimension_semantics=("parallel", "parallel", "arbitrary"),
            vmem_limit_bytes=VMEM_LIMIT),
        name=name,
    )(x, w, *extras)


ATT_BLOCK = ATT_STEPS * max(d for _, d in GROUPS)
ATT_LANES = 4


def _attn_prompt_kernel(slopes_ref, *refs):
    ng = N_GROUPS
    q_refs, kc_refs, vc_refs = refs[0:ng], refs[ng:2 * ng], refs[2 * ng:3 * ng]
    kp_refs, vp_refs = refs[3 * ng:4 * ng], refs[4 * ng:5 * ng]
    o_ref = refs[5 * ng]
    m_scr, l_scr, acc_scr = refs[5 * ng + 1:]
    hp = pl.program_id(1)
    j = pl.program_id(2)
    S = ATT_STEPS
    lane_lo = lax.broadcasted_iota(jnp.int32, (S, LANES), 1) < HD_B
    iq = lax.broadcasted_iota(jnp.int32, (S, 2 * S), 0)
    jk = lax.broadcasted_iota(jnp.int32, (S, 2 * S), 1)
    steps = iq + S - jk
    mask = jnp.where((steps >= 0) & (steps <= S), 0.0, -jnp.inf)
    scale = HD_B ** -0.5
    no_prev = jnp.where(j > 0, 0.0, -jnp.inf)

    order = sorted(range(ng), key=lambda g: -GROUPS[g][1])
    for g in order:
        dil = GROUPS[g][1]
        biases = [mask - slopes_ref[g * H_B + 2 * hp + h] * (steps * dil).astype(F32) for h in range(2)]
        q_ref, kc_ref, vc_ref, kp_ref, vp_ref = q_refs[g], kc_refs[g], vc_refs[g], kp_refs[g], vp_refs[g]

        def attend(cur_start, prev_ref, prev_start, prev_bias, dil=dil, g=g, biases=biases, q_ref=q_ref,
                   kc_ref=kc_ref, vc_ref=vc_ref):
            cur = pl.ds(cur_start, S, stride=dil) if dil > 1 else pl.ds(cur_start, S)
            prv = pl.ds(prev_start, S, stride=dil) if dil > 1 else pl.ds(prev_start, S)
            q, kc, vc = q_ref[0, cur, :], kc_ref[0, cur, :], vc_ref[0, cur, :]
            kp, vp = prev_ref[0][0, prv, :], prev_ref[1][0, prv, :]
            qs = [jnp.where(lane_lo, q, 0.0), jnp.where(lane_lo, 0.0, q)]
            scores = [(_bdot_nt(qh, kp), _bdot_nt(qh, kc)) for qh in qs]
            yield
            ms, ls, probs = [], [], []
            for h in range(2):
                sp = scores[h][0] * scale + biases[h][:, :S]
                if prev_bias is not None:
                    sp = sp + prev_bias
                sc = scores[h][1] * scale + biases[h][:, S:]
                m = jnp.max(jnp.maximum(sp, sc), axis=-1, keepdims=True)
                pp = jnp.exp(sp - m)
                pc = jnp.exp(sc - m)
                ls.append(jnp.sum(pp + pc, axis=-1, keepdims=True))
                probs.append((pp, pc))
                ms.append(m)
            accs = [_bdot(pp, vp) + _bdot(pc, vc) for pp, pc in probs]
            yield
            m = jnp.where(lane_lo, ms[0], ms[1])
            l = jnp.where(lane_lo, ls[0], ls[1])
            acc = jnp.where(lane_lo, accs[0], accs[1])
            if g != order[0]:
                m_old = m_scr[cur, :]
                m_new = jnp.maximum(m_old, m)
                w_old, w_new = jnp.exp(m_old - m_new), jnp.exp(m - m_new)
                l = l_scr[cur, :] * w_old + l * w_new
                acc = acc_scr[cur, :] * w_old + acc * w_new
                m = m_new
            m_scr[cur, :] = m
            l_scr[cur, :] = l
            acc_scr[cur, :] = acc

        sd = S * dil
        n_sub = ATT_BLOCK // sd
        run = lambda items, attend=attend: _lockstep([attend(*it) for it in items])
        first = lambda c, kp_ref=kp_ref, vp_ref=vp_ref: (c, (kp_ref, vp_ref), c, no_prev)
        later = lambda c, i, sd=sd, kc_ref=kc_ref, vc_ref=vc_ref: (
            i * sd + c, (kc_ref, vc_ref), (i - 1) * sd + c, None)

        def loop(lo, hi, fn):
            def body(idx, carry):
                fn(idx)
                return carry
            lax.fori_loop(lo, hi, body, 0)

        W = ATT_LANES if dil == 1 else min(ATT_LANES, dil)
        if dil == 1:
            hb = n_sub // W
            run([first(0)] + [later(0, k * hb) for k in range(1, W)])
            loop(1, hb, lambda i, hb=hb: run([later(0, i + k * hb) for k in range(W)]))
        else:
            cs = dil // W
            def streams(c, cs=cs, n_sub=n_sub):
                run([first(c + k * cs) for k in range(W)])
                if n_sub > 1:
                    loop(1, n_sub, lambda i: run([later(c + k * cs, i) for k in range(W)]))
            if cs == 1:
                streams(0)
            else:
                loop(0, cs, streams)

    o_ref[0] = (acc_scr[...] / l_scr[...]).astype(o_ref.dtype)


def _attn_prompt_call(qkv, slopes):
    B, T, W = qkv.shape
    hw = H_B * HD_B
    n_hp = H_B // 2
    TB = ATT_BLOCK
    assert T % TB == 0
    col = lambda g, which, hp: (g * 3 * hw + which * hw) // LANES + hp
    cur = lambda g, which: pl.BlockSpec((1, TB, LANES), lambda b, hp, j: (b, j, col(g, which, hp)))
    def prev(g, which):
        rows = ATT_STEPS * GROUPS[g][1]
        per = TB // rows
        return pl.BlockSpec((1, rows, LANES),
                            lambda b, hp, j: (b, jnp.maximum(j * per - 1, 0), col(g, which, hp)))
    gs = range(N_GROUPS)
    in_specs = ([pl.BlockSpec(memory_space=pltpu.SMEM)]
                + [cur(g, 0) for g in gs] + [cur(g, 1) for g in gs] + [cur(g, 2) for g in gs]
                + [prev(g, 1) for g in gs] + [prev(g, 2) for g in gs])
    return pl.pallas_call(
        _attn_prompt_kernel,
        grid=(B, n_hp, T // TB),
        in_specs=in_specs,
        out_specs=pl.BlockSpec((1, TB, LANES), lambda b, hp, j: (b, j, hp)),
        out_shape=jax.ShapeDtypeStruct((B, T, hw), BF16),
        scratch_shapes=[pltpu.VMEM((TB, LANES), F32)] * 3,
        compiler_params=pltpu.CompilerParams(
            dimension_semantics=("parallel", "parallel", "arbitrary"), vmem_limit_bytes=VMEM_LIMIT),
        name="attn_prompt",
    )(slopes, *([qkv] * (5 * N_GROUPS)))


def _kv_window_kernel(x_ref, o_ref):
    for hp in range(H_B // 2):
        sl = slice(hp * LANES, (hp + 1) * LANES)
        o_ref[0, 0, sl, :] = x_ref[0, :, sl].T


def _kv_window_call(qkv, g, window):
    B, T, _ = qkv.shape
    hw = H_B * HD_B
    wb = min(window, 512)
    first = (T - window) // wb
    out = pl.pallas_call(
        _kv_window_kernel,
        grid=(B, 2, window // wb),
        in_specs=[pl.BlockSpec((1, wb, hw), lambda b, kv, i: (b, first + i, 3 * g + 1 + kv))],
        out_specs=pl.BlockSpec((1, 1, hw, wb), lambda b, kv, i: (b, kv, 0, i)),
        out_shape=jax.ShapeDtypeStruct((B, 2, hw, window), F32),
        compiler_params=pltpu.CompilerParams(dimension_semantics=("parallel", "parallel", "parallel")),
        name=f"kv_window_g{g}",
    )(qkv)
    return out.reshape(B, 2, H_B, HD_B, window).transpose(0, 4, 1, 2, 3)


def _roll_append(old, new, o_ref, kv):
    R, W = old.shape
    S = new.shape[0]
    rolled = pltpu.roll(old, W - S, 1)
    new_t = jnp.concatenate([jnp.zeros((LANES - S, R), F32), new], axis=0).T
    if W > LANES:
        o_ref[0, kv, :, :W - LANES] = rolled[:, :W - LANES]
    lane = lax.broadcasted_iota(jnp.int32, (R, LANES), 1)
    o_ref[0, kv, :, W - LANES:] = jnp.where(lane >= LANES - S, new_t, rolled[:, W - LANES:])


def _attn_decode_kernel(slopes_ref, *refs):
    ng = N_GROUPS
    q_refs, k_refs, v_refs = refs[0:ng], refs[ng:2 * ng], refs[2 * ng:3 * ng]
    kt_refs, vt_refs = refs[3 * ng:4 * ng], refs[4 * ng:5 * ng]
    o_ref = refs[5 * ng]
    roll_refs = refs[5 * ng + 1:6 * ng + 1]
    hp = pl.program_id(1)
    S = q_refs[0].shape[1]
    lane_lo = lax.broadcasted_iota(jnp.int32, (S, LANES), 1) < HD_B
    scale = HD_B ** -0.5

    for kv, (new_refs, old_refs) in enumerate(((k_refs, kt_refs), (v_refs, vt_refs))):
        for g in range(ng):
            _roll_append(old_refs[g][0, 0], new_refs[g][0], roll_refs[g], kv)

    def chain(g, h):
        window, dil = GROUPS[g]
        slope = slopes_ref[g * H_B + 2 * hp + h]
        q = q_refs[g][0]
        qh = jnp.where(lane_lo, q, 0.0) if h == 0 else jnp.where(lane_lo, 0.0, q)
        s_old = _bdot(qh, kt_refs[g][0, 0])
        s_new = _bdot_nt(qh, k_refs[g][0])
        yield
        j_o = lax.broadcasted_iota(jnp.int32, (S, window), 0)
        w_o = lax.broadcasted_iota(jnp.int32, (S, window), 1)
        dist_o = window + j_o - w_o
        ok_o = ((dist_o & (dil - 1)) == 0) & (dist_o <= window)
        j_n = lax.broadcasted_iota(jnp.int32, (S, S), 0)
        dist_n = j_n - lax.broadcasted_iota(jnp.int32, (S, S), 1)
        ok_n = ((dist_n & (dil - 1)) == 0) & (dist_n >= 0)
        s_old = jnp.where(ok_o, s_old * scale - slope * dist_o.astype(F32), -jnp.inf)
        s_new = jnp.where(ok_n, s_new * scale - slope * dist_n.astype(F32), -jnp.inf)
        m = jnp.maximum(jnp.max(s_old, axis=-1, keepdims=True), jnp.max(s_new, axis=-1, keepdims=True))
        p_old = jnp.exp(s_old - m)
        p_new = jnp.exp(s_new - m)
        l = jnp.sum(p_old, axis=-1, keepdims=True) + jnp.sum(p_new, axis=-1, keepdims=True)
        acc = _bdot_nt(p_old, vt_refs[g][0, 0]) + _bdot(p_new, v_refs[g][0])
        yield
        return m, l, acc

    res = _lockstep([chain(g, h) for g in range(ng) for h in range(2)])
    m = l = acc = None
    for g in range(ng):
        (m0, l0, a0), (m1, l1, a1) = res[2 * g], res[2 * g + 1]
        mg = jnp.where(lane_lo, m0, m1)
        lg = jnp.where(lane_lo, l0, l1)
        ag = jnp.where(lane_lo, a0, a1)
        if g == 0:
            m, l, acc = mg, lg, ag
        else:
            m_new = jnp.maximum(m, mg)
            w_old, w_new = jnp.exp(m - m_new), jnp.exp(mg - m_new)
            l = l * w_old + lg * w_new
            acc = acc * w_old + ag * w_new
            m = m_new
    o_ref[0] = (acc / l).astype(o_ref.dtype)


def _attn_decode_call(qkv, caches, slopes):
    B, S, _ = qkv.shape
    hw = H_B * HD_B
    n_hp = H_B // 2
    assert all(w >= S and d & (d - 1) == 0 for w, d in GROUPS)
    cache_t = [c.transpose(0, 2, 3, 4, 1).reshape(B, 2, hw, c.shape[1]) for c in caches]
    col = lambda g, which, hp: (g * 3 * hw + which * hw) // LANES + hp
    new = lambda g, which: pl.BlockSpec((1, S, LANES), lambda b, hp: (b, 0, col(g, which, hp)))
    old = lambda g, kv: pl.BlockSpec((1, 1, LANES, GROUPS[g][0]), lambda b, hp: (b, kv, hp, 0))
    gs = range(N_GROUPS)
    in_specs = ([pl.BlockSpec(memory_space=pltpu.SMEM)]
                + [new(g, 0) for g in gs] + [new(g, 1) for g in gs] + [new(g, 2) for g in gs]
                + [old(g, 0) for g in gs] + [old(g, 1) for g in gs])
    outs = pl.pallas_call(
        _attn_decode_kernel,
        grid=(B, n_hp),
        in_specs=in_specs,
        out_specs=([pl.BlockSpec((1, S, LANES), lambda b, hp: (b, 0, hp))]
                   + [pl.BlockSpec((1, 2, LANES, w), lambda b, hp: (b, 0, hp, 0)) for w, _ in GROUPS]),
        out_shape=([jax.ShapeDtypeStruct((B, S, hw), BF16)]
                   + [jax.ShapeDtypeStruct((B, 2, hw, w), F32) for w, _ in GROUPS]),
        compiler_params=pltpu.CompilerParams(dimension_semantics=("parallel", "parallel")),
        name="attn_decode",
    )(slopes, *([qkv] * (3 * N_GROUPS)), *cache_t, *cache_t)
    rolled = [r.reshape(B, 2, H_B, HD_B, r.shape[-1]).transpose(0, 4, 1, 2, 3) for r in outs[1:]]
    return outs[0], rolled


def _alibi_slopes():
    n = N_GROUPS * H_B
    return 2.0 ** (-8.0 * jnp.arange(1, n + 1, dtype=F32) / n)


def _pair_states(s):
    B, H = s.shape[:2]
    s = s.reshape(B, H // 2, 2, HEAD_A, HEAD_A)
    z = jnp.zeros_like(s[:, :, 0])
    top = jnp.concatenate([s[:, :, 0], z], axis=-1)
    bot = jnp.concatenate([z, s[:, :, 1]], axis=-1)
    return jnp.concatenate([top, bot], axis=-2)


def _unpair_states(s2):
    B, P = s2.shape[:2]
    return jnp.stack([s2[:, :, :HEAD_A, :HEAD_A], s2[:, :, HEAD_A:, HEAD_A:]],
                     axis=2).reshape(B, 2 * P, HEAD_A, HEAD_A)


def _softplus(x):
    return jnp.maximum(x, 0.0) + jnp.log(1.0 + jnp.exp(-jnp.abs(x)))


def _norm_mix_kernel(x_ref, g_ref, mu_ref, shift_ref, w0_ref, w1_ref, w2_ref, a0_ref, a1_ref, a2_ref,
                     g1_ref, g2_ref, xr_ref, xk_ref, xv_ref, wlog_ref, a_ref, gate_ref, hlast_ref, carry_scr,
                     *, tiles_per_seq):
    @pl.when(pl.program_id(0) == 0)
    def _():
        carry_scr[...] = jnp.zeros_like(carry_scr)

    x = x_ref[0]
    h = x * lax.rsqrt(jnp.mean(x * x, axis=-1, keepdims=True) + RMS_EPS) * g_ref[...]
    rows = h.shape[0]
    first_tile = pl.program_id(0) % tiles_per_seq == 0
    row0 = jnp.where(first_tile, shift_ref[0], carry_scr[...])
    prev = pltpu.roll(h, 1, 0) if rows > 1 else h
    prev = jnp.where(lax.broadcasted_iota(jnp.int32, h.shape, 0) == 0, row0, prev)
    carry_scr[...] = h[rows - 1:rows, :]
    hlast_ref[0] = h[rows - 1:rows, :]
    diff = prev - h
    mix = lambda i: h + diff * mu_ref[i:i + 1, :]
    xr_ref[0] = mix(0).astype(BF16)
    xk_ref[0] = mix(2).astype(BF16)
    xv_ref[0] = mix(3).astype(BF16)
    w_pre = w0_ref[...] + _bdot(jnp.tanh(_bdot(mix(1), w1_ref[...])), w2_ref[...])
    wlog_ref[0] = -_softplus(-w_pre) - 0.5
    a_ref[0] = jax.nn.sigmoid(a0_ref[...] + _bdot(_bdot(mix(4), a1_ref[...]), a2_ref[...]))
    gate_ref[0] = _bdot(jax.nn.sigmoid(_bdot(mix(5), g1_ref[...])), g2_ref[...])


def _norm_mix_call(x, g, mu, shift0, w0, w1, w2, a0, a1, a2, g1, g2, bt=256):
    B, T, D = x.shape
    bt = min(bt, T)
    assert T % bt == 0
    tiles = T // bt
    seq = pl.BlockSpec((1, bt, D), lambda m: (m // tiles, m % tiles, 0))
    per_seq = pl.BlockSpec((1, 1, D), lambda m: (m // tiles, 0, 0))
    whole = lambda z: pl.BlockSpec(z.shape, lambda m: (0,) * z.ndim)
    row = lambda z: z.reshape(1, D)
    consts = [row(g), mu]
    lora = [row(w0), w1, w2, row(a0), a1, a2, g1, g2]
    outs = pl.pallas_call(
        functools.partial(_norm_mix_kernel, tiles_per_seq=tiles),
        grid=(B * tiles,),
        in_specs=[seq] + [whole(z) for z in consts] + [per_seq] + [whole(z) for z in lora],
        out_specs=[seq] * 6 + [per_seq],
        out_shape=([jax.ShapeDtypeStruct((B, T, D), BF16)] * 3 + [jax.ShapeDtypeStruct((B, T, D), F32)] * 3
                   + [jax.ShapeDtypeStruct((B, 1, D), F32)]),
        scratch_shapes=[pltpu.VMEM((1, D), F32)],
        compiler_params=pltpu.CompilerParams(dimension_semantics=("arbitrary",),
                                             vmem_limit_bytes=VMEM_LIMIT),
        name="rwkv_norm_mix",
    )(x, *consts, shift0.reshape(B, 1, D), *lora)
    return [o.reshape(B * T, D) for o in outs[:6]], outs[6].reshape(B, D)


def _rwkv_block(x, shift0, wkv0, norm_g, p):
    (mu, w0, w1, w2, a0, a1, a2, g1, g2, k_k, k_a, r_k, lnx_w, lnx_b, w_r, w_k, w_v, w_o) = p
    B, T, D = x.shape
    M = B * T
    x2 = x.reshape(M, D)
    (xr, xk, xv, wlog, a, g), h_last = _norm_mix_call(x, norm_g, mu, shift0, w0, w1, w2, a0, a1, a2, g1, g2)
    r = _matmul_call(xr, w_r, name="rwkv_r")
    k = _matmul_call(xk, w_k, name="rwkv_k")
    v = _matmul_call(xv, w_v, name="rwkv_v")

    L = WKV_CHUNK
    Tp = -(-T // L) * L
    def seq(z, fill=0.0):
        z = z.reshape(B, T, D)
        if Tp != T:
            z = jnp.pad(z, ((0, 0), (0, Tp - T), (0, 0)), constant_values=fill)
        return z
    y, s2 = _wkv_call(seq(r), seq(k), seq(v), seq(wlog, -jnp.inf), seq(a), seq(g),
                      k_k, k_a, r_k.reshape(D), lnx_w, lnx_b, _pair_states(wkv0))
    y2 = y[:, :T].reshape(M, D)
    out = _matmul_call(y2, w_o, [x2], post=_add_residual, name="rwkv_o")
    return out.reshape(B, T, D), _unpair_states(s2), h_last


def _ffn_block(x2, norm_g, w_up, w_down, layer):
    mid = _matmul_norm_call(x2, norm_g, w_up, post=lambda acc: jnp.square(jnp.maximum(acc, 0.0)),
                            out_dtype=BF16, layer=layer, name="ffn_up")
    return _matmul_call(mid, w_down, [x2], post=_add_residual, bk=2048, layer=layer, name="ffn_down")


def _attn_block(x, norm_g, w_qkv, w_o, slopes, caches=None):
    B, T, D = x.shape
    M = B * T
    x2 = x.reshape(M, D)
    qkv = _matmul_norm_call(x2, norm_g, w_qkv, name="attn_qkv").reshape(B, T, -1)
    hw = H_B * HD_B
    if caches is None:
        merged = _attn_prompt_call(qkv, slopes)
        bufs = [_kv_window_call(qkv, g, min(w, T)) for g, (w, _) in enumerate(GROUPS)]
    else:
        merged, bufs = _attn_decode_call(qkv, caches, slopes)
    out = _matmul_call(merged.reshape(M, hw), w_o, [x2], post=_add_residual, name="attn_o")
    return out.reshape(B, T, D), bufs


def kernel(x_prompt, x_sample, state_wkv, state_shift, cache_kv_g1, cache_kv_g2, cache_kv_g3,
           norm_mix, norm_ffn, norm_final,
           rwkv_mu, rwkv_w0, rwkv_w1, rwkv_w2, rwkv_a0, rwkv_a1, rwkv_a2, rwkv_g1, rwkv_g2,
           rwkv_k_k, rwkv_k_a, rwkv_r_k, rwkv_lnx_w, rwkv_lnx_b, rwkv_w_r, rwkv_w_k, rwkv_w_v, rwkv_w_o,
           attn_w_qkv, attn_w_o, ffn_w_up, ffn_w_down):
    rwkv_params = (rwkv_mu, rwkv_w0, rwkv_w1, rwkv_w2, rwkv_a0, rwkv_a1, rwkv_a2, rwkv_g1, rwkv_g2,
                   rwkv_k_k, rwkv_k_a, rwkv_r_k, rwkv_lnx_w, rwkv_lnx_b,
                   rwkv_w_r, rwkv_w_k, rwkv_w_v, rwkv_w_o)
    slopes = _alibi_slopes()
    Bp, Tp, D = x_prompt.shape
    Bs, Ts, _ = x_sample.shape

    def ffn(x, i):
        B, T, _ = x.shape
        return _ffn_block(x.reshape(B * T, D), norm_ffn[i], ffn_w_up, ffn_w_down, i).reshape(B, T, D)

    xp, wkv_p, shift_p = _rwkv_block(x_prompt, jnp.zeros((Bp, D), F32),
                                     jnp.zeros((Bp, H_A, HEAD_A, HEAD_A), F32), norm_mix[0], rwkv_params)
    xs, wkv_s, shift_s = _rwkv_block(x_sample, state_shift, state_wkv, norm_mix[0], rwkv_params)
    xp, xs = ffn(xp, 0), ffn(xs, 0)
    xp, (kv1_p, kv2_p, kv3_p) = _attn_block(xp, norm_mix[1], attn_w_qkv, attn_w_o, slopes)
    xs, (kv1_s, kv2_s, kv3_s) = _attn_block(xs, norm_mix[1], attn_w_qkv, attn_w_o, slopes,
                                            (cache_kv_g1, cache_kv_g2, cache_kv_g3))
    xp, xs = ffn(xp, 1), ffn(xs, 1)
    y_prompt = _rmsnorm_call(xp.reshape(Bp * Tp, D), norm_final).reshape(Bp, Tp, D)
    y_sample = _rmsnorm_call(xs.reshape(Bs * Ts, D), norm_final).reshape(Bs, Ts, D)
    return (y_prompt, y_sample, wkv_p, shift_p, kv1_p, kv2_p, kv3_p,
            wkv_s, shift_s, kv1_s, kv2_s, kv3_s)
```

```python
import functools

import jax
import jax.numpy as jnp
from jax import lax
from jax.experimental import pallas as pl
from jax.experimental.pallas import tpu as pltpu

F32 = jnp.float32
BF16 = jnp.bfloat16

D_MODEL = 2048
HEAD_A = 64
H_A = D_MODEL // HEAD_A
GN_EPS = 64e-5
GROUPS = ((128, 1), (512, 4), (2048, 16))
N_GROUPS = len(GROUPS)
H_B = 16
HD_B = 64
ATT_STEPS = 128
RMS_EPS = 1e-6

LANES = 128
VMEM_LIMIT = 56 * 1024 * 1024


def _bdot(a, b):
    return jnp.dot(a.astype(BF16), b.astype(BF16), preferred_element_type=F32)


def _bdot_nt(a, b):
    return lax.dot_general(a.astype(BF16), b.astype(BF16), (((1,), (1,)), ((), ())),
                           preferred_element_type=F32)


def _bdot_tn(a, b):
    return lax.dot_general(a.astype(BF16), b.astype(BF16), (((0,), (0,)), ((), ())),
                           preferred_element_type=F32)


def _exact_dot(m, x):
    hi = x.astype(BF16)
    r1 = x - hi.astype(F32)
    mid = r1.astype(BF16)
    lo = (r1 - mid.astype(F32)).astype(BF16)
    n = x.shape[1]
    parts = jnp.dot(m.astype(BF16), jnp.concatenate([hi, mid, lo], axis=1), preferred_element_type=F32)
    return (parts[:, :n] + parts[:, n:2 * n]) + parts[:, 2 * n:]


def _lockstep(chains):
    results = [None] * len(chains)
    live = list(range(len(chains)))
    while live:
        for idx in list(live):
            try:
                next(chains[idx])
            except StopIteration as done:
                results[idx] = done.value
                live.remove(idx)
    return results


def _head_sum(x, lane_lo):
    s0 = jnp.sum(jnp.where(lane_lo, x, 0.0), axis=-1, keepdims=True)
    s1 = jnp.sum(jnp.where(lane_lo, 0.0, x), axis=-1, keepdims=True)
    return jnp.where(lane_lo, s0, s1)


WKV_CHUNK = HEAD_A
(_MK_EYE, _MK_LVL0, _MK_BD, _MK_NMASK, _MK_CAUSAL, _MK_TRI, _MK_LEVELS) = range(7)
_N_LEVELS = 5
_N_MASKS = _MK_LEVELS + _N_LEVELS


def _wkv_fill_masks(mk_ref):
    L = WKV_CHUNK
    r2 = lax.broadcasted_iota(jnp.int32, (LANES, LANES), 0)
    c2 = lax.broadcasted_iota(jnp.int32, (LANES, LANES), 1)
    f = lambda m: m.astype(F32)
    same_block = (r2 < L) == (c2 < L)
    strict = (c2 % L) < (r2 % L)
    incl = (c2 % L) <= (r2 % L)
    mk_ref[_MK_EYE] = f(r2 == c2)
    mk_ref[_MK_LVL0] = f((r2 // 2 == c2 // 2) & (c2 < r2))
    mk_ref[_MK_BD] = f(same_block)
    mk_ref[_MK_NMASK] = f(strict & same_block)
    mk_ref[_MK_CAUSAL] = f(strict | (incl & (r2 >= L)))
    mk_ref[_MK_TRI] = f(incl & (r2 < L) & (c2 < L))
    size = 2
    for lvl in range(_N_LEVELS):
        mk_ref[_MK_LEVELS + lvl] = f((r2 // (2 * size) == c2 // (2 * size))
                                     & ((r2 // size) % 2 == 1) & ((c2 // size) % 2 == 0) & same_block)
        size *= 2


def _wkv_pair_chunk(r, kraw, v, wl, a, g, k_k, k_a, r_k, lnx_w, lnx_b, s2, mk_ref):
    L = WKV_CHUNK
    lane_lo = lax.broadcasted_iota(jnp.int32, (L, LANES), 1) < HEAD_A
    lo2 = lax.broadcasted_iota(jnp.int32, (2 * L, LANES), 1) < HEAD_A

    logd = -jnp.exp(wl)
    kkraw = kraw * k_k
    kk = kkraw * lax.rsqrt(jnp.maximum(_head_sum(kkraw * kkraw, lane_lo), 1e-24))
    k2 = kraw * (1.0 + (a - 1.0) * k_a)

    cum = _exact_dot(mk_ref[_MK_TRI][:L, :L], logd)
    w_incl = jnp.exp(cum)
    w_excl = jnp.exp(cum - logd)
    w_inv = jnp.exp(-cum)
    w_last = w_incl[L - 1:L, :]

    at = -kk * w_excl
    bt = (kk * a) * w_inv
    kt = k2 * w_inv
    rt = r * w_incl

    ar = jnp.concatenate([at, rt], axis=0)
    bk = jnp.concatenate([bt, kt], axis=0)
    kb = jnp.concatenate([kt, bt], axis=0)
    mm = _bdot_nt(jnp.concatenate([jnp.where(lo2, ar, 0.0), jnp.where(lo2, 0.0, ar)], axis=0),
                  jnp.concatenate([bk, kb], axis=0))
    m0 = mm[:2 * L, :2 * L]
    m1 = mm[2 * L:, 2 * L:]
    aprp = _bdot_nt(ar, s2)
    yield

    a0 = m0[:L]
    a1 = m1[:L]
    n2 = jnp.concatenate([a0, a1], axis=0) * mk_ref[_MK_NMASK]
    causal = mk_ref[_MK_CAUSAL]
    strict, incl = causal[:L], causal[L:]
    v_lo = jnp.where(lane_lo, v, 0.0)
    v_hi = jnp.where(lane_lo, 0.0, v)
    ak_cat = jnp.where(lane_lo, a1, a0) * strict

    t2 = mk_ref[_MK_EYE] + n2 * mk_ref[_MK_LVL0]
    for lvl in range(_N_LEVELS):
        tc = _bdot(t2, n2 * mk_ref[_MK_LEVELS + lvl])
        yield
        t2 = t2 + _bdot(tc, t2)
        yield
    tcat = t2[:L] + t2[L:]
    rhs = aprp[:L] + _bdot(ak_cat, jnp.concatenate([v_hi, v_lo], axis=0))
    yield

    u = _bdot(tcat, jnp.concatenate([jnp.where(lane_lo, rhs, 0.0), jnp.where(lane_lo, 0.0, rhs)], axis=0))
    yield
    u_lo = jnp.where(lane_lo, u, 0.0)
    u_hi = jnp.where(lane_lo, 0.0, u)

    y = aprp[L:] + _bdot(jnp.concatenate([m0[L:] * incl, m1[L:] * incl], axis=1),
                         jnp.concatenate([u_lo, v_lo, v_hi, u_hi], axis=0))
    delta = _bdot_tn(jnp.concatenate([u, v], axis=0), bk)
    yield
    s_new = (s2 + delta * mk_ref[_MK_BD]) * w_last

    mean = _head_sum(y, lane_lo) * (1.0 / HEAD_A)
    yc = y - mean
    var = _head_sum(yc * yc, lane_lo) * (1.0 / HEAD_A)
    yn = yc * lax.rsqrt(var + GN_EPS) * lnx_w + lnx_b
    bonus = _head_sum(r * k2 * r_k, lane_lo)
    return (yn + bonus * v) * g, s_new


def _wkv_kernel(r_ref, k_ref, v_ref, wl_ref, a_ref, g_ref, kk_ref, ka_ref, rk_ref, lw_ref, lb_ref,
                s0_ref, y_ref, sout_ref, s_scr, mk_scr):
    b, c = pl.program_id(0), pl.program_id(1)

    @pl.when((b == 0) & (c == 0))
    def _():
        _wkv_fill_masks(mk_scr)

    @pl.when(c == 0)
    def _():
        s_scr[...] = s0_ref[0]

    n_pairs = s_scr.shape[0]
    lanes = [slice(p * LANES, (p + 1) * LANES) for p in range(n_pairs)]
    chains = [_wkv_pair_chunk(
        r_ref[0, :, sl], k_ref[0, :, sl], v_ref[0, :, sl], wl_ref[0, :, sl], a_ref[0, :, sl],
        g_ref[0, :, sl], kk_ref[:, sl], ka_ref[:, sl], rk_ref[:, sl], lw_ref[:, sl], lb_ref[:, sl],
        s_scr[p], mk_scr) for p, sl in enumerate(lanes)]
    for (y, s_new), p, sl in zip(_lockstep(chains), range(n_pairs), lanes):
        y_ref[0, :, sl] = y.astype(y_ref.dtype)
        s_scr[p] = s_new
        sout_ref[0, p] = s_new


def _wkv_call(r, k, v, wlog, a, g, k_k, k_a, r_k, lnx_w, lnx_b, s2_0):
    B, T, D = r.shape
    L, P = WKV_CHUNK, D // LANES
    assert T % L == 0 and 2 * L == LANES and D % LANES == 0
    seq = pl.BlockSpec((1, L, D), lambda b, c: (b, c, 0))
    par = pl.BlockSpec((1, D), lambda b, c: (0, 0))
    st = pl.BlockSpec((1, P, LANES, LANES), lambda b, c: (b, 0, 0, 0))
    row = lambda x: x.reshape(1, D)
    return pl.pallas_call(
        _wkv_kernel,
        grid=(B, T // L),
        in_specs=[seq] * 6 + [par] * 5 + [st],
        out_specs=[seq, st],
        out_shape=[jax.ShapeDtypeStruct((B, T, D), BF16), jax.ShapeDtypeStruct((B, P, LANES, LANES), F32)],
        scratch_shapes=[pltpu.VMEM((P, LANES, LANES), F32), pltpu.VMEM((_N_MASKS, LANES, LANES), F32)],
        compiler_params=pltpu.CompilerParams(dimension_semantics=("arbitrary", "arbitrary")),
        name="wkv7_chunk_scan",
    )(r, k, v, wlog, a, g, row(k_k), row(k_a), row(r_k), row(lnx_w), row(lnx_b), s2_0)


def _rmsnorm(x, g):
    return x * lax.rsqrt(jnp.mean(x * x, axis=-1, keepdims=True) + RMS_EPS) * g


def _rmsnorm_kernel(x_ref, g_ref, o_ref):
    o_ref[...] = _rmsnorm(x_ref[...], g_ref[...]).astype(o_ref.dtype)


def _rmsnorm_call(x, g, out_dtype=F32):
    M, D = x.shape
    bm = min(M, 512)
    return pl.pallas_call(
        _rmsnorm_kernel,
        grid=(M // bm,),
        in_specs=[pl.BlockSpec((bm, D), lambda m: (m, 0)), pl.BlockSpec((1, D), lambda m: (0, 0))],
        out_specs=pl.BlockSpec((bm, D), lambda m: (m, 0)),
        out_shape=jax.ShapeDtypeStruct((M, D), out_dtype),
        compiler_params=pltpu.CompilerParams(dimension_semantics=("parallel",)),
        name="rmsnorm",
    )(x, g.reshape(1, D))


MM_ROW_CHUNK = 256


def _mm_kernel(x_ref, w_ref, *refs, n_e, post, nk):
    e_refs = refs[:n_e]
    o_ref = refs[n_e]
    bm = o_ref.shape[0]
    acc_ref = refs[-1] if nk > 1 else None
    k = pl.program_id(2)
    if nk > 1:
        @pl.when(k == 0)
        def _():
            acc_ref[...] = jnp.zeros_like(acc_ref)

    w = w_ref[...].astype(BF16)
    chunk = min(bm, MM_ROW_CHUNK)
    for r0 in range(0, bm, chunk):
        rows = slice(r0, r0 + chunk)
        tile = lambda ref: ref[rows, :] if ref.shape[0] == bm else ref[...]
        part = jnp.dot(x_ref[rows, :].astype(BF16), w, preferred_element_type=F32)
        if nk == 1:
            o_ref[rows, :] = post(part, *[tile(e) for e in e_refs]).astype(o_ref.dtype)
        else:
            acc_ref[rows, :] += part

    if nk > 1:
        @pl.when(k == nk - 1)
        def _():
            o_ref[...] = post(acc_ref[...], *[e[...] for e in e_refs]).astype(o_ref.dtype)


def _mm_norm_kernel(x_ref, g_ref, w_ref, o_ref, x_scr, *, post):
    @pl.when(pl.program_id(1) == 0)
    def _():
        x_scr[...] = _rmsnorm(x_ref[...], g_ref[...]).astype(BF16)

    bm = o_ref.shape[0]
    w = w_ref[...].astype(BF16)
    chunk = min(bm, MM_ROW_CHUNK)
    for r0 in range(0, bm, chunk):
        rows = slice(r0, r0 + chunk)
        part = jnp.dot(x_scr[rows, :], w, preferred_element_type=F32)
        o_ref[rows, :] = post(part).astype(o_ref.dtype)


def _identity(x):
    return x


def _add_residual(acc, res):
    return res + acc


def _weight_spec(block, index, layer):
    if layer is None:
        return pl.BlockSpec(block, index)
    return pl.BlockSpec((None,) + block, lambda *ids: (layer,) + index(*ids))


def _matmul_norm_call(x, g, w, *, post=_identity, out_dtype=F32, bm=2048, bn=512, layer=None, name="matmul"):
    K, N = w.shape[-2:]
    M = x.shape[0]
    bm = min(bm, M)
    bn = min(bn, N)
    assert M % bm == 0 and N % bn == 0
    return pl.pallas_call(
        functools.partial(_mm_norm_kernel, post=post),
        grid=(M // bm, N // bn),
        in_specs=[pl.BlockSpec((bm, K), lambda m, n: (m, 0), pipeline_mode=pl.Buffered(1)),
                  pl.BlockSpec((1, K), lambda m, n: (0, 0)),
                  _weight_spec((K, bn), lambda m, n: (0, n), layer)],
        out_specs=pl.BlockSpec((bm, bn), lambda m, n: (m, n)),
        out_shape=jax.ShapeDtypeStruct((M, N), out_dtype),
        scratch_shapes=[pltpu.VMEM((bm, K), BF16)],
        compiler_params=pltpu.CompilerParams(
            dimension_semantics=("parallel", "arbitrary"), vmem_limit_bytes=VMEM_LIMIT),
        name=name,
    )(x, g.reshape(1, K), w)


def _matmul_call(x, w, extras=(), *, post=_identity, out_dtype=F32,
                 bm=1024, bn=1024, bk=None, layer=None, name="matmul"):
    K, N = w.shape[-2:]
    M = x.shape[0]
    bm = min(bm, M)
    bn = min(bn, N)
    bk = K if bk is None else min(bk, K)
    assert M % bm == 0 and N % bn == 0 and K % bk == 0
    nk = K // bk
    e_specs = [pl.BlockSpec((bm, bn), lambda n, m, k: (m, n)) if e.shape[0] == M
               else pl.BlockSpec((1, bn), lambda n, m, k: (0, n)) for e in extras]
    return pl.pallas_call(
        functools.partial(_mm_kernel, n_e=len(extras), post=post, nk=nk),
        grid=(N // bn, M // bm, nk),
        in_specs=[pl.BlockSpec((bm, bk), lambda n, m, k: (m, k)),
                  _weight_spec((bk, bn), lambda n, m, k: (k, n), layer)] + e_specs,
        out_specs=pl.BlockSpec((bm, bn), lambda n, m, k: (m, n)),
        out_shape=jax.ShapeDtypeStruct((M, N), out_dtype),
        scratch_shapes=[pltpu.VMEM((bm, bn), F32)] if nk > 1 else [],
        compiler_params=pltpu.CompilerParams(
            dimension_semantics=("parallel", "parallel", "arbitrary"),
            vmem_limit_bytes=VMEM_LIMIT),
        name=name,
    )(x, w, *extras)


ATT_BLOCK = ATT_STEPS * max(d for _, d in GROUPS)
ATT_LANES = 4


def _attn_prompt_kernel(slopes_ref, *refs):
    ng = N_GROUPS
    q_refs, kc_refs, vc_refs = refs[0:ng], refs[ng:2 * ng], refs[2 * ng:3 * ng]
    kp_refs, vp_refs = refs[3 * ng:4 * ng], refs[4 * ng:5 * ng]
    o_ref = refs[5 * ng]
    m_scr, l_scr, acc_scr = refs[5 * ng + 1:]
    hp = pl.program_id(1)
    j = pl.program_id(2)
    S = ATT_STEPS
    lane_lo = lax.broadcasted_iota(jnp.int32, (S, LANES), 1) < HD_B
    iq = lax.broadcasted_iota(jnp.int32, (S, 2 * S), 0)
    jk = lax.broadcasted_iota(jnp.int32, (S, 2 * S), 1)
    steps = iq + S - jk
    mask = jnp.where((steps >= 0) & (steps <= S), 0.0, -jnp.inf)
    scale = HD_B ** -0.5
    no_prev = jnp.where(j > 0, 0.0, -jnp.inf)

    order = sorted(range(ng), key=lambda g: -GROUPS[g][1])
    for g in order:
        dil = GROUPS[g][1]
        biases = [mask - slopes_ref[g * H_B + 2 * hp + h] * (steps * dil).astype(F32) for h in range(2)]
        q_ref, kc_ref, vc_ref, kp_ref, vp_ref = q_refs[g], kc_refs[g], vc_refs[g], kp_refs[g], vp_refs[g]

        def attend(cur_start, prev_ref, prev_start, prev_bias, dil=dil, g=g, biases=biases, q_ref=q_ref,
                   kc_ref=kc_ref, vc_ref=vc_ref):
            cur = pl.ds(cur_start, S, stride=dil) if dil > 1 else pl.ds(cur_start, S)
            prv = pl.ds(prev_start, S, stride=dil) if dil > 1 else pl.ds(prev_start, S)
            q, kc, vc = q_ref[0, cur, :], kc_ref[0, cur, :], vc_ref[0, cur, :]
            kp, vp = prev_ref[0][0, prv, :], prev_ref[1][0, prv, :]
            qs = [jnp.where(lane_lo, q, 0.0), jnp.where(lane_lo, 0.0, q)]
            scores = [(_bdot_nt(qh, kp), _bdot_nt(qh, kc)) for qh in qs]
            yield
            ms, ls, probs = [], [], []
            for h in range(2):
                sp = scores[h][0] * scale + biases[h][:, :S]
                if prev_bias is not None:
                    sp = sp + prev_bias
                sc = scores[h][1] * scale + biases[h][:, S:]
                m = jnp.max(jnp.maximum(sp, sc), axis=-1, keepdims=True)
                pp = jnp.exp(sp - m)
                pc = jnp.exp(sc - m)
                ls.append(jnp.sum(pp + pc, axis=-1, keepdims=True))
                probs.append((pp, pc))
                ms.append(m)
            accs = [_bdot(pp, vp) + _bdot(pc, vc) for pp, pc in probs]
            yield
            m = jnp.where(lane_lo, ms[0], ms[1])
            l = jnp.where(lane_lo, ls[0], ls[1])
            acc = jnp.where(lane_lo, accs[0], accs[1])
            if g != order[0]:
                m_old = m_scr[cur, :]
                m_new = jnp.maximum(m_old, m)
                w_old, w_new = jnp.exp(m_old - m_new), jnp.exp(m - m_new)
                l = l_scr[cur, :] * w_old + l * w_new
                acc = acc_scr[cur, :] * w_old + acc * w_new
                m = m_new
            m_scr[cur, :] = m
            l_scr[cur, :] = l
            acc_scr[cur, :] = acc

        sd = S * dil
        n_sub = ATT_BLOCK // sd
        run = lambda items, attend=attend: _lockstep([attend(*it) for it in items])
        first = lambda c, kp_ref=kp_ref, vp_ref=vp_ref: (c, (kp_ref, vp_ref), c, no_prev)
        later = lambda c, i, sd=sd, kc_ref=kc_ref, vc_ref=vc_ref: (
            i * sd + c, (kc_ref, vc_ref), (i - 1) * sd + c, None)

        def loop(lo, hi, fn):
            def body(idx, carry):
                fn(idx)
                return carry
            lax.fori_loop(lo, hi, body, 0)

        W = ATT_LANES if dil == 1 else min(ATT_LANES, dil)
        if dil == 1:
            hb = n_sub // W
            run([first(0)] + [later(0, k * hb) for k in range(1, W)])
            loop(1, hb, lambda i, hb=hb: run([later(0, i + k * hb) for k in range(W)]))
        else:
            cs = dil // W
            def streams(c, cs=cs, n_sub=n_sub):
                run([first(c + k * cs) for k in range(W)])
                if n_sub > 1:
                    loop(1, n_sub, lambda i: run([later(c + k * cs, i) for k in range(W)]))
            if cs == 1:
                streams(0)
            else:
                loop(0, cs, streams)

    o_ref[0] = (acc_scr[...] / l_scr[...]).astype(o_ref.dtype)


def _attn_prompt_call(qkv, slopes):
    B, T, W = qkv.shape
    hw = H_B * HD_B
    n_hp = H_B // 2
    TB = ATT_BLOCK
    assert T % TB == 0
    col = lambda g, which, hp: (g * 3 * hw + which * hw) // LANES + hp
    cur = lambda g, which: pl.BlockSpec((1, TB, LANES), lambda b, hp, j: (b, j, col(g, which, hp)))
    def prev(g, which):
        rows = ATT_STEPS * GROUPS[g][1]
        per = TB // rows
        return pl.BlockSpec((1, rows, LANES),
                            lambda b, hp, j: (b, jnp.maximum(j * per - 1, 0), col(g, which, hp)))
    gs = range(N_GROUPS)
    in_specs = ([pl.BlockSpec(memory_space=pltpu.SMEM)]
                + [cur(g, 0) for g in gs] + [cur(g, 1) for g in gs] + [cur(g, 2) for g in gs]
                + [prev(g, 1) for g in gs] + [prev(g, 2) for g in gs])
    return pl.pallas_call(
        _attn_prompt_kernel,
        grid=(B, n_hp, T // TB),
        in_specs=in_specs,
        out_specs=pl.BlockSpec((1, TB, LANES), lambda b, hp, j: (b, j, hp)),
        out_shape=jax.ShapeDtypeStruct((B, T, hw), BF16),
        scratch_shapes=[pltpu.VMEM((TB, LANES), F32)] * 3,
        compiler_params=pltpu.CompilerParams(
            dimension_semantics=("parallel", "parallel", "arbitrary"), vmem_limit_bytes=VMEM_LIMIT),
        name="attn_prompt",
    )(slopes, *([qkv] * (5 * N_GROUPS)))


def _kv_window_kernel(x_ref, o_ref):
    for hp in range(H_B // 2):
        sl = slice(hp * LANES, (hp + 1) * LANES)
        o_ref[0, 0, sl, :] = x_ref[0, :, sl].T


def _kv_window_call(qkv, g, window):
    B, T, _ = qkv.shape
    hw = H_B * HD_B
    wb = min(window, 512)
    first = (T - window) // wb
    out = pl.pallas_call(
        _kv_window_kernel,
        grid=(B, 2, window // wb),
        in_specs=[pl.BlockSpec((1, wb, hw), lambda b, kv, i: (b, first + i, 3 * g + 1 + kv))],
        out_specs=pl.BlockSpec((1, 1, hw, wb), lambda b, kv, i: (b, kv, 0, i)),
        out_shape=jax.ShapeDtypeStruct((B, 2, hw, window), F32),
        compiler_params=pltpu.CompilerParams(dimension_semantics=("parallel", "parallel", "parallel")),
        name=f"kv_window_g{g}",
    )(qkv)
    return out.reshape(B, 2, H_B, HD_B, window).transpose(0, 4, 1, 2, 3)


def _roll_append(old, new, o_ref, kv):
    R, W = old.shape
    S = new.shape[0]
    rolled = pltpu.roll(old, W - S, 1)
    new_t = jnp.concatenate([jnp.zeros((LANES - S, R), F32), new], axis=0).T
    if W > LANES:
        o_ref[0, kv, :, :W - LANES] = rolled[:, :W - LANES]
    lane = lax.broadcasted_iota(jnp.int32, (R, LANES), 1)
    o_ref[0, kv, :, W - LANES:] = jnp.where(lane >= LANES - S, new_t, rolled[:, W - LANES:])


def _attn_decode_kernel(slopes_ref, *refs):
    ng = N_GROUPS
    q_refs, k_refs, v_refs = refs[0:ng], refs[ng:2 * ng], refs[2 * ng:3 * ng]
    kt_refs, vt_refs = refs[3 * ng:4 * ng], refs[4 * ng:5 * ng]
    o_ref = refs[5 * ng]
    roll_refs = refs[5 * ng + 1:6 * ng + 1]
    hp = pl.program_id(1)
    S = q_refs[0].shape[1]
    lane_lo = lax.broadcasted_iota(jnp.int32, (S, LANES), 1) < HD_B
    scale = HD_B ** -0.5

    for kv, (new_refs, old_refs) in enumerate(((k_refs, kt_refs), (v_refs, vt_refs))):
        for g in range(ng):
            _roll_append(old_refs[g][0, 0], new_refs[g][0], roll_refs[g], kv)

    def chain(g, h):
        window, dil = GROUPS[g]
        slope = slopes_ref[g * H_B + 2 * hp + h]
        q = q_refs[g][0]
        qh = jnp.where(lane_lo, q, 0.0) if h == 0 else jnp.where(lane_lo, 0.0, q)
        s_old = _bdot(qh, kt_refs[g][0, 0])
        s_new = _bdot_nt(qh, k_refs[g][0])
        yield
        j_o = lax.broadcasted_iota(jnp.int32, (S, window), 0)
        w_o = lax.broadcasted_iota(jnp.int32, (S, window), 1)
        dist_o = window + j_o - w_o
        ok_o = ((dist_o & (dil - 1)) == 0) & (dist_o <= window)
        j_n = lax.broadcasted_iota(jnp.int32, (S, S), 0)
        dist_n = j_n - lax.broadcasted_iota(jnp.int32, (S, S), 1)
        ok_n = ((dist_n & (dil - 1)) == 0) & (dist_n >= 0)
        s_old = jnp.where(ok_o, s_old * scale - slope * dist_o.astype(F32), -jnp.inf)
        s_new = jnp.where(ok_n, s_new * scale - slope * dist_n.astype(F32), -jnp.inf)
        m = jnp.maximum(jnp.max(s_old, axis=-1, keepdims=True), jnp.max(s_new, axis=-1, keepdims=True))
        p_old = jnp.exp(s_old - m)
        p_new = jnp.exp(s_new - m)
        l = jnp.sum(p_old, axis=-1, keepdims=True) + jnp.sum(p_new, axis=-1, keepdims=True)
        acc = _bdot_nt(p_old, vt_refs[g][0, 0]) + _bdot(p_new, v_refs[g][0])
        yield
        return m, l, acc

    res = _lockstep([chain(g, h) for g in range(ng) for h in range(2)])
    m = l = acc = None
    for g in range(ng):
        (m0, l0, a0), (m1, l1, a1) = res[2 * g], res[2 * g + 1]
        mg = jnp.where(lane_lo, m0, m1)
        lg = jnp.where(lane_lo, l0, l1)
        ag = jnp.where(lane_lo, a0, a1)
        if g == 0:
            m, l, acc = mg, lg, ag
        else:
            m_new = jnp.maximum(m, mg)
            w_old, w_new = jnp.exp(m - m_new), jnp.exp(mg - m_new)
            l = l * w_old + lg * w_new
            acc = acc * w_old + ag * w_new
            m = m_new
    o_ref[0] = (acc / l).astype(o_ref.dtype)


def _attn_decode_call(qkv, caches, slopes):
    B, S, _ = qkv.shape
    hw = H_B * HD_B
    n_hp = H_B // 2
    assert all(w >= S and d & (d - 1) == 0 for w, d in GROUPS)
    cache_t = [c.transpose(0, 2, 3, 4, 1).reshape(B, 2, hw, c.shape[1]) for c in caches]
    col = lambda g, which, hp: (g * 3 * hw + which * hw) // LANES + hp
    new = lambda g, which: pl.BlockSpec((1, S, LANES), lambda b, hp: (b, 0, col(g, which, hp)))
    old = lambda g, kv: pl.BlockSpec((1, 1, LANES, GROUPS[g][0]), lambda b, hp: (b, kv, hp, 0))
    gs = range(N_GROUPS)
    in_specs = ([pl.BlockSpec(memory_space=pltpu.SMEM)]
                + [new(g, 0) for g in gs] + [new(g, 1) for g in gs] + [new(g, 2) for g in gs]
                + [old(g, 0) for g in gs] + [old(g, 1) for g in gs])
    outs = pl.pallas_call(
        _attn_decode_kernel,
        grid=(B, n_hp),
        in_specs=in_specs,
        out_specs=([pl.BlockSpec((1, S, LANES), lambda b, hp: (b, 0, hp))]
                   + [pl.BlockSpec((1, 2, LANES, w), lambda b, hp: (b, 0, hp, 0)) for w, _ in GROUPS]),
        out_shape=([jax.ShapeDtypeStruct((B, S, hw), BF16)]
                   + [jax.ShapeDtypeStruct((B, 2, hw, w), F32) for w, _ in GROUPS]),
        compiler_params=pltpu.CompilerParams(dimension_semantics=("parallel", "parallel")),
        name="attn_decode",
    )(slopes, *([qkv] * (3 * N_GROUPS)), *cache_t, *cache_t)
    rolled = [r.reshape(B, 2, H_B, HD_B, r.shape[-1]).transpose(0, 4, 1, 2, 3) for r in outs[1:]]
    return outs[0], rolled


def _alibi_slopes():
    n = N_GROUPS * H_B
    return 2.0 ** (-8.0 * jnp.arange(1, n + 1, dtype=F32) / n)


def _pair_states(s):
    B, H = s.shape[:2]
    s = s.reshape(B, H // 2, 2, HEAD_A, HEAD_A)
    z = jnp.zeros_like(s[:, :, 0])
    top = jnp.concatenate([s[:, :, 0], z], axis=-1)
    bot = jnp.concatenate([z, s[:, :, 1]], axis=-1)
    return jnp.concatenate([top, bot], axis=-2)


def _unpair_states(s2):
    B, P = s2.shape[:2]
    return jnp.stack([s2[:, :, :HEAD_A, :HEAD_A], s2[:, :, HEAD_A:, HEAD_A:]],
                     axis=2).reshape(B, 2 * P, HEAD_A, HEAD_A)


def _softplus(x):
    return jnp.maximum(x, 0.0) + jnp.log(1.0 + jnp.exp(-jnp.abs(x)))


def _sigmoid(x):
    return 0.5 * (1.0 + jnp.tanh(0.5 * x))


def _norm_mix_kernel(x_ref, g_ref, mu_ref, shift_ref, w0_ref, w1_ref, w2_ref, a0_ref, a1_ref, a2_ref,
                     g1_ref, g2_ref, xr_ref, xk_ref, xv_ref, wlog_ref, a_ref, gate_ref, hlast_ref, carry_scr,
                     *, tiles_per_seq):
    @pl.when(pl.program_id(0) == 0)
    def _():
        carry_scr[...] = jnp.zeros_like(carry_scr)

    x = x_ref[0]
    h = x * lax.rsqrt(jnp.mean(x * x, axis=-1, keepdims=True) + RMS_EPS) * g_ref[...]
    rows = h.shape[0]
    first_tile = pl.program_id(0) % tiles_per_seq == 0
    row0 = jnp.where(first_tile, shift_ref[0], carry_scr[...])
    prev = pltpu.roll(h, 1, 0) if rows > 1 else h
    prev = jnp.where(lax.broadcasted_iota(jnp.int32, h.shape, 0) == 0, row0, prev)
    carry_scr[...] = h[rows - 1:rows, :]
    hlast_ref[0] = h[rows - 1:rows, :]
    diff = prev - h
    mix = lambda i: h + diff * mu_ref[i:i + 1, :]
    xr_ref[0] = mix(0).astype(BF16)
    xk_ref[0] = mix(2).astype(BF16)
    xv_ref[0] = mix(3).astype(BF16)
    w_pre = w0_ref[...] + _bdot(jnp.tanh(_bdot(mix(1), w1_ref[...])), w2_ref[...])
    wlog_ref[0] = -_softplus(-w_pre) - 0.5
    a_ref[0] = _sigmoid(a0_ref[...] + _bdot(_bdot(mix(4), a1_ref[...]), a2_ref[...]))
    gate_ref[0] = _bdot(_sigmoid(_bdot(mix(5), g1_ref[...])), g2_ref[...])


def _norm_mix_call(x, g, mu, shift0, w0, w1, w2, a0, a1, a2, g1, g2, bt=256):
    B, T, D = x.shape
    bt = min(bt, T)
    assert T % bt == 0
    tiles = T // bt
    seq = pl.BlockSpec((1, bt, D), lambda m: (m // tiles, m % tiles, 0))
    per_seq = pl.BlockSpec((1, 1, D), lambda m: (m // tiles, 0, 0))
    whole = lambda z: pl.BlockSpec(z.shape, lambda m: (0,) * z.ndim)
    row = lambda z: z.reshape(1, D)
    consts = [row(g), mu]
    bf = lambda z: z.astype(BF16)
    lora = [row(w0), bf(w1), bf(w2), row(a0), bf(a1), bf(a2), bf(g1), bf(g2)]
    outs = pl.pallas_call(
        functools.partial(_norm_mix_kernel, tiles_per_seq=tiles),
        grid=(B * tiles,),
        in_specs=[seq] + [whole(z) for z in consts] + [per_seq] + [whole(z) for z in lora],
        out_specs=[seq] * 6 + [per_seq],
        out_shape=([jax.ShapeDtypeStruct((B, T, D), BF16)] * 3 + [jax.ShapeDtypeStruct((B, T, D), F32)] * 3
                   + [jax.ShapeDtypeStruct((B, 1, D), F32)]),
        scratch_shapes=[pltpu.VMEM((1, D), F32)],
        compiler_params=pltpu.CompilerParams(dimension_semantics=("arbitrary",),
                                             vmem_limit_bytes=VMEM_LIMIT),
        name="rwkv_norm_mix",
    )(x, *consts, shift0.reshape(B, 1, D), *lora)
    return [o.reshape(B * T, D) for o in outs[:6]], outs[6].reshape(B, D)


def _rwkv_block(x, shift0, wkv0, norm_g, p):
    (mu, w0, w1, w2, a0, a1, a2, g1, g2, k_k, k_a, r_k, lnx_w, lnx_b, w_r, w_k, w_v, w_o) = p
    B, T, D = x.shape
    M = B * T
    x2 = x.reshape(M, D)
    (xr, xk, xv, wlog, a, g), h_last = _norm_mix_call(x, norm_g, mu, shift0, w0, w1, w2, a0, a1, a2, g1, g2)
    r = _matmul_call(xr, w_r, name="rwkv_r")
    k = _matmul_call(xk, w_k, name="rwkv_k")
    v = _matmul_call(xv, w_v, name="rwkv_v")

    L = WKV_CHUNK
    Tp = -(-T // L) * L
    def seq(z, fill=0.0):
        z = z.reshape(B, T, D)
        if Tp != T:
            z = jnp.pad(z, ((0, 0), (0, Tp - T), (0, 0)), constant_values=fill)
        return z
    y, s2 = _wkv_call(seq(r), seq(k), seq(v), seq(wlog, -jnp.inf), seq(a), seq(g),
                      k_k, k_a, r_k.reshape(D), lnx_w, lnx_b, _pair_states(wkv0))
    y2 = y[:, :T].reshape(M, D)
    out = _matmul_call(y2, w_o, [x2], post=_add_residual, name="rwkv_o")
    return out.reshape(B, T, D), _unpair_states(s2), h_last


def _ffn_block(x2, norm_g, w_up, w_down, layer):
    mid = _matmul_norm_call(x2, norm_g, w_up, post=lambda acc: jnp.square(jnp.maximum(acc, 0.0)),
                            out_dtype=BF16, layer=layer, name="ffn_up")
    return _matmul_call(mid, w_down, [x2], post=_add_residual, bk=2048, layer=layer, name="ffn_down")


def _attn_block(x, norm_g, w_qkv, w_o, slopes, caches=None):
    B, T, D = x.shape
    M = B * T
    x2 = x.reshape(M, D)
    qkv = _matmul_norm_call(x2, norm_g, w_qkv, name="attn_qkv").reshape(B, T, -1)
    hw = H_B * HD_B
    if caches is None:
        merged = _attn_prompt_call(qkv, slopes)
        bufs = [_kv_window_call(qkv, g, min(w, T)) for g, (w, _) in enumerate(GROUPS)]
    else:
        merged, bufs = _attn_decode_call(qkv, caches, slopes)
    out = _matmul_call(merged.reshape(M, hw), w_o, [x2], post=_add_residual, name="attn_o")
    return out.reshape(B, T, D), bufs


def kernel(x_prompt, x_sample, state_wkv, state_shift, cache_kv_g1, cache_kv_g2, cache_kv_g3,
           norm_mix, norm_ffn, norm_final,
           rwkv_mu, rwkv_w0, rwkv_w1, rwkv_w2, rwkv_a0, rwkv_a1, rwkv_a2, rwkv_g1, rwkv_g2,
           rwkv_k_k, rwkv_k_a, rwkv_r_k, rwkv_lnx_w, rwkv_lnx_b, rwkv_w_r, rwkv_w_k, rwkv_w_v, rwkv_w_o,
           attn_w_qkv, attn_w_o, ffn_w_up, ffn_w_down):
    rwkv_params = (rwkv_mu, rwkv_w0, rwkv_w1, rwkv_w2, rwkv_a0, rwkv_a1, rwkv_a2, rwkv_g1, rwkv_g2,
                   rwkv_k_k, rwkv_k_a, rwkv_r_k, rwkv_lnx_w, rwkv_lnx_b,
                   rwkv_w_r, rwkv_w_k, rwkv_w_v, rwkv_w_o)
    slopes = _alibi_slopes()
    Bp, Tp, D = x_prompt.shape
    Bs, Ts, _ = x_sample.shape

    def ffn(x, i):
        B, T, _ = x.shape
        return _ffn_block(x.reshape(B * T, D), norm_ffn[i], ffn_w_up, ffn_w_down, i).reshape(B, T, D)

    xp, wkv_p, shift_p = _rwkv_block(x_prompt, jnp.zeros((Bp, D), F32),
                                     jnp.zeros((Bp, H_A, HEAD_A, HEAD_A), F32), norm_mix[0], rwkv_params)
    xs, wkv_s, shift_s = _rwkv_block(x_sample, state_shift, state_wkv, norm_mix[0], rwkv_params)
    xp, xs = ffn(xp, 0), ffn(xs, 0)
    xp, (kv1_p, kv2_p, kv3_p) = _attn_block(xp, norm_mix[1], attn_w_qkv, attn_w_o, slopes)
    xs, (kv1_s, kv2_s, kv3_s) = _attn_block(xs, norm_mix[1], attn_w_qkv, attn_w_o, slopes,
                                            (cache_kv_g1, cache_kv_g2, cache_kv_g3))
    xp, xs = ffn(xp, 1), ffn(xs, 1)
    y_prompt = _rmsnorm_call(xp.reshape(Bp * Tp, D), norm_final).reshape(Bp, Tp, D)
    y_sample = _rmsnorm_call(xs.reshape(Bs * Ts, D), norm_final).reshape(Bs, Ts, D)
    return (y_prompt, y_sample, wkv_p, shift_p, kv1_p, kv2_p, kv3_p,
            wkv_s, shift_s, kv1_s, kv2_s, kv3_s)
```

```python
import functools

import jax
import jax.numpy as jnp
from jax import lax
from jax.experimental import pallas as pl
from jax.experimental.pallas import tpu as pltpu

F32 = jnp.float32
BF16 = jnp.bfloat16

D_MODEL = 2048
HEAD_A = 64
H_A = D_MODEL // HEAD_A
GN_EPS = 64e-5
GROUPS = ((128, 1), (512, 4), (2048, 16))
N_GROUPS = len(GROUPS)
H_B = 16
HD_B = 64
ATT_STEPS = 128
RMS_EPS = 1e-6

LANES = 128
VMEM_LIMIT = 56 * 1024 * 1024


def _bdot(a, b):
    return jnp.dot(a.astype(BF16), b.astype(BF16), preferred_element_type=F32)


def _bdot_nt(a, b):
    return lax.dot_general(a.astype(BF16), b.astype(BF16), (((1,), (1,)), ((), ())),
                           preferred_element_type=F32)


def _bdot_tn(a, b):
    return lax.dot_general(a.astype(BF16), b.astype(BF16), (((0,), (0,)), ((), ())),
                           preferred_element_type=F32)


def _exact_dot(m, x):
    hi = x.astype(BF16)
    r1 = x - hi.astype(F32)
    mid = r1.astype(BF16)
    lo = (r1 - mid.astype(F32)).astype(BF16)
    n = x.shape[1]
    parts = jnp.dot(m.astype(BF16), jnp.concatenate([hi, mid, lo], axis=1), preferred_element_type=F32)
    return (parts[:, :n] + parts[:, n:2 * n]) + parts[:, 2 * n:]


def _lockstep(chains):
    results = [None] * len(chains)
    live = list(range(len(chains)))
    while live:
        for idx in list(live):
            try:
                next(chains[idx])
            except StopIteration as done:
                results[idx] = done.value
                live.remove(idx)
    return results


def _head_sum(x, lane_lo):
    s0 = jnp.sum(jnp.where(lane_lo, x, 0.0), axis=-1, keepdims=True)
    s1 = jnp.sum(jnp.where(lane_lo, 0.0, x), axis=-1, keepdims=True)
    return jnp.where(lane_lo, s0, s1)


WKV_CHUNK = HEAD_A
(_MK_EYE, _MK_LVL0, _MK_BD, _MK_NMASK, _MK_CAUSAL, _MK_TRI, _MK_LEVELS) = range(7)
_N_LEVELS = 5
_N_MASKS = _MK_LEVELS + _N_LEVELS


def _wkv_fill_masks(mk_ref):
    L = WKV_CHUNK
    r2 = lax.broadcasted_iota(jnp.int32, (LANES, LANES), 0)
    c2 = lax.broadcasted_iota(jnp.int32, (LANES, LANES), 1)
    f = lambda m: m.astype(F32)
    same_block = (r2 < L) == (c2 < L)
    strict = (c2 % L) < (r2 % L)
    incl = (c2 % L) <= (r2 % L)
    mk_ref[_MK_EYE] = f(r2 == c2)
    mk_ref[_MK_LVL0] = f((r2 // 2 == c2 // 2) & (c2 < r2))
    mk_ref[_MK_BD] = f(same_block)
    mk_ref[_MK_NMASK] = f(strict & same_block)
    mk_ref[_MK_CAUSAL] = f(strict | (incl & (r2 >= L)))
    mk_ref[_MK_TRI] = f(incl & (r2 < L) & (c2 < L))
    size = 2
    for lvl in range(_N_LEVELS):
        mk_ref[_MK_LEVELS + lvl] = f((r2 // (2 * size) == c2 // (2 * size))
                                     & ((r2 // size) % 2 == 1) & ((c2 // size) % 2 == 0) & same_block)
        size *= 2


def _wkv_pair_chunk(r, kraw, v, wl, a, g, k_k, k_a, r_k, lnx_w, lnx_b, s2, mk_ref):
    L = WKV_CHUNK
    lane_lo = lax.broadcasted_iota(jnp.int32, (L, LANES), 1) < HEAD_A
    lo2 = lax.broadcasted_iota(jnp.int32, (2 * L, LANES), 1) < HEAD_A

    logd = -jnp.exp(wl)
    kkraw = kraw * k_k
    kk = kkraw * lax.rsqrt(jnp.maximum(_head_sum(kkraw * kkraw, lane_lo), 1e-24))
    k2 = kraw * (1.0 + (a - 1.0) * k_a)

    cum = _exact_dot(mk_ref[_MK_TRI][:L, :L], logd)
    w_incl = jnp.exp(cum)
    w_excl = jnp.exp(cum - logd)
    w_inv = jnp.exp(-cum)
    w_last = w_incl[L - 1:L, :]

    at = -kk * w_excl
    bt = (kk * a) * w_inv
    kt = k2 * w_inv
    rt = r * w_incl

    ar = jnp.concatenate([at, rt], axis=0)
    bk = jnp.concatenate([bt, kt], axis=0)
    kb = jnp.concatenate([kt, bt], axis=0)
    mm = _bdot_nt(jnp.concatenate([jnp.where(lo2, ar, 0.0), jnp.where(lo2, 0.0, ar)], axis=0),
                  jnp.concatenate([bk, kb], axis=0))
    m0 = mm[:2 * L, :2 * L]
    m1 = mm[2 * L:, 2 * L:]
    aprp = _bdot_nt(ar, s2)
    yield

    a0 = m0[:L]
    a1 = m1[:L]
    n2 = jnp.concatenate([a0, a1], axis=0) * mk_ref[_MK_NMASK]
    causal = mk_ref[_MK_CAUSAL]
    strict, incl = causal[:L], causal[L:]
    v_lo = jnp.where(lane_lo, v, 0.0)
    v_hi = jnp.where(lane_lo, 0.0, v)
    ak_cat = jnp.where(lane_lo, a1, a0) * strict

    t2 = mk_ref[_MK_EYE] + n2 * mk_ref[_MK_LVL0]
    for lvl in range(_N_LEVELS):
        tc = _bdot(t2, n2 * mk_ref[_MK_LEVELS + lvl])
        yield
        t2 = t2 + _bdot(tc, t2)
        yield
    tcat = t2[:L] + t2[L:]
    rhs = aprp[:L] + _bdot(ak_cat, jnp.concatenate([v_hi, v_lo], axis=0))
    yield

    u = _bdot(tcat, jnp.concatenate([jnp.where(lane_lo, rhs, 0.0), jnp.where(lane_lo, 0.0, rhs)], axis=0))
    yield
    u_lo = jnp.where(lane_lo, u, 0.0)
    u_hi = jnp.where(lane_lo, 0.0, u)

    y = aprp[L:] + _bdot(jnp.concatenate([m0[L:] * incl, m1[L:] * incl], axis=1),
                         jnp.concatenate([u_lo, v_lo, v_hi, u_hi], axis=0))
    delta = _bdot_tn(jnp.concatenate([u, v], axis=0), bk)
    yield
    s_new = (s2 + delta * mk_ref[_MK_BD]) * w_last

    mean = _head_sum(y, lane_lo) * (1.0 / HEAD_A)
    yc = y - mean
    var = _head_sum(yc * yc, lane_lo) * (1.0 / HEAD_A)
    yn = yc * lax.rsqrt(var + GN_EPS) * lnx_w + lnx_b
    bonus = _head_sum(r * k2 * r_k, lane_lo)
    return (yn + bonus * v) * g, s_new


def _wkv_kernel(r_ref, k_ref, v_ref, wl_ref, a_ref, g_ref, kk_ref, ka_ref, rk_ref, lw_ref, lb_ref,
                s0_ref, y_ref, sout_ref, s_scr, mk_scr):
    b, c = pl.program_id(0), pl.program_id(1)

    @pl.when((b == 0) & (c == 0))
    def _():
        _wkv_fill_masks(mk_scr)

    @pl.when(c == 0)
    def _():
        s_scr[...] = s0_ref[0]

    n_pairs = s_scr.shape[0]
    lanes = [slice(p * LANES, (p + 1) * LANES) for p in range(n_pairs)]
    chains = [_wkv_pair_chunk(
        r_ref[0, :, sl], k_ref[0, :, sl], v_ref[0, :, sl], wl_ref[0, :, sl], a_ref[0, :, sl],
        g_ref[0, :, sl], kk_ref[:, sl], ka_ref[:, sl], rk_ref[:, sl], lw_ref[:, sl], lb_ref[:, sl],
        s_scr[p], mk_scr) for p, sl in enumerate(lanes)]
    for (y, s_new), p, sl in zip(_lockstep(chains), range(n_pairs), lanes):
        y_ref[0, :, sl] = y.astype(y_ref.dtype)
        s_scr[p] = s_new
        sout_ref[0, p] = s_new


def _wkv_call(r, k, v, wlog, a, g, k_k, k_a, r_k, lnx_w, lnx_b, s2_0):
    B, T, D = r.shape
    L, P = WKV_CHUNK, D // LANES
    assert T % L == 0 and 2 * L == LANES and D % LANES == 0
    seq = pl.BlockSpec((1, L, D), lambda b, c: (b, c, 0))
    par = pl.BlockSpec((1, D), lambda b, c: (0, 0))
    st = pl.BlockSpec((1, P, LANES, LANES), lambda b, c: (b, 0, 0, 0))
    row = lambda x: x.reshape(1, D)
    return pl.pallas_call(
        _wkv_kernel,
        grid=(B, T // L),
        in_specs=[seq] * 6 + [par] * 5 + [st],
        out_specs=[seq, st],
        out_shape=[jax.ShapeDtypeStruct((B, T, D), BF16), jax.ShapeDtypeStruct((B, P, LANES, LANES), F32)],
        scratch_shapes=[pltpu.VMEM((P, LANES, LANES), F32), pltpu.VMEM((_N_MASKS, LANES, LANES), F32)],
        compiler_params=pltpu.CompilerParams(dimension_semantics=("arbitrary", "arbitrary")),
        name="wkv7_chunk_scan",
    )(r, k, v, wlog, a, g, row(k_k), row(k_a), row(r_k), row(lnx_w), row(lnx_b), s2_0)


def _rmsnorm(x, g):
    return x * lax.rsqrt(jnp.mean(x * x, axis=-1, keepdims=True) + RMS_EPS) * g


def _rmsnorm_kernel(x_ref, g_ref, o_ref):
    o_ref[...] = _rmsnorm(x_ref[...], g_ref[...]).astype(o_ref.dtype)


def _rmsnorm_call(x, g, out_dtype=F32):
    M, D = x.shape
    bm = min(M, 512)
    return pl.pallas_call(
        _rmsnorm_kernel,
        grid=(M // bm,),
        in_specs=[pl.BlockSpec((bm, D), lambda m: (m, 0)), pl.BlockSpec((1, D), lambda m: (0, 0))],
        out_specs=pl.BlockSpec((bm, D), lambda m: (m, 0)),
        out_shape=jax.ShapeDtypeStruct((M, D), out_dtype),
        compiler_params=pltpu.CompilerParams(dimension_semantics=("parallel",)),
        name="rmsnorm",
    )(x, g.reshape(1, D))


MM_ROW_CHUNK = 256


def _mm_kernel(x_ref, w_ref, *refs, n_e, post, nk):
    e_refs = refs[:n_e]
    o_ref = refs[n_e]
    bm = o_ref.shape[0]
    acc_ref = refs[-1] if nk > 1 else None
    k = pl.program_id(2)
    if nk > 1:
        @pl.when(k == 0)
        def _():
            acc_ref[...] = jnp.zeros_like(acc_ref)

    w = w_ref[...].astype(BF16)
    chunk = min(bm, MM_ROW_CHUNK)
    for r0 in range(0, bm, chunk):
        rows = slice(r0, r0 + chunk)
        tile = lambda ref: ref[rows, :] if ref.shape[0] == bm else ref[...]
        part = jnp.dot(x_ref[rows, :].astype(BF16), w, preferred_element_type=F32)
        if nk == 1:
            o_ref[rows, :] = post(part, *[tile(e) for e in e_refs]).astype(o_ref.dtype)
        else:
            acc_ref[rows, :] += part

    if nk > 1:
        @pl.when(k == nk - 1)
        def _():
            o_ref[...] = post(acc_ref[...], *[e[...] for e in e_refs]).astype(o_ref.dtype)


def _mm_norm_kernel(x_ref, g_ref, w_ref, o_ref, x_scr, *, post):
    @pl.when(pl.program_id(1) == 0)
    def _():
        x_scr[...] = _rmsnorm(x_ref[...], g_ref[...]).astype(BF16)

    bm = o_ref.shape[0]
    w = w_ref[...].astype(BF16)
    chunk = min(bm, MM_ROW_CHUNK)
    for r0 in range(0, bm, chunk):
        rows = slice(r0, r0 + chunk)
        part = jnp.dot(x_scr[rows, :], w, preferred_element_type=F32)
        o_ref[rows, :] = post(part).astype(o_ref.dtype)


def _identity(x):
    return x


def _add_residual(acc, res):
    return res + acc


def _weight_spec(block, index, layer):
    if layer is None:
        return pl.BlockSpec(block, index)
    return pl.BlockSpec((None,) + block, lambda *ids: (layer,) + index(*ids))


def _matmul_norm_call(x, g, w, *, post=_identity, out_dtype=F32, bm=2048, bn=512, layer=None, name="matmul"):
    K, N = w.shape[-2:]
    M = x.shape[0]
    bm = min(bm, M)
    bn = min(bn, N)
    assert M % bm == 0 and N % bn == 0
    return pl.pallas_call(
        functools.partial(_mm_norm_kernel, post=post),
        grid=(M // bm, N // bn),
        in_specs=[pl.BlockSpec((bm, K), lambda m, n: (m, 0), pipeline_mode=pl.Buffered(1)),
                  pl.BlockSpec((1, K), lambda m, n: (0, 0)),
                  _weight_spec((K, bn), lambda m, n: (0, n), layer)],
        out_specs=pl.BlockSpec((bm, bn), lambda m, n: (m, n)),
        out_shape=jax.ShapeDtypeStruct((M, N), out_dtype),
        scratch_shapes=[pltpu.VMEM((bm, K), BF16)],
        compiler_params=pltpu.CompilerParams(
            dimension_semantics=("parallel", "arbitrary"), vmem_limit_bytes=VMEM_LIMIT),
        name=name,
    )(x, g.reshape(1, K), w)


def _matmul_call(x, w, extras=(), *, post=_identity, out_dtype=F32,
                 bm=1024, bn=1024, bk=None, layer=None, name="matmul"):
    K, N = w.shape[-2:]
    M = x.shape[0]
    bm = min(bm, M)
    bn = min(bn, N)
    bk = K if bk is None else min(bk, K)
    assert M % bm == 0 and N % bn == 0 and K % bk == 0
    nk = K // bk
    e_specs = [pl.BlockSpec((bm, bn), lambda n, m, k: (m, n)) if e.shape[0] == M
               else pl.BlockSpec((1, bn), lambda n, m, k: (0, n)) for e in extras]
    return pl.pallas_call(
        functools.partial(_mm_kernel, n_e=len(extras), post=post, nk=nk),
        grid=(N // bn, M // bm, nk),
        in_specs=[pl.BlockSpec((bm, bk), lambda n, m, k: (m, k)),
                  _weight_spec((bk, bn), lambda n, m, k: (k, n), layer)] + e_specs,
        out_specs=pl.BlockSpec((bm, bn), lambda n, m, k: (m, n)),
        out_shape=jax.ShapeDtypeStruct((M, N), out_dtype),
        scratch_shapes=[pltpu.VMEM((bm, bn), F32)] if nk > 1 else [],
        compiler_params=pltpu.CompilerParams(
            dimension_semantics=("parallel", "parallel", "arbitrary"),
            vmem_limit_bytes=VMEM_LIMIT),
        name=name,
    )(x, w, *extras)


ATT_BLOCK = ATT_STEPS * max(d for _, d in GROUPS)
ATT_LANES = 4


def _attn_prompt_kernel(slopes_ref, *refs):
    ng = N_GROUPS
    q_refs, kc_refs, vc_refs = refs[0:ng], refs[ng:2 * ng], refs[2 * ng:3 * ng]
    kp_refs, vp_refs = refs[3 * ng:4 * ng], refs[4 * ng:5 * ng]
    o_ref = refs[5 * ng]
    m_scr, l_scr, acc_scr = refs[5 * ng + 1:]
    hp = pl.program_id(1)
    j = pl.program_id(2)
    S = ATT_STEPS
    lane_lo = lax.broadcasted_iota(jnp.int32, (S, LANES), 1) < HD_B
    iq = lax.broadcasted_iota(jnp.int32, (S, 2 * S), 0)
    jk = lax.broadcasted_iota(jnp.int32, (S, 2 * S), 1)
    steps = iq + S - jk
    mask = jnp.where((steps >= 0) & (steps <= S), 0.0, -jnp.inf)
    scale = HD_B ** -0.5
    no_prev = jnp.where(j > 0, 0.0, -jnp.inf)

    order = sorted(range(ng), key=lambda g: -GROUPS[g][1])
    for g in order:
        dil = GROUPS[g][1]
        biases = [mask - slopes_ref[g * H_B + 2 * hp + h] * (steps * dil).astype(F32) for h in range(2)]
        q_ref, kc_ref, vc_ref, kp_ref, vp_ref = q_refs[g], kc_refs[g], vc_refs[g], kp_refs[g], vp_refs[g]

        def attend(cur_start, prev_ref, prev_start, prev_bias, dil=dil, g=g, biases=biases, q_ref=q_ref,
                   kc_ref=kc_ref, vc_ref=vc_ref):
            cur = pl.ds(cur_start, S, stride=dil) if dil > 1 else pl.ds(cur_start, S)
            prv = pl.ds(prev_start, S, stride=dil) if dil > 1 else pl.ds(prev_start, S)
            q, kc, vc = q_ref[0, cur, :], kc_ref[0, cur, :], vc_ref[0, cur, :]
            kp, vp = prev_ref[0][0, prv, :], prev_ref[1][0, prv, :]
            qs = [jnp.where(lane_lo, q, 0.0), jnp.where(lane_lo, 0.0, q)]
            scores = [(_bdot_nt(qh, kp), _bdot_nt(qh, kc)) for qh in qs]
            yield
            ms, ls, probs = [], [], []
            for h in range(2):
                sp = scores[h][0] * scale + biases[h][:, :S]
                if prev_bias is not None:
                    sp = sp + prev_bias
                sc = scores[h][1] * scale + biases[h][:, S:]
                m = jnp.max(jnp.maximum(sp, sc), axis=-1, keepdims=True)
                pp = jnp.exp(sp - m)
                pc = jnp.exp(sc - m)
                ls.append(jnp.sum(pp + pc, axis=-1, keepdims=True))
                probs.append((pp, pc))
                ms.append(m)
            accs = [_bdot(pp, vp) + _bdot(pc, vc) for pp, pc in probs]
            yield
            m = jnp.where(lane_lo, ms[0], ms[1])
            l = jnp.where(lane_lo, ls[0], ls[1])
            acc = jnp.where(lane_lo, accs[0], accs[1])
            if g != order[0]:
                m_old = m_scr[cur, :]
                m_new = jnp.maximum(m_old, m)
                w_old, w_new = jnp.exp(m_old - m_new), jnp.exp(m - m_new)
                l = l_scr[cur, :] * w_old + l * w_new
                acc = acc_scr[cur, :] * w_old + acc * w_new
                m = m_new
            m_scr[cur, :] = m
            l_scr[cur, :] = l
            acc_scr[cur, :] = acc

        sd = S * dil
        n_sub = ATT_BLOCK // sd
        run = lambda items, attend=attend: _lockstep([attend(*it) for it in items])
        first = lambda c, kp_ref=kp_ref, vp_ref=vp_ref: (c, (kp_ref, vp_ref), c, no_prev)
        later = lambda c, i, sd=sd, kc_ref=kc_ref, vc_ref=vc_ref: (
            i * sd + c, (kc_ref, vc_ref), (i - 1) * sd + c, None)

        def loop(lo, hi, fn):
            def body(idx, carry):
                fn(idx)
                return carry
            lax.fori_loop(lo, hi, body, 0)

        W = 2 * ATT_LANES if dil == 1 else min(ATT_LANES, dil)
        if dil == 1:
            hb = n_sub // W
            run([first(0)] + [later(0, k * hb) for k in range(1, W)])
            loop(1, hb, lambda i, hb=hb: run([later(0, i + k * hb) for k in range(W)]))
        else:
            cs = dil // W
            def streams(c, cs=cs, n_sub=n_sub):
                run([first(c + k * cs) for k in range(W)])
                if n_sub > 1:
                    loop(1, n_sub, lambda i: run([later(c + k * cs, i) for k in range(W)]))
            if cs == 1:
                streams(0)
            else:
                loop(0, cs, streams)

    o_ref[0] = (acc_scr[...] / l_scr[...]).astype(o_ref.dtype)


def _attn_prompt_call(qkv, slopes):
    B, T, W = qkv.shape
    hw = H_B * HD_B
    n_hp = H_B // 2
    TB = ATT_BLOCK
    assert T % TB == 0
    col = lambda g, which, hp: (g * 3 * hw + which * hw) // LANES + hp
    cur = lambda g, which: pl.BlockSpec((1, TB, LANES), lambda b, hp, j: (b, j, col(g, which, hp)))
    def prev(g, which):
        rows = ATT_STEPS * GROUPS[g][1]
        per = TB // rows
        return pl.BlockSpec((1, rows, LANES),
                            lambda b, hp, j: (b, jnp.maximum(j * per - 1, 0), col(g, which, hp)))
    gs = range(N_GROUPS)
    in_specs = ([pl.BlockSpec(memory_space=pltpu.SMEM)]
                + [cur(g, 0) for g in gs] + [cur(g, 1) for g in gs] + [cur(g, 2) for g in gs]
                + [prev(g, 1) for g in gs] + [prev(g, 2) for g in gs])
    return pl.pallas_call(
        _attn_prompt_kernel,
        grid=(B, n_hp, T // TB),
        in_specs=in_specs,
        out_specs=pl.BlockSpec((1, TB, LANES), lambda b, hp, j: (b, j, hp)),
        out_shape=jax.ShapeDtypeStruct((B, T, hw), BF16),
        scratch_shapes=[pltpu.VMEM((TB, LANES), F32)] * 3,
        compiler_params=pltpu.CompilerParams(
            dimension_semantics=("parallel", "parallel", "arbitrary"), vmem_limit_bytes=VMEM_LIMIT),
        name="attn_prompt",
    )(slopes, *([qkv] * (5 * N_GROUPS)))


def _kv_window_kernel(x_ref, o_ref):
    for hp in range(H_B // 2):
        sl = slice(hp * LANES, (hp + 1) * LANES)
        o_ref[0, 0, sl, :] = x_ref[0, :, sl].T


def _kv_window_call(qkv, g, window):
    B, T, _ = qkv.shape
    hw = H_B * HD_B
    wb = min(window, 512)
    first = (T - window) // wb
    out = pl.pallas_call(
        _kv_window_kernel,
        grid=(B, 2, window // wb),
        in_specs=[pl.BlockSpec((1, wb, hw), lambda b, kv, i: (b, first + i, 3 * g + 1 + kv))],
        out_specs=pl.BlockSpec((1, 1, hw, wb), lambda b, kv, i: (b, kv, 0, i)),
        out_shape=jax.ShapeDtypeStruct((B, 2, hw, window), F32),
        compiler_params=pltpu.CompilerParams(dimension_semantics=("parallel", "parallel", "parallel")),
        name=f"kv_window_g{g}",
    )(qkv)
    return out.reshape(B, 2, H_B, HD_B, window).transpose(0, 4, 1, 2, 3)


def _roll_append(old, new, o_ref, kv):
    R, W = old.shape
    S = new.shape[0]
    rolled = pltpu.roll(old, W - S, 1)
    new_t = jnp.concatenate([jnp.zeros((LANES - S, R), F32), new], axis=0).T
    if W > LANES:
        o_ref[0, kv, :, :W - LANES] = rolled[:, :W - LANES]
    lane = lax.broadcasted_iota(jnp.int32, (R, LANES), 1)
    o_ref[0, kv, :, W - LANES:] = jnp.where(lane >= LANES - S, new_t, rolled[:, W - LANES:])


def _attn_decode_kernel(slopes_ref, *refs):
    ng = N_GROUPS
    q_refs, k_refs, v_refs = refs[0:ng], refs[ng:2 * ng], refs[2 * ng:3 * ng]
    kt_refs, vt_refs = refs[3 * ng:4 * ng], refs[4 * ng:5 * ng]
    o_ref = refs[5 * ng]
    roll_refs = refs[5 * ng + 1:6 * ng + 1]
    hp = pl.program_id(1)
    S = q_refs[0].shape[1]
    lane_lo = lax.broadcasted_iota(jnp.int32, (S, LANES), 1) < HD_B
    scale = HD_B ** -0.5

    for kv, (new_refs, old_refs) in enumerate(((k_refs, kt_refs), (v_refs, vt_refs))):
        for g in range(ng):
            _roll_append(old_refs[g][0, 0], new_refs[g][0], roll_refs[g], kv)

    def chain(g, h):
        window, dil = GROUPS[g]
        slope = slopes_ref[g * H_B + 2 * hp + h]
        q = q_refs[g][0]
        qh = jnp.where(lane_lo, q, 0.0) if h == 0 else jnp.where(lane_lo, 0.0, q)
        s_old = _bdot(qh, kt_refs[g][0, 0])
        s_new = _bdot_nt(qh, k_refs[g][0])
        yield
        j_o = lax.broadcasted_iota(jnp.int32, (S, window), 0)
        w_o = lax.broadcasted_iota(jnp.int32, (S, window), 1)
        dist_o = window + j_o - w_o
        ok_o = ((dist_o & (dil - 1)) == 0) & (dist_o <= window)
        j_n = lax.broadcasted_iota(jnp.int32, (S, S), 0)
        dist_n = j_n - lax.broadcasted_iota(jnp.int32, (S, S), 1)
        ok_n = ((dist_n & (dil - 1)) == 0) & (dist_n >= 0)
        s_old = jnp.where(ok_o, s_old * scale - slope * dist_o.astype(F32), -jnp.inf)
        s_new = jnp.where(ok_n, s_new * scale - slope * dist_n.astype(F32), -jnp.inf)
        m = jnp.maximum(jnp.max(s_old, axis=-1, keepdims=True), jnp.max(s_new, axis=-1, keepdims=True))
        p_old = jnp.exp(s_old - m)
        p_new = jnp.exp(s_new - m)
        l = jnp.sum(p_old, axis=-1, keepdims=True) + jnp.sum(p_new, axis=-1, keepdims=True)
        acc = _bdot_nt(p_old, vt_refs[g][0, 0]) + _bdot(p_new, v_refs[g][0])
        yield
        return m, l, acc

    res = _lockstep([chain(g, h) for g in range(ng) for h in range(2)])
    m = l = acc = None
    for g in range(ng):
        (m0, l0, a0), (m1, l1, a1) = res[2 * g], res[2 * g + 1]
        mg = jnp.where(lane_lo, m0, m1)
        lg = jnp.where(lane_lo, l0, l1)
        ag = jnp.where(lane_lo, a0, a1)
        if g == 0:
            m, l, acc = mg, lg, ag
        else:
            m_new = jnp.maximum(m, mg)
            w_old, w_new = jnp.exp(m - m_new), jnp.exp(mg - m_new)
            l = l * w_old + lg * w_new
            acc = acc * w_old + ag * w_new
            m = m_new
    o_ref[0] = (acc / l).astype(o_ref.dtype)


def _attn_decode_call(qkv, caches, slopes):
    B, S, _ = qkv.shape
    hw = H_B * HD_B
    n_hp = H_B // 2
    assert all(w >= S and d & (d - 1) == 0 for w, d in GROUPS)
    cache_t = [c.transpose(0, 2, 3, 4, 1).reshape(B, 2, hw, c.shape[1]) for c in caches]
    col = lambda g, which, hp: (g * 3 * hw + which * hw) // LANES + hp
    new = lambda g, which: pl.BlockSpec((1, S, LANES), lambda b, hp: (b, 0, col(g, which, hp)))
    old = lambda g, kv: pl.BlockSpec((1, 1, LANES, GROUPS[g][0]), lambda b, hp: (b, kv, hp, 0))
    gs = range(N_GROUPS)
    in_specs = ([pl.BlockSpec(memory_space=pltpu.SMEM)]
                + [new(g, 0) for g in gs] + [new(g, 1) for g in gs] + [new(g, 2) for g in gs]
                + [old(g, 0) for g in gs] + [old(g, 1) for g in gs])
    outs = pl.pallas_call(
        _attn_decode_kernel,
        grid=(B, n_hp),
        in_specs=in_specs,
        out_specs=([pl.BlockSpec((1, S, LANES), lambda b, hp: (b, 0, hp))]
                   + [pl.BlockSpec((1, 2, LANES, w), lambda b, hp: (b, 0, hp, 0)) for w, _ in GROUPS]),
        out_shape=([jax.ShapeDtypeStruct((B, S, hw), BF16)]
                   + [jax.ShapeDtypeStruct((B, 2, hw, w), F32) for w, _ in GROUPS]),
        compiler_params=pltpu.CompilerParams(dimension_semantics=("parallel", "parallel")),
        name="attn_decode",
    )(slopes, *([qkv] * (3 * N_GROUPS)), *cache_t, *cache_t)
    rolled = [r.reshape(B, 2, H_B, HD_B, r.shape[-1]).transpose(0, 4, 1, 2, 3) for r in outs[1:]]
    return outs[0], rolled


def _alibi_slopes():
    n = N_GROUPS * H_B
    return 2.0 ** (-8.0 * jnp.arange(1, n + 1, dtype=F32) / n)


def _pair_states(s):
    B, H = s.shape[:2]
    s = s.reshape(B, H // 2, 2, HEAD_A, HEAD_A)
    z = jnp.zeros_like(s[:, :, 0])
    top = jnp.concatenate([s[:, :, 0], z], axis=-1)
    bot = jnp.concatenate([z, s[:, :, 1]], axis=-1)
    return jnp.concatenate([top, bot], axis=-2)


def _unpair_states(s2):
    B, P = s2.shape[:2]
    return jnp.stack([s2[:, :, :HEAD_A, :HEAD_A], s2[:, :, HEAD_A:, HEAD_A:]],
                     axis=2).reshape(B, 2 * P, HEAD_A, HEAD_A)


def _softplus(x):
    return jnp.maximum(x, 0.0) + jnp.log(1.0 + jnp.exp(-jnp.abs(x)))


def _sigmoid(x):
    return 0.5 * (1.0 + jnp.tanh(0.5 * x))


def _norm_mix_kernel(x_ref, g_ref, mu_ref, shift_ref, w0_ref, w1_ref, w2_ref, a0_ref, a1_ref, a2_ref,
                     g1_ref, g2_ref, xr_ref, xk_ref, xv_ref, wlog_ref, a_ref, gate_ref, hlast_ref, carry_scr,
                     *, tiles_per_seq):
    @pl.when(pl.program_id(0) == 0)
    def _():
        carry_scr[...] = jnp.zeros_like(carry_scr)

    x = x_ref[0]
    h = x * lax.rsqrt(jnp.mean(x * x, axis=-1, keepdims=True) + RMS_EPS) * g_ref[...]
    rows = h.shape[0]
    first_tile = pl.program_id(0) % tiles_per_seq == 0
    row0 = jnp.where(first_tile, shift_ref[0], carry_scr[...])
    prev = pltpu.roll(h, 1, 0) if rows > 1 else h
    prev = jnp.where(lax.broadcasted_iota(jnp.int32, h.shape, 0) == 0, row0, prev)
    carry_scr[...] = h[rows - 1:rows, :]
    hlast_ref[0] = h[rows - 1:rows, :]
    diff = prev - h
    mix = lambda i: h + diff * mu_ref[i:i + 1, :]
    xr_ref[0] = mix(0).astype(BF16)
    xk_ref[0] = mix(2).astype(BF16)
    xv_ref[0] = mix(3).astype(BF16)
    w_pre = w0_ref[...] + _bdot(jnp.tanh(_bdot(mix(1), w1_ref[...])), w2_ref[...])
    wlog_ref[0] = -_softplus(-w_pre) - 0.5
    a_ref[0] = _sigmoid(a0_ref[...] + _bdot(_bdot(mix(4), a1_ref[...]), a2_ref[...]))
    gate_ref[0] = _bdot(_sigmoid(_bdot(mix(5), g1_ref[...])), g2_ref[...])


def _norm_mix_call(x, g, mu, shift0, w0, w1, w2, a0, a1, a2, g1, g2, bt=256):
    B, T, D = x.shape
    bt = min(bt, T)
    assert T % bt == 0
    tiles = T // bt
    seq = pl.BlockSpec((1, bt, D), lambda m: (m // tiles, m % tiles, 0))
    per_seq = pl.BlockSpec((1, 1, D), lambda m: (m // tiles, 0, 0))
    whole = lambda z: pl.BlockSpec(z.shape, lambda m: (0,) * z.ndim)
    row = lambda z: z.reshape(1, D)
    consts = [row(g), mu]
    bf = lambda z: z.astype(BF16)
    lora = [row(w0), bf(w1), bf(w2), row(a0), bf(a1), bf(a2), bf(g1), bf(g2)]
    outs = pl.pallas_call(
        functools.partial(_norm_mix_kernel, tiles_per_seq=tiles),
        grid=(B * tiles,),
        in_specs=[seq] + [whole(z) for z in consts] + [per_seq] + [whole(z) for z in lora],
        out_specs=[seq] * 6 + [per_seq],
        out_shape=([jax.ShapeDtypeStruct((B, T, D), BF16)] * 3 + [jax.ShapeDtypeStruct((B, T, D), F32)] * 3
                   + [jax.ShapeDtypeStruct((B, 1, D), F32)]),
        scratch_shapes=[pltpu.VMEM((1, D), F32)],
        compiler_params=pltpu.CompilerParams(dimension_semantics=("arbitrary",),
                                             vmem_limit_bytes=VMEM_LIMIT),
        name="rwkv_norm_mix",
    )(x, *consts, shift0.reshape(B, 1, D), *lora)
    return [o.reshape(B * T, D) for o in outs[:6]], outs[6].reshape(B, D)


def _rwkv_block(x, shift0, wkv0, norm_g, p):
    (mu, w0, w1, w2, a0, a1, a2, g1, g2, k_k, k_a, r_k, lnx_w, lnx_b, w_r, w_k, w_v, w_o) = p
    B, T, D = x.shape
    M = B * T
    x2 = x.reshape(M, D)
    (xr, xk, xv, wlog, a, g), h_last = _norm_mix_call(x, norm_g, mu, shift0, w0, w1, w2, a0, a1, a2, g1, g2)
    r = _matmul_call(xr, w_r, name="rwkv_r")
    k = _matmul_call(xk, w_k, name="rwkv_k")
    v = _matmul_call(xv, w_v, name="rwkv_v")

    L = WKV_CHUNK
    Tp = -(-T // L) * L
    def seq(z, fill=0.0):
        z = z.reshape(B, T, D)
        if Tp != T:
            z = jnp.pad(z, ((0, 0), (0, Tp - T), (0, 0)), constant_values=fill)
        return z
    y, s2 = _wkv_call(seq(r), seq(k), seq(v), seq(wlog, -jnp.inf), seq(a), seq(g),
                      k_k, k_a, r_k.reshape(D), lnx_w, lnx_b, _pair_states(wkv0))
    y2 = y[:, :T].reshape(M, D)
    out = _matmul_call(y2, w_o, [x2], post=_add_residual, name="rwkv_o")
    return out.reshape(B, T, D), _unpair_states(s2), h_last


def _ffn_block(x2, norm_g, w_up, w_down, layer):
    mid = _matmul_norm_call(x2, norm_g, w_up, post=lambda acc: jnp.square(jnp.maximum(acc, 0.0)),
                            out_dtype=BF16, layer=layer, name="ffn_up")
    return _matmul_call(mid, w_down, [x2], post=_add_residual, bk=2048, layer=layer, name="ffn_down")


def _attn_block(x, norm_g, w_qkv, w_o, slopes, caches=None):
    B, T, D = x.shape
    M = B * T
    x2 = x.reshape(M, D)
    qkv = _matmul_norm_call(x2, norm_g, w_qkv, name="attn_qkv").reshape(B, T, -1)
    hw = H_B * HD_B
    if caches is None:
        merged = _attn_prompt_call(qkv, slopes)
        bufs = [_kv_window_call(qkv, g, min(w, T)) for g, (w, _) in enumerate(GROUPS)]
    else:
        merged, bufs = _attn_decode_call(qkv, caches, slopes)
    out = _matmul_call(merged.reshape(M, hw), w_o, [x2], post=_add_residual, name="attn_o")
    return out.reshape(B, T, D), bufs


def kernel(x_prompt, x_sample, state_wkv, state_shift, cache_kv_g1, cache_kv_g2, cache_kv_g3,
           norm_mix, norm_ffn, norm_final,
           rwkv_mu, rwkv_w0, rwkv_w1, rwkv_w2, rwkv_a0, rwkv_a1, rwkv_a2, rwkv_g1, rwkv_g2,
           rwkv_k_k, rwkv_k_a, rwkv_r_k, rwkv_lnx_w, rwkv_lnx_b, rwkv_w_r, rwkv_w_k, rwkv_w_v, rwkv_w_o,
           attn_w_qkv, attn_w_o, ffn_w_up, ffn_w_down):
    rwkv_params = (rwkv_mu, rwkv_w0, rwkv_w1, rwkv_w2, rwkv_a0, rwkv_a1, rwkv_a2, rwkv_g1, rwkv_g2,
                   rwkv_k_k, rwkv_k_a, rwkv_r_k, rwkv_lnx_w, rwkv_lnx_b,
                   rwkv_w_r, rwkv_w_k, rwkv_w_v, rwkv_w_o)
    slopes = _alibi_slopes()
    Bp, Tp, D = x_prompt.shape
    Bs, Ts, _ = x_sample.shape

    def ffn(x, i):
        B, T, _ = x.shape
        return _ffn_block(x.reshape(B * T, D), norm_ffn[i], ffn_w_up, ffn_w_down, i).reshape(B, T, D)

    xp, wkv_p, shift_p = _rwkv_block(x_prompt, jnp.zeros((Bp, D), F32),
                                     jnp.zeros((Bp, H_A, HEAD_A, HEAD_A), F32), norm_mix[0], rwkv_params)
    xs, wkv_s, shift_s = _rwkv_block(x_sample, state_shift, state_wkv, norm_mix[0], rwkv_params)
    xp, xs = ffn(xp, 0), ffn(xs, 0)
    xp, (kv1_p, kv2_p, kv3_p) = _attn_block(xp, norm_mix[1], attn_w_qkv, attn_w_o, slopes)
    xs, (kv1_s, kv2_s, kv3_s) = _attn_block(xs, norm_mix[1], attn_w_qkv, attn_w_o, slopes,
                                            (cache_kv_g1, cache_kv_g2, cache_kv_g3))
    xp, xs = ffn(xp, 1), ffn(xs, 1)
    y_prompt = _rmsnorm_call(xp.reshape(Bp * Tp, D), norm_final).reshape(Bp, Tp, D)
    y_sample = _rmsnorm_call(xs.reshape(Bs * Ts, D), norm_final).reshape(Bs, Ts, D)
    return (y_prompt, y_sample, wkv_p, shift_p, kv1_p, kv2_p, kv3_p,
            wkv_s, shift_s, kv1_s, kv2_s, kv3_s)
```
